```python
import math
import jax, jax.numpy as jnp
from jax import lax
import numpy as np

D_MODEL = 1024
BATCH = 4
SEQ = 4096
DEPTH = 4

CHUNK = 64
N_META = 16
SB_BLOCK = 128
HEAD_DIM = 64
SB_HEADS = D_MODEL // 128
RW_HEADS = D_MODEL // 128
C_SB = SB_HEADS * HEAD_DIM
C_RW = RW_HEADS * HEAD_DIM
W_LORA = 64
A_LORA = 64
V_LORA = 32
G_LORA = 160
RW_COLS = 3 * C_RW + W_LORA + A_LORA + G_LORA
IN_COLS = 3 * C_SB + RW_COLS + 2 * D_MODEL
FFN_HIDDEN = ((8 * D_MODEL // 3 + 255) // 256) * 256
RMS_EPS = 1e-6
GN_EPS = 64e-5

kernel_name = "hybrid_stickbreaking_rwkv7_gated_trunk"


def rms_norm(x, g):
    xf = x.astype(jnp.float32)
    y = xf * lax.rsqrt(jnp.mean(xf * xf, axis=-1, keepdims=True) + RMS_EPS)
    return (y * g.astype(jnp.float32)).astype(x.dtype)


def split_heads(t, n_heads):
    b, l, _ = t.shape
    return t.reshape(b, l, n_heads, HEAD_DIM)


def token_shift(p, mu):
    prev = jnp.pad(p, ((0, 0), (1, 0), (0, 0)))[:, :-1]
    return p + (prev - p) * mu


def stick_breaking_attention(q, k, v):
    b, l, h, dh = q.shape
    n_blocks = l // SB_BLOCK
    q_blocks = q.reshape(b, n_blocks, SB_BLOCK, h, dh).transpose(1, 0, 2, 3, 4)
    k_pos = jnp.arange(l)
    scale = dh ** -0.5

    def one_block(args):
        i, q_blk = args
        z = jnp.einsum("bqhd,bkhd->bhqk", q_blk, k).astype(jnp.float32) * scale
        q_pos = i * SB_BLOCK + jnp.arange(SB_BLOCK)
        mask = k_pos[None, :] < q_pos[:, None]
        neg_log_one_minus_beta = jnp.where(mask, jax.nn.softplus(z), 0.0)
        between = lax.cumsum(neg_log_one_minus_beta, axis=3, reverse=True) - neg_log_one_minus_beta
        log_a = jax.nn.log_sigmoid(z) - between
        a = jnp.where(mask, jnp.exp(log_a), 0.0)
        return jnp.einsum("bhqk,bkhd->bqhd", a.astype(v.dtype), v)

    out = lax.map(one_block, (jnp.arange(n_blocks), q_blocks))
    return out.transpose(1, 0, 2, 3, 4).reshape(b, l, h, dh)


def rwkv7_recurrence(r, decay, k, v, kk, a):
    b, l, h, n = r.shape
    f32 = jnp.float32

    def step(s, inp):
        r_t, w_t, k_t, v_t, kk_t, a_t = inp
        s_kk = jnp.einsum("bhvk,bhk->bhv", s, -kk_t)
        s = (s * w_t[:, :, None, :]
             + s_kk[..., None] * (kk_t * a_t)[:, :, None, :]
             + v_t[..., None] * k_t[:, :, None, :])
        y = jnp.einsum("bhvk,bhk->bhv", s, r_t)
        return s, y

    xs = tuple(jnp.moveaxis(t.astype(f32), 1, 0) for t in (r, decay, k, v, kk, a))
    s0 = jnp.zeros((b, h, n, n), f32)
    _, ys = lax.scan(step, s0, xs)
    return jnp.moveaxis(ys, 0, 1)


def group_norm_heads(y, w, b):
    mean = jnp.mean(y, axis=-1, keepdims=True)
    var = jnp.mean(jnp.square(y - mean), axis=-1, keepdims=True)
    yn = (y - mean) * lax.rsqrt(var + GN_EPS)
    bb, l, h, n = y.shape
    return yn.reshape(bb, l, h * n) * w.astype(jnp.float32) + b.astype(jnp.float32)


def setup_inputs(seed: int = 0) -> dict:
    key = jax.random.key(seed)
    ks = jax.random.split(key, 32)
    f32 = jnp.float32
    n_vres = DEPTH - 1

    def nrm(k, shape, scale):
        return jax.random.normal(k, shape, f32) * scale

    def gain(k, shape):
        return 1.0 + 0.05 * jax.random.normal(k, shape, f32)

    return {
        "x": nrm(ks[0], (BATCH, SEQ, D_MODEL), 1.0),
        "meta_tokens": nrm(ks[1], (N_META, D_MODEL), 1.0),
        "norm_mix": gain(ks[2], (DEPTH, D_MODEL)),
        "norm_ffn": gain(ks[3], (DEPTH, D_MODEL)),
        "norm_final": gain(ks[4], (D_MODEL,)),
        "w_in": nrm(ks[5], (DEPTH, D_MODEL, IN_COLS), D_MODEL ** -0.5),
        "mu_rw": jax.random.uniform(ks[6], (DEPTH, RW_COLS), f32, 0.2, 0.8),
        "w0": jax.random.uniform(ks[7], (DEPTH, C_RW), f32, -6.5, -1.5),
        "w_up": nrm(ks[8], (DEPTH, W_LORA, C_RW), W_LORA ** -0.5),
        "a0": nrm(ks[9], (DEPTH, C_RW), 0.1),
        "a_up": nrm(ks[10], (DEPTH, A_LORA, C_RW), A_LORA ** -0.5),
        "g_up": nrm(ks[11], (DEPTH, G_LORA, C_RW), G_LORA ** -0.5),
        "k_k": 0.85 + 0.05 * jax.random.normal(ks[12], (DEPTH, C_RW), f32),
        "k_a": gain(ks[13], (DEPTH, C_RW)),
        "r_k": nrm(ks[14], (DEPTH, RW_HEADS, HEAD_DIM), 0.1),
        "ln_x_w": gain(ks[15], (DEPTH, C_RW)),
        "ln_x_b": nrm(ks[16], (DEPTH, C_RW), 0.02),
        "vres_down": nrm(ks[17], (n_vres, D_MODEL, V_LORA), D_MODEL ** -0.5),
        "vres_mu": jax.random.uniform(ks[18], (n_vres, V_LORA), f32, 0.2, 0.8),
        "vres_up": nrm(ks[19], (n_vres, V_LORA, C_RW), V_LORA ** -0.5),
        "vres0": gain(ks[20], (n_vres, C_RW)),
        "w_sb_out": nrm(ks[21], (DEPTH, C_SB, D_MODEL), C_SB ** -0.5),
        "w_rw_out": nrm(ks[22], (DEPTH, C_RW, D_MODEL), C_RW ** -0.5),
        "w_out": nrm(ks[23], (DEPTH, D_MODEL, D_MODEL), D_MODEL ** -0.5),
        "w_ffn_in": nrm(ks[24], (DEPTH, D_MODEL, 2 * FFN_HIDDEN), D_MODEL ** -0.5),
        "w_ffn_out": nrm(ks[25], (DEPTH, FFN_HIDDEN, D_MODEL), FFN_HIDDEN ** -0.5),
    }


def reference(x, meta_tokens, norm_mix, norm_ffn, norm_final, w_in, mu_rw, w0, w_up, a0, a_up,
              g_up, k_k, k_a, r_k, ln_x_w, ln_x_b, vres_down, vres_mu, vres_up, vres0,
              w_sb_out, w_rw_out, w_out, w_ffn_in, w_ffn_out):
    b, s, d = x.shape
    l_real = N_META + s
    l_pad = -(-l_real // SB_BLOCK) * SB_BLOCK
    meta = jnp.broadcast_to(meta_tokens.astype(x.dtype)[None], (b, N_META, d))
    h = jnp.concatenate([meta, x], axis=1)
    h = jnp.pad(h, ((0, 0), (0, l_pad - l_real), (0, 0)))

    in_splits = [C_SB, 2 * C_SB, 3 * C_SB, 3 * C_SB + RW_COLS, 3 * C_SB + RW_COLS + D_MODEL]
    rw_splits = [C_RW, 2 * C_RW, 3 * C_RW, 3 * C_RW + W_LORA, 3 * C_RW + W_LORA + A_LORA]
    v_first = None
    for layer in range(DEPTH):
        xn = rms_norm(h, norm_mix[layer])
        proj = xn @ w_in[layer]
        q_sb, k_sb, v_sb, rw, gate_sb, gate_rw = jnp.split(proj, in_splits, axis=-1)

        y_sb = stick_breaking_attention(split_heads(q_sb, SB_HEADS), split_heads(k_sb, SB_HEADS),
                                        split_heads(v_sb, SB_HEADS)).reshape(b, l_pad, C_SB)

        rw = token_shift(rw, mu_rw[layer])
        r, kr, vr, w_lo, a_lo, g_lo = jnp.split(rw, rw_splits, axis=-1)
        w_log = -jax.nn.softplus(-(w0[layer] + jnp.tanh(w_lo) @ w_up[layer])) - 0.5
        decay = jnp.exp(-jnp.exp(w_log.astype(jnp.float32)))
        a = jax.nn.sigmoid(a0[layer] + a_lo @ a_up[layer])
        g = jax.nn.sigmoid(g_lo) @ g_up[layer]
        if layer == 0:
            v_first = vr
        else:
            v_lo = token_shift(xn @ vres_down[layer - 1], vres_mu[layer - 1])
            vr = vr + (v_first - vr) * jax.nn.sigmoid(vres0[layer - 1] + v_lo @ vres_up[layer - 1])
        kk = split_heads((kr * k_k[layer]).astype(jnp.float32), RW_HEADS)
        kk = kk / jnp.maximum(jnp.linalg.norm(kk, axis=-1, keepdims=True), 1e-12)
        kr = kr * (1.0 + (a - 1.0) * k_a[layer])
        r_h, k_h, v_h, a_h = (split_heads(t, RW_HEADS) for t in (r, kr, vr, a))
        y_rw = rwkv7_recurrence(r_h, split_heads(decay, RW_HEADS), k_h, v_h, kk, a_h)
        y_rw = group_norm_heads(y_rw, ln_x_w[layer], ln_x_b[layer])
        bonus = jnp.sum(r_h * k_h * r_k[layer], axis=-1, keepdims=True) * v_h
        y_rw = ((y_rw + bonus.reshape(b, l_pad, C_RW).astype(jnp.float32)).astype(h.dtype)) * g

        merged = (jax.nn.sigmoid(gate_sb) * (y_sb @ w_sb_out[layer])
                  + jax.nn.sigmoid(gate_rw) * (y_rw @ w_rw_out[layer]))
        h = h + merged @ w_out[layer]

        hn = rms_norm(h, norm_ffn[layer])
        gate_ffn, up_ffn = jnp.split(hn @ w_ffn_in[layer], [FFN_HIDDEN], axis=-1)
        h = h + (jax.nn.silu(gate_ffn) * up_ffn) @ w_ffn_out[layer]

    h = rms_norm(h, norm_final)
    return h[:, N_META:N_META + s]
```

```python
import functools

import jax
import jax.numpy as jnp
from jax import lax
from jax.experimental import pallas as pl
from jax.experimental.pallas import tpu as pltpu

D_MODEL = 1024
HEAD_DIM = 64
N_HEADS = 8
C_MIX = N_HEADS * HEAD_DIM
N_META = 16
SB_BLOCK = 128
W_LORA, A_LORA, V_LORA, G_LORA = 64, 64, 32, 160
RW_COLS = 3 * C_MIX + W_LORA + A_LORA + G_LORA
FFN_HIDDEN = 2816
RMS_EPS = 1e-6
GN_EPS = 64e-5
RW_CHUNK = 64

COL_QKV = 0
COL_RW = 1536
COL_GATES = 3072
COL_LORA = 5120
N_PROJ = 5632
LORA_USED = W_LORA + A_LORA + G_LORA + V_LORA

V7X_VMEM_LIMIT = 56 * 1024 * 1024

F32 = jnp.float32
BF16 = jnp.bfloat16


def _pick(n, cands):
    for c in cands:
        if n % c == 0:
            return c
    raise ValueError(f"no tile for {n} in {cands}")


def _params(*sem):
    return pltpu.CompilerParams(dimension_semantics=sem, vmem_limit_bytes=V7X_VMEM_LIMIT)


def _mm(a, b):
    return jnp.dot(a.astype(BF16), b.astype(BF16), preferred_element_type=F32)


def _mm_nt(a, b):
    return lax.dot_general(a.astype(BF16), b.astype(BF16), (((1,), (1,)), ((), ())),
                           preferred_element_type=F32)


def _mm_tn(a, b):
    return lax.dot_general(a.astype(BF16), b.astype(BF16), (((0,), (0,)), ((), ())),
                           preferred_element_type=F32)


def _split_dot(x, w_bf16):
    hi = x.astype(BF16)
    lo = (x - hi.astype(F32)).astype(BF16)
    return (jnp.dot(hi, w_bf16, preferred_element_type=F32)
            + jnp.dot(lo, w_bf16, preferred_element_type=F32))


def _softplus(u):
    return jnp.maximum(u, 0.0) + jnp.log(1.0 + jnp.exp(-jnp.abs(u)))


def _sigmoid(u):
    return 1.0 / (1.0 + jnp.exp(-u))


def _norm_proj_kernel(h_ref, g_ref, w_ref, o_ref, xn_ref):
    @pl.when(pl.program_id(1) == 0)
    def _():
        x = h_ref[...]
        ms = jnp.mean(x * x, axis=-1, keepdims=True)
        xn_ref[...] = (x * lax.rsqrt(ms + RMS_EPS) * g_ref[...]).astype(BF16)

    o_ref[...] = jnp.dot(xn_ref[...], w_ref[...], preferred_element_type=F32)


def _norm_proj(h2, gain, w_cat):
    m = h2.shape[0]
    n = w_cat.shape[1]
    tm = _pick(m, (1536, 1024, 512, 256, 128))
    tn = 512
    return pl.pallas_call(
        _norm_proj_kernel,
        grid=(m // tm, n // tn),
        in_specs=[pl.BlockSpec((tm, D_MODEL), lambda i, j: (i, 0)),
                  pl.BlockSpec((1, D_MODEL), lambda i, j: (0, 0)),
                  pl.BlockSpec((D_MODEL, tn), lambda i, j: (0, j))],
        out_specs=pl.BlockSpec((tm, tn), lambda i, j: (i, j)),
        out_shape=jax.ShapeDtypeStruct((m, n), F32),
        scratch_shapes=[pltpu.VMEM((tm, D_MODEL), BF16)],
        compiler_params=_params("arbitrary", "arbitrary"),
        name="norm_proj",
    )(h2, gain.reshape(1, D_MODEL), w_cat)


def _sb_kernel(q_ref, k_ref, v_ref, o_ref, *, tq):
    qi = pl.program_id(2)
    scale = HEAD_DIM ** -0.5
    q = q_ref[0] * scale
    lane = lax.broadcasted_iota(jnp.int32, (1, 2 * HEAD_DIM), 1)
    head_a = lane < HEAD_DIM
    q_heads = (jnp.where(head_a, q, 0.0).astype(BF16), jnp.where(head_a, 0.0, q).astype(BF16))
    row = lax.broadcasted_iota(jnp.int32, (tq, tq), 0)
    col = lax.broadcasted_iota(jnp.int32, (tq, tq), 1)
    causal = col < row
    later = (row > col).astype(BF16)

    def visit(j, carry, diagonal):
        start = pl.multiple_of(j * tq, tq)
        k = k_ref[0, pl.ds(start, tq), :].astype(BF16)
        v = v_ref[0, pl.ds(start, tq), :].astype(BF16)
        out = []
        for hd in range(2):
            r_run, acc = carry[hd]
            z = lax.dot_general(q_heads[hd], k, (((1,), (1,)), ((), ())), preferred_element_type=F32)
            sp = _softplus(z)
            if diagonal:
                sp = jnp.where(causal, sp, 0.0)
            within = _split_dot(sp, later)
            a = jnp.exp(z - sp - within - r_run)
            if diagonal:
                a = jnp.where(causal, a, 0.0)
            acc = acc + jnp.dot(a.astype(BF16), v, preferred_element_type=F32)
            r_run = r_run + jnp.sum(sp, axis=-1, keepdims=True)
            out.append((r_run, acc))
        return tuple(out)

    zero = (jnp.zeros((tq, 1), F32), jnp.zeros((tq, 2 * HEAD_DIM), F32))
    carry = visit(qi, (zero, zero), True)
    carry = lax.fori_loop(0, qi, lambda it, c: visit(qi - 1 - it, c, False), carry)
    o_ref[0] = jnp.where(head_a, carry[0][1], carry[1][1])


def _sb_attention(proj3):
    b, l, _ = proj3.shape
    tq = SB_BLOCK
    qb = COL_QKV // 128
    return pl.pallas_call(
        functools.partial(_sb_kernel, tq=tq),
        grid=(b, N_HEADS // 2, l // tq),
        in_specs=[pl.BlockSpec((1, tq, 128), lambda bi, hp, qi: (bi, qi, qb + hp)),
                  pl.BlockSpec((1, l, 128), lambda bi, hp, qi: (bi, 0, qb + 4 + hp)),
                  pl.BlockSpec((1, l, 128), lambda bi, hp, qi: (bi, 0, qb + 8 + hp))],
        out_specs=pl.BlockSpec((1, tq, 128), lambda bi, hp, qi: (bi, qi, hp)),
        out_shape=jax.ShapeDtypeStruct((b, l, C_MIX), F32),
        compiler_params=_params("arbitrary", "arbitrary", "arbitrary"),
        name="sb_attention",
    )(proj3, proj3, proj3)


def _token_shift(x, prev8, mu, first_block):
    tm = x.shape[0]
    rolled = pltpu.roll(x, shift=1, axis=0)
    carry_in = jnp.where(first_block, 0.0, prev8[7:8, :])
    rowi = lax.broadcasted_iota(jnp.int32, (tm, 1), 0)
    prev = jnp.where(rowi == 0, carry_in, rolled)
    return x + (prev - x) * mu


def _rw_prep_kernel(*refs, has_vres):
    if has_vres:
        (main_ref, lora_ref, pmain_ref, plora_ref, vfirst_ref, mu_main_ref, mu_lora_ref, w0_ref, a0_ref,
         kk_ref, ka_ref, vres0_ref, wl_ref, bd_ref,
         r_out, ld_out, k_out, v_out, kk_out, b_out, g_out) = refs
    else:
        (main_ref, lora_ref, pmain_ref, plora_ref, mu_main_ref, mu_lora_ref, w0_ref, a0_ref,
         kk_ref, ka_ref, wl_ref, bd_ref,
         r_out, ld_out, k_out, v_out, kk_out, b_out, g_out) = refs
    first = pl.program_id(1) == 0
    xs = _token_shift(main_ref[0], pmain_ref[0], mu_main_ref[...], first)
    lo = _token_shift(lora_ref[0], plora_ref[0], mu_lora_ref[...], first)
    r = xs[:, 0:C_MIX]
    kr = xs[:, C_MIX:2 * C_MIX]
    vr = xs[:, 2 * C_MIX:3 * C_MIX]

    lane = lax.broadcasted_iota(jnp.int32, (1, C_MIX), 1)
    act = jnp.where(lane < W_LORA, jnp.tanh(lo),
                    jnp.where(jnp.logical_and(lane >= W_LORA + A_LORA, lane < W_LORA + A_LORA + G_LORA),
                              _sigmoid(lo), lo))
    pre = jnp.dot(act.astype(BF16), wl_ref[...], preferred_element_type=F32)
    w_log = -_softplus(-(w0_ref[...] + pre[:, 0:C_MIX])) - 0.5
    ld = -jnp.exp(w_log)
    a = _sigmoid(a0_ref[...] + pre[:, C_MIX:2 * C_MIX])
    g = pre[:, 2 * C_MIX:3 * C_MIX]
    if has_vres:
        vr = vr + (vfirst_ref[0] - vr) * _sigmoid(vres0_ref[...] + pre[:, 3 * C_MIX:4 * C_MIX])
    kk = kr * kk_ref[...]
    nsq = _split_dot(kk * kk, bd_ref[...])
    kk = kk / jnp.maximum(jnp.sqrt(nsq), 1e-12)
    k = kr * (1.0 + (a - 1.0) * ka_ref[...])
    r_out[0] = r
    ld_out[0] = ld
    k_out[0] = k
    v_out[0] = vr
    kk_out[0] = kk
    b_out[0] = kk * a
    g_out[0] = g


def _rw_prep(proj3, v_first, mu_main, mu_lora, w0, a0, k_k, k_a, vres0, w_lora, bd):
    b, l, _ = proj3.shape
    tm = _pick(l, (384, 256, 128))
    has_vres = v_first is not None
    mb, lb = COL_RW // (3 * C_MIX), COL_LORA // C_MIX
    row = lambda width: pl.BlockSpec((1, width), lambda bi, i: (0, 0))
    tok = pl.BlockSpec((1, tm, C_MIX), lambda bi, i: (bi, i, 0))
    prev_idx = lambda i: jnp.maximum(i * (tm // 8) - 1, 0)
    in_specs = [pl.BlockSpec((1, tm, 3 * C_MIX), lambda bi, i: (bi, i, mb)),
                pl.BlockSpec((1, tm, C_MIX), lambda bi, i: (bi, i, lb)),
                pl.BlockSpec((1, 8, 3 * C_MIX), lambda bi, i: (bi, prev_idx(i), mb)),
                pl.BlockSpec((1, 8, C_MIX), lambda bi, i: (bi, prev_idx(i), lb))]
    args = [proj3, proj3, proj3, proj3]
    if has_vres:
        in_specs.append(tok)
        args.append(v_first)
    in_specs += [row(3 * C_MIX), row(C_MIX), row(C_MIX), row(C_MIX), row(C_MIX), row(C_MIX)]
    args += [mu_main, mu_lora, w0, a0, k_k, k_a]
    if has_vres:
        in_specs.append(row(C_MIX))
        args.append(vres0)
    in_specs += [pl.BlockSpec((C_MIX, 4 * C_MIX), lambda bi, i: (0, 0)),
                 pl.BlockSpec((C_MIX, C_MIX), lambda bi, i: (0, 0))]
    args += [w_lora, bd]
    out = jax.ShapeDtypeStruct((b, l, C_MIX), F32)
    return pl.pallas_call(
        functools.partial(_rw_prep_kernel, has_vres=has_vres),
        grid=(b, l // tm),
        in_specs=in_specs,
        out_specs=[tok] * 7,
        out_shape=[out] * 7,
        compiler_params=_params("arbitrary", "arbitrary"),
        name="rw_prep",
    )(*args)


def _unit_lower_inverse(a_strict, rowi, coli):
    eye = (rowi == coli).astype(F32)
    same = lambda sh: (rowi >> sh) == (coli >> sh)
    m8, m16, m32 = same(3), same(4), same(5)
    n1 = -jnp.where(m8, a_strict, 0.0)
    n2 = _mm(n1, n1)
    n4 = _mm(n2, n2)
    t = eye + n1
    t = t + _mm(t, n2)
    t = t + _mm(t, n4)
    for inner, outer in ((m8, m16), (m16, m32), (m32, None)):
        off = jnp.logical_not(inner) if outer is None else jnp.logical_and(outer, jnp.logical_not(inner))
        t = t - _mm(_mm(t, jnp.where(off, a_strict, 0.0)), t)
    return t


def _rw_scan_kernel(r_ref, ld_ref, k_ref, v_ref, kk_ref, b_ref, y_ref, s_ref):
    c = RW_CHUNK

    @pl.when(pl.program_id(1) == 0)
    def _():
        s_ref[...] = jnp.zeros_like(s_ref)

    rowi = lax.broadcasted_iota(jnp.int32, (c, c), 0)
    coli = lax.broadcasted_iota(jnp.int32, (c, c), 1)
    lower_incl = coli <= rowi
    lower_strict = coli < rowi

    ld = ld_ref[0]
    cum = jnp.dot(lower_incl.astype(F32), ld, precision=lax.Precision.HIGHEST,
                  preferred_element_type=F32)
    cum_end = cum[c - 1:c, :]
    e_neg = jnp.exp(-cum)
    to_end = jnp.exp(cum_end - cum)
    kt_all = kk_ref[0] * jnp.exp(cum - ld)
    bt_all = b_ref[0] * e_neg
    kn_all = k_ref[0] * e_neg
    rt_all = r_ref[0] * jnp.exp(cum)
    bh_all = b_ref[0] * to_end
    kh_all = k_ref[0] * to_end
    g_end = jnp.exp(cum_end)
    v_all = v_ref[0]

    ys = []
    for h in range(N_HEADS):
        sl = slice(h * HEAD_DIM, (h + 1) * HEAD_DIM)
        kt, bt, kn, rt, bh, kh, v = (x[:, sl] for x in (kt_all, bt_all, kn_all, rt_all, bh_all, kh_all, v_all))
        a_b = jnp.where(lower_strict, _mm_nt(kt, bt), 0.0)
        a_k = jnp.where(lower_strict, _mm_nt(kt, kn), 0.0)
        a_rb = jnp.where(lower_incl, _mm_nt(rt, bt), 0.0)
        a_rk = jnp.where(lower_incl, _mm_nt(rt, kn), 0.0)
        t_inv = _unit_lower_inverse(a_b, rowi, coli)
        k_hat = _mm(t_inv, kt)
        u_bar = _mm(t_inv, _mm(a_k, v))
        r_hat = rt - _mm(a_rb, k_hat)
        y_bar = _mm(a_rk, v) - _mm(a_rb, u_bar)
        e_bar = _mm_tn(k_hat, bh)
        h_add = _mm_tn(v, kh) - _mm_tn(u_bar, bh)
        s = s_ref[h]
        ys.append(_mm_nt(r_hat, s) + y_bar)
        s_ref[h] = s * g_end[:, sl] - _mm(s, e_bar) + h_add
    y_ref[0] = jnp.concatenate(ys, axis=-1)


def _rw_scan(r, ld, k, v, kk, bvec):
    b, l, _ = r.shape
    tok = pl.BlockSpec((1, RW_CHUNK, C_MIX), lambda bi, ci: (bi, ci, 0))
    return pl.pallas_call(
        _rw_scan_kernel,
        grid=(b, l // RW_CHUNK),
        in_specs=[tok] * 6,
        out_specs=tok,
        out_shape=jax.ShapeDtypeStruct((b, l, C_MIX), F32),
        scratch_shapes=[pltpu.VMEM((N_HEADS, HEAD_DIM, HEAD_DIM), F32)],
        compiler_params=_params("arbitrary", "arbitrary"),
        name="rw_scan",
    )(r, ld, k, v, kk, bvec)


def _merge_kernel(h_ref, gsb_ref, grw_ref, ysb_ref, yrw_ref, r_ref, k_ref, v_ref, g_ref, lnw_ref, lnb_ref,
                  rk_ref, bd_ref, wsb_ref, wrw_ref, wout_ref, o_ref):
    bd = bd_ref[...]
    inv_n = 1.0 / HEAD_DIM
    y = yrw_ref[...]
    mean = _split_dot(y, bd) * inv_n
    d = y - mean
    var = _split_dot(d * d, bd) * inv_n
    yn = d * lax.rsqrt(var + GN_EPS) * lnw_ref[...] + lnb_ref[...]
    bonus = _split_dot(r_ref[...] * k_ref[...] * rk_ref[...], bd) * v_ref[...]
    y_rw = (yn + bonus) * g_ref[...]
    o_sb = jnp.dot(ysb_ref[...].astype(BF16), wsb_ref[...], preferred_element_type=F32)
    o_rw = jnp.dot(y_rw.astype(BF16), wrw_ref[...], preferred_element_type=F32)
    merged = _sigmoid(gsb_ref[...]) * o_sb + _sigmoid(grw_ref[...]) * o_rw
    o_ref[...] = h_ref[...] + jnp.dot(merged.astype(BF16), wout_ref[...], preferred_element_type=F32)


def _merge(h2, proj2, y_sb, y_rw, r, k, v, g, ln_w, ln_b, r_k, bd, w_sb, w_rw, w_out):
    m = h2.shape[0]
    tm = _pick(m, (512, 256, 128))
    rows = lambda width: pl.BlockSpec((tm, width), lambda i: (i, 0))
    full = lambda shape: pl.BlockSpec(shape, lambda i: (0, 0))
    return pl.pallas_call(
        _merge_kernel,
        grid=(m // tm,),
        in_specs=[rows(D_MODEL),
                  pl.BlockSpec((tm, D_MODEL), lambda i: (i, COL_GATES // D_MODEL)),
                  pl.BlockSpec((tm, D_MODEL), lambda i: (i, COL_GATES // D_MODEL + 1)),
                  rows(C_MIX), rows(C_MIX), rows(C_MIX), rows(C_MIX), rows(C_MIX), rows(C_MIX),
                  full((1, C_MIX)), full((1, C_MIX)), full((1, C_MIX)), full((C_MIX, C_MIX)),
                  full((C_MIX, D_MODEL)), full((C_MIX, D_MODEL)), full((D_MODEL, D_MODEL))],
        out_specs=rows(D_MODEL),
        out_shape=jax.ShapeDtypeStruct((m, D_MODEL), F32),
        compiler_params=_params("arbitrary"),
        name="merge",
    )(h2, proj2, proj2, y_sb, y_rw, r, k, v, g, ln_w, ln_b, r_k, bd, w_sb, w_rw, w_out)


def _ffn_kernel(h_ref, g_ref, wg_ref, wu_ref, wo_ref, o_ref, hn_ref, acc_ref):
    f = pl.program_id(1)

    @pl.when(f == 0)
    def _():
        x = h_ref[...]
        ms = jnp.mean(x * x, axis=-1, keepdims=True)
        hn_ref[...] = (x * lax.rsqrt(ms + RMS_EPS) * g_ref[...]).astype(BF16)
        acc_ref[...] = x

    hn = hn_ref[...]
    gate = jnp.dot(hn, wg_ref[...], preferred_element_type=F32)
    up = jnp.dot(hn, wu_ref[...], preferred_element_type=F32)
    act = gate * _sigmoid(gate) * up
    acc_ref[...] += jnp.dot(act.astype(BF16), wo_ref[...], preferred_element_type=F32)

    @pl.when(f == pl.num_programs(1) - 1)
    def _():
        o_ref[...] = acc_ref[...]


def _ffn(h2, gain, w_in, w_out):
    m = h2.shape[0]
    tm = _pick(m, (512, 256, 128))
    tf = FFN_HIDDEN // 2
    nf = FFN_HIDDEN // tf
    return pl.pallas_call(
        _ffn_kernel,
        grid=(m // tm, nf),
        in_specs=[pl.BlockSpec((tm, D_MODEL), lambda i, f: (i, 0)),
                  pl.BlockSpec((1, D_MODEL), lambda i, f: (0, 0)),
                  pl.BlockSpec((D_MODEL, tf), lambda i, f: (0, f)),
                  pl.BlockSpec((D_MODEL, tf), lambda i, f: (0, nf + f)),
                  pl.BlockSpec((tf, D_MODEL), lambda i, f: (f, 0))],
        out_specs=pl.BlockSpec((tm, D_MODEL), lambda i, f: (i, 0)),
        out_shape=jax.ShapeDtypeStruct((m, D_MODEL), F32),
        scratch_shapes=[pltpu.VMEM((tm, D_MODEL), BF16), pltpu.VMEM((tm, D_MODEL), F32)],
        compiler_params=_params("arbitrary", "arbitrary"),
        name="ffn",
    )(h2, gain.reshape(1, D_MODEL), w_in, w_in, w_out)


def _final_norm_kernel(h_ref, g_ref, o_ref):
    x = h_ref[...]
    ms = jnp.mean(x * x, axis=-1, keepdims=True)
    o_ref[...] = x * lax.rsqrt(ms + RMS_EPS) * g_ref[...]


def _final_norm(h2, gain):
    m = h2.shape[0]
    tm = _pick(m, (512, 256, 128))
    return pl.pallas_call(
        _final_norm_kernel,
        grid=(m // tm,),
        in_specs=[pl.BlockSpec((tm, D_MODEL), lambda i: (i, 0)), pl.BlockSpec((1, D_MODEL), lambda i: (0, 0))],
        out_specs=pl.BlockSpec((tm, D_MODEL), lambda i: (i, 0)),
        out_shape=jax.ShapeDtypeStruct((m, D_MODEL), F32),
        compiler_params=_params("arbitrary"),
        name="final_norm",
    )(h2, gain.reshape(1, D_MODEL))


def _head_block_diag():
    idx = jnp.arange(C_MIX) // HEAD_DIM
    return (idx[:, None] == idx[None, :]).astype(BF16)


def _lora_weight(w_up, a_up, g_up, vres_up):
    w = jnp.zeros((C_MIX, 4 * C_MIX), F32)
    o = 0
    for seg, (mat, width) in enumerate(((w_up, W_LORA), (a_up, A_LORA), (g_up, G_LORA), (vres_up, V_LORA))):
        if mat is not None:
            w = w.at[o:o + width, seg * C_MIX:(seg + 1) * C_MIX].set(mat)
        o += width
    return w.astype(BF16)


def kernel(x, meta_tokens, norm_mix, norm_ffn, norm_final, w_in, mu_rw, w0, w_up, a0, a_up, g_up, k_k, k_a, r_k, ln_x_w, ln_x_b, vres_down, vres_mu, vres_up, vres0, w_sb_out, w_rw_out, w_out, w_ffn_in, w_ffn_out):
    b, s, d = x.shape
    depth = w_in.shape[0]
    l_real = N_META + s
    l_pad = -(-l_real // SB_BLOCK) * SB_BLOCK
    meta = jnp.broadcast_to(meta_tokens.astype(x.dtype)[None], (b, N_META, d))
    h = jnp.concatenate([meta, x, jnp.zeros((b, l_pad - l_real, d), x.dtype)], axis=1)
    h2 = h.reshape(b * l_pad, d)
    bd = _head_block_diag()
    n_in = 3 * C_MIX + RW_COLS
    row = lambda vec: vec.reshape(1, -1)

    v_first = None
    for layer in range(depth):
        wl = w_in[layer]
        vdown = vres_down[layer - 1] if layer > 0 else jnp.zeros((d, V_LORA), F32)
        w_cat = jnp.concatenate([wl[:, :6 * C_MIX], wl[:, n_in:], wl[:, 6 * C_MIX:n_in], vdown,
                                 jnp.zeros((d, C_MIX - LORA_USED), F32)], axis=1).astype(BF16)
        vmu = vres_mu[layer - 1] if layer > 0 else jnp.zeros((V_LORA,), F32)
        mu_main = row(mu_rw[layer, :3 * C_MIX])
        mu_lora = row(jnp.concatenate([mu_rw[layer, 3 * C_MIX:], vmu, jnp.zeros((C_MIX - LORA_USED,), F32)]))
        w_lora = _lora_weight(w_up[layer], a_up[layer], g_up[layer], vres_up[layer - 1] if layer > 0 else None)

        proj2 = _norm_proj(h2, norm_mix[layer], w_cat)
        proj3 = proj2.reshape(b, l_pad, N_PROJ)
        y_sb = _sb_attention(proj3)
        r, ld, k, v, kk, bvec, g = _rw_prep(
            proj3, v_first, mu_main, mu_lora, row(w0[layer]), row(a0[layer]), row(k_k[layer]), row(k_a[layer]),
            row(vres0[layer - 1]) if layer > 0 else None, w_lora, bd)
        if layer == 0:
            v_first = v
        y_rw = _rw_scan(r, ld, k, v, kk, bvec)
        flat = lambda t: t.reshape(b * l_pad, C_MIX)
        h2 = _merge(h2, proj2, flat(y_sb), flat(y_rw), flat(r), flat(k), flat(v), flat(g),
                    row(ln_x_w[layer]), row(ln_x_b[layer]), row(r_k[layer].reshape(-1)), bd,
                    w_sb_out[layer].astype(BF16), w_rw_out[layer].astype(BF16), w_out[layer].astype(BF16))
        h2 = _ffn(h2, norm_ffn[layer], w_ffn_in[layer].astype(BF16), w_ffn_out[layer].astype(BF16))

    out = _final_norm(h2, norm_final).reshape(b, l_pad, d)
    return out[:, N_META:N_META + s]
```

```python
import functools

import jax
import jax.numpy as jnp
from jax import lax
from jax.experimental import pallas as pl
from jax.experimental.pallas import tpu as pltpu

D_MODEL = 1024
HEAD_DIM = 64
N_HEADS = 8
C_MIX = N_HEADS * HEAD_DIM
N_META = 16
SB_BLOCK = 128
W_LORA, A_LORA, V_LORA, G_LORA = 64, 64, 32, 160
RW_COLS = 3 * C_MIX + W_LORA + A_LORA + G_LORA
FFN_HIDDEN = 2816
RMS_EPS = 1e-6
GN_EPS = 64e-5
RW_CHUNK = 64
SB_UNDERFLOW = 104.0
SB_EAGER_BLOCKS = 2

COL_QKV = 0
COL_RW = 1536
COL_GATES = 3072
COL_LORA = 5120
N_PROJ = 5632
LORA_USED = W_LORA + A_LORA + G_LORA + V_LORA

V7X_VMEM_LIMIT = 56 * 1024 * 1024

F32 = jnp.float32
BF16 = jnp.bfloat16


def _pick(n, cands):
    for c in cands:
        if n % c == 0:
            return c
    raise ValueError(f"no tile for {n} in {cands}")


def _params(*sem):
    return pltpu.CompilerParams(dimension_semantics=sem, vmem_limit_bytes=V7X_VMEM_LIMIT)


def _mm(a, b):
    return jnp.dot(a.astype(BF16), b.astype(BF16), preferred_element_type=F32)


def _mm_nt(a, b):
    return lax.dot_general(a.astype(BF16), b.astype(BF16), (((1,), (1,)), ((), ())),
                           preferred_element_type=F32)


def _mm_tn(a, b):
    return lax.dot_general(a.astype(BF16), b.astype(BF16), (((0,), (0,)), ((), ())),
                           preferred_element_type=F32)


def _split_dot(x, w_bf16):
    hi = x.astype(BF16)
    lo = (x - hi.astype(F32)).astype(BF16)
    return (jnp.dot(hi, w_bf16, preferred_element_type=F32)
            + jnp.dot(lo, w_bf16, preferred_element_type=F32))


def _softplus(u):
    return jnp.maximum(u, 0.0) + jnp.log(1.0 + jnp.exp(-jnp.abs(u)))


def _sigmoid(u):
    return 1.0 / (1.0 + jnp.exp(-u))


def _norm_proj_kernel(h_ref, g_ref, w_ref, o_ref, xn_ref):
    @pl.when(pl.program_id(1) == 0)
    def _():
        x = h_ref[...]
        ms = jnp.mean(x * x, axis=-1, keepdims=True)
        xn_ref[...] = (x * lax.rsqrt(ms + RMS_EPS) * g_ref[...]).astype(BF16)

    o_ref[...] = jnp.dot(xn_ref[...], w_ref[...], preferred_element_type=F32)


def _norm_proj(h2, gain, w_cat):
    m = h2.shape[0]
    n = w_cat.shape[1]
    tm = _pick(m, (1536, 1024, 512, 256, 128))
    tn = 512
    return pl.pallas_call(
        _norm_proj_kernel,
        grid=(m // tm, n // tn),
        in_specs=[pl.BlockSpec((tm, D_MODEL), lambda i, j: (i, 0)),
                  pl.BlockSpec((1, D_MODEL), lambda i, j: (0, 0)),
                  pl.BlockSpec((D_MODEL, tn), lambda i, j: (0, j))],
        out_specs=pl.BlockSpec((tm, tn), lambda i, j: (i, j)),
        out_shape=jax.ShapeDtypeStruct((m, n), F32),
        scratch_shapes=[pltpu.VMEM((tm, D_MODEL), BF16)],
        compiler_params=_params("arbitrary", "arbitrary"),
        name="norm_proj",
    )(h2, gain.reshape(1, D_MODEL), w_cat)


def _sb_kernel(q_ref, k_ref, v_ref, o_ref, *, tq):
    qi = pl.program_id(2)
    scale = HEAD_DIM ** -0.5
    q = q_ref[0] * scale
    lane = lax.broadcasted_iota(jnp.int32, (1, 2 * HEAD_DIM), 1)
    head_a = lane < HEAD_DIM
    q_heads = (jnp.where(head_a, q, 0.0).astype(BF16), jnp.where(head_a, 0.0, q).astype(BF16))
    row = lax.broadcasted_iota(jnp.int32, (tq, tq), 0)
    col = lax.broadcasted_iota(jnp.int32, (tq, tq), 1)
    causal = col < row
    later = (row > col).astype(BF16)

    def visit(j, carry, diagonal=False, live=None):
        start = pl.multiple_of(j * tq, tq)
        k = k_ref[0, pl.ds(start, tq), :].astype(BF16)
        v = v_ref[0, pl.ds(start, tq), :].astype(BF16)
        keep = causal if diagonal else live
        out = []
        for hd in range(2):
            r_run, acc = carry[hd]
            z = lax.dot_general(q_heads[hd], k, (((1,), (1,)), ((), ())), preferred_element_type=F32)
            sp = _softplus(z)
            if keep is not None:
                sp = jnp.where(keep, sp, 0.0)
            within = _split_dot(sp, later)
            a = jnp.exp(z - sp - within - r_run)
            if keep is not None:
                a = jnp.where(keep, a, 0.0)
            acc = acc + jnp.dot(a.astype(BF16), v, preferred_element_type=F32)
            r_run = r_run + jnp.sum(sp, axis=-1, keepdims=True)
            out.append((r_run, acc))
        return tuple(out)

    zero = (jnp.zeros((tq, 1), F32), jnp.zeros((tq, 2 * HEAD_DIM), F32))
    carry = visit(qi, (zero, zero), diagonal=True)
    for back in range(1, SB_EAGER_BLOCKS + 1):
        carry = visit(jnp.maximum(qi - back, 0), carry, live=qi >= back)

    def more(c):
        j, cr = c
        nearest = jnp.minimum(jnp.min(cr[0][0]), jnp.min(cr[1][0]))
        return jnp.logical_and(j >= 0, nearest < SB_UNDERFLOW)

    _, carry = lax.while_loop(more, lambda c: (c[0] - 1, visit(c[0], c[1])),
                              (qi - 1 - SB_EAGER_BLOCKS, carry))
    o_ref[0] = jnp.where(head_a, carry[0][1], carry[1][1])


def _sb_attention(proj3):
    b, l, _ = proj3.shape
    tq = SB_BLOCK
    qb = COL_QKV // 128
    return pl.pallas_call(
        functools.partial(_sb_kernel, tq=tq),
        grid=(b, N_HEADS // 2, l // tq),
        in_specs=[pl.BlockSpec((1, tq, 128), lambda bi, hp, qi: (bi, qi, qb + hp)),
                  pl.BlockSpec((1, l, 128), lambda bi, hp, qi: (bi, 0, qb + 4 + hp)),
                  pl.BlockSpec((1, l, 128), lambda bi, hp, qi: (bi, 0, qb + 8 + hp))],
        out_specs=pl.BlockSpec((1, tq, 128), lambda bi, hp, qi: (bi, qi, hp)),
        out_shape=jax.ShapeDtypeStruct((b, l, C_MIX), F32),
        compiler_params=_params("arbitrary", "arbitrary", "arbitrary"),
        name="sb_attention",
    )(proj3, proj3, proj3)


def _token_shift(x, prev8, mu, first_block):
    tm = x.shape[0]
    rolled = pltpu.roll(x, shift=1, axis=0)
    carry_in = jnp.where(first_block, 0.0, prev8[7:8, :])
    rowi = lax.broadcasted_iota(jnp.int32, (tm, 1), 0)
    prev = jnp.where(rowi == 0, carry_in, rolled)
    return x + (prev - x) * mu


def _rw_prep_kernel(*refs, has_vres):
    if has_vres:
        (main_ref, lora_ref, pmain_ref, plora_ref, vfirst_ref, mu_main_ref, mu_lora_ref, w0_ref, a0_ref,
         kk_ref, ka_ref, vres0_ref, wl_ref, bd_ref,
         r_out, ld_out, k_out, v_out, kk_out, b_out, g_out) = refs
    else:
        (main_ref, lora_ref, pmain_ref, plora_ref, mu_main_ref, mu_lora_ref, w0_ref, a0_ref,
         kk_ref, ka_ref, wl_ref, bd_ref,
         r_out, ld_out, k_out, v_out, kk_out, b_out, g_out) = refs
    first = pl.program_id(1) == 0
    xs = _token_shift(main_ref[0], pmain_ref[0], mu_main_ref[...], first)
    lo = _token_shift(lora_ref[0], plora_ref[0], mu_lora_ref[...], first)
    r = xs[:, 0:C_MIX]
    kr = xs[:, C_MIX:2 * C_MIX]
    vr = xs[:, 2 * C_MIX:3 * C_MIX]

    lane = lax.broadcasted_iota(jnp.int32, (1, C_MIX), 1)
    act = jnp.where(lane < W_LORA, jnp.tanh(lo),
                    jnp.where(jnp.logical_and(lane >= W_LORA + A_LORA, lane < W_LORA + A_LORA + G_LORA),
                              _sigmoid(lo), lo))
    pre = jnp.dot(act.astype(BF16), wl_ref[...], preferred_element_type=F32)
    w_log = -_softplus(-(w0_ref[...] + pre[:, 0:C_MIX])) - 0.5
    ld = -jnp.exp(w_log)
    a = _sigmoid(a0_ref[...] + pre[:, C_MIX:2 * C_MIX])
    g = pre[:, 2 * C_MIX:3 * C_MIX]
    if has_vres:
        vr = vr + (vfirst_ref[0] - vr) * _sigmoid(vres0_ref[...] + pre[:, 3 * C_MIX:4 * C_MIX])
    kk = kr * kk_ref[...]
    nsq = _split_dot(kk * kk, bd_ref[...])
    kk = kk / jnp.maximum(jnp.sqrt(nsq), 1e-12)
    k = kr * (1.0 + (a - 1.0) * ka_ref[...])
    r_out[0] = r
    ld_out[0] = ld
    k_out[0] = k
    v_out[0] = vr
    kk_out[0] = kk
    b_out[0] = kk * a
    g_out[0] = g


def _rw_prep(proj3, v_first, mu_main, mu_lora, w0, a0, k_k, k_a, vres0, w_lora, bd):
    b, l, _ = proj3.shape
    tm = _pick(l, (384, 256, 128))
    has_vres = v_first is not None
    mb, lb = COL_RW // (3 * C_MIX), COL_LORA // C_MIX
    row = lambda width: pl.BlockSpec((1, width), lambda bi, i: (0, 0))
    tok = pl.BlockSpec((1, tm, C_MIX), lambda bi, i: (bi, i, 0))
    prev_idx = lambda i: jnp.maximum(i * (tm // 8) - 1, 0)
    in_specs = [pl.BlockSpec((1, tm, 3 * C_MIX), lambda bi, i: (bi, i, mb)),
                pl.BlockSpec((1, tm, C_MIX), lambda bi, i: (bi, i, lb)),
                pl.BlockSpec((1, 8, 3 * C_MIX), lambda bi, i: (bi, prev_idx(i), mb)),
                pl.BlockSpec((1, 8, C_MIX), lambda bi, i: (bi, prev_idx(i), lb))]
    args = [proj3, proj3, proj3, proj3]
    if has_vres:
        in_specs.append(tok)
        args.append(v_first)
    in_specs += [row(3 * C_MIX), row(C_MIX), row(C_MIX), row(C_MIX), row(C_MIX), row(C_MIX)]
    args += [mu_main, mu_lora, w0, a0, k_k, k_a]
    if has_vres:
        in_specs.append(row(C_MIX))
        args.append(vres0)
    in_specs += [pl.BlockSpec((C_MIX, 4 * C_MIX), lambda bi, i: (0, 0)),
                 pl.BlockSpec((C_MIX, C_MIX), lambda bi, i: (0, 0))]
    args += [w_lora, bd]
    out = jax.ShapeDtypeStruct((b, l, C_MIX), F32)
    return pl.pallas_call(
        functools.partial(_rw_prep_kernel, has_vres=has_vres),
        grid=(b, l // tm),
        in_specs=in_specs,
        out_specs=[tok] * 7,
        out_shape=[out] * 7,
        compiler_params=_params("arbitrary", "arbitrary"),
        name="rw_prep",
    )(*args)


def _unit_lower_inverse(a_strict, rowi, coli):
    eye = (rowi == coli).astype(F32)
    same = lambda sh: (rowi >> sh) == (coli >> sh)
    m8, m16, m32 = same(3), same(4), same(5)
    n1 = -jnp.where(m8, a_strict, 0.0)
    n2 = _mm(n1, n1)
    n4 = _mm(n2, n2)
    t = eye + n1
    t = t + _mm(t, n2)
    t = t + _mm(t, n4)
    for inner, outer in ((m8, m16), (m16, m32), (m32, None)):
        off = jnp.logical_not(inner) if outer is None else jnp.logical_and(outer, jnp.logical_not(inner))
        t = t - _mm(_mm(t, jnp.where(off, a_strict, 0.0)), t)
    return t


def _rw_scan_kernel(r_ref, ld_ref, k_ref, v_ref, kk_ref, b_ref, y_ref, s_ref):
    c = RW_CHUNK

    @pl.when(pl.program_id(1) == 0)
    def _():
        s_ref[...] = jnp.zeros_like(s_ref)

    rowi = lax.broadcasted_iota(jnp.int32, (c, c), 0)
    coli = lax.broadcasted_iota(jnp.int32, (c, c), 1)
    lower_incl = coli <= rowi
    lower_strict = coli < rowi

    ld = ld_ref[0]
    cum = jnp.dot(lower_incl.astype(F32), ld, precision=lax.Precision.HIGHEST,
                  preferred_element_type=F32)
    cum_end = cum[c - 1:c, :]
    e_neg = jnp.exp(-cum)
    to_end = jnp.exp(cum_end - cum)
    kt_all = kk_ref[0] * jnp.exp(cum - ld)
    bt_all = b_ref[0] * e_neg
    kn_all = k_ref[0] * e_neg
    rt_all = r_ref[0] * jnp.exp(cum)
    bh_all = b_ref[0] * to_end
    kh_all = k_ref[0] * to_end
    g_end = jnp.exp(cum_end)
    v_all = v_ref[0]

    ys = []
    for h in range(N_HEADS):
        sl = slice(h * HEAD_DIM, (h + 1) * HEAD_DIM)
        kt, bt, kn, rt, bh, kh, v = (x[:, sl] for x in (kt_all, bt_all, kn_all, rt_all, bh_all, kh_all, v_all))
        a_b = jnp.where(lower_strict, _mm_nt(kt, bt), 0.0)
        a_k = jnp.where(lower_strict, _mm_nt(kt, kn), 0.0)
        a_rb = jnp.where(lower_incl, _mm_nt(rt, bt), 0.0)
        a_rk = jnp.where(lower_incl, _mm_nt(rt, kn), 0.0)
        t_inv = _unit_lower_inverse(a_b, rowi, coli)
        k_hat = _mm(t_inv, kt)
        u_bar = _mm(t_inv, _mm(a_k, v))
        r_hat = rt - _mm(a_rb, k_hat)
        y_bar = _mm(a_rk, v) - _mm(a_rb, u_bar)
        e_bar = _mm_tn(k_hat, bh)
        h_add = _mm_tn(v, kh) - _mm_tn(u_bar, bh)
        s = s_ref[h]
        ys.append(_mm_nt(r_hat, s) + y_bar)
        s_ref[h] = s * g_end[:, sl] - _mm(s, e_bar) + h_add
    y_ref[0] = jnp.concatenate(ys, axis=-1)


def _rw_scan(r, ld, k, v, kk, bvec):
    b, l, _ = r.shape
    tok = pl.BlockSpec((1, RW_CHUNK, C_MIX), lambda bi, ci: (bi, ci, 0))
    return pl.pallas_call(
        _rw_scan_kernel,
        grid=(b, l // RW_CHUNK),
        in_specs=[tok] * 6,
        out_specs=tok,
        out_shape=jax.ShapeDtypeStruct((b, l, C_MIX), F32),
        scratch_shapes=[pltpu.VMEM((N_HEADS, HEAD_DIM, HEAD_DIM), F32)],
        compiler_params=_params("arbitrary", "arbitrary"),
        name="rw_scan",
    )(r, ld, k, v, kk, bvec)


def _merge_kernel(h_ref, gsb_ref, grw_ref, ysb_ref, yrw_ref, r_ref, k_ref, v_ref, g_ref, lnw_ref, lnb_ref,
                  rk_ref, bd_ref, wsb_ref, wrw_ref, wout_ref, o_ref):
    bd = bd_ref[...]
    inv_n = 1.0 / HEAD_DIM
    y = yrw_ref[...]
    mean = _split_dot(y, bd) * inv_n
    d = y - mean
    var = _split_dot(d * d, bd) * inv_n
    yn = d * lax.rsqrt(var + GN_EPS) * lnw_ref[...] + lnb_ref[...]
    bonus = _split_dot(r_ref[...] * k_ref[...] * rk_ref[...], bd) * v_ref[...]
    y_rw = (yn + bonus) * g_ref[...]
    o_sb = jnp.dot(ysb_ref[...].astype(BF16), wsb_ref[...], preferred_element_type=F32)
    o_rw = jnp.dot(y_rw.astype(BF16), wrw_ref[...], preferred_element_type=F32)
    merged = _sigmoid(gsb_ref[...]) * o_sb + _sigmoid(grw_ref[...]) * o_rw
    o_ref[...] = h_ref[...] + jnp.dot(merged.astype(BF16), wout_ref[...], preferred_element_type=F32)


def _merge(h2, proj2, y_sb, y_rw, r, k, v, g, ln_w, ln_b, r_k, bd, w_sb, w_rw, w_out):
    m = h2.shape[0]
    tm = _pick(m, (512, 256, 128))
    rows = lambda width: pl.BlockSpec((tm, width), lambda i: (i, 0))
    full = lambda shape: pl.BlockSpec(shape, lambda i: (0, 0))
    return pl.pallas_call(
        _merge_kernel,
        grid=(m // tm,),
        in_specs=[rows(D_MODEL),
                  pl.BlockSpec((tm, D_MODEL), lambda i: (i, COL_GATES // D_MODEL)),
                  pl.BlockSpec((tm, D_MODEL), lambda i: (i, COL_GATES // D_MODEL + 1)),
                  rows(C_MIX), rows(C_MIX), rows(C_MIX), rows(C_MIX), rows(C_MIX), rows(C_MIX),
                  full((1, C_MIX)), full((1, C_MIX)), full((1, C_MIX)), full((C_MIX, C_MIX)),
                  full((C_MIX, D_MODEL)), full((C_MIX, D_MODEL)), full((D_MODEL, D_MODEL))],
        out_specs=rows(D_MODEL),
        out_shape=jax.ShapeDtypeStruct((m, D_MODEL), F32),
        compiler_params=_params("arbitrary"),
        name="merge",
    )(h2, proj2, proj2, y_sb, y_rw, r, k, v, g, ln_w, ln_b, r_k, bd, w_sb, w_rw, w_out)


def _ffn_kernel(h_ref, g_ref, wg_ref, wu_ref, wo_ref, o_ref, hn_ref, acc_ref):
    f = pl.program_id(1)

    @pl.when(f == 0)
    def _():
        x = h_ref[...]
        ms = jnp.mean(x * x, axis=-1, keepdims=True)
        hn_ref[...] = (x * lax.rsqrt(ms + RMS_EPS) * g_ref[...]).astype(BF16)
        acc_ref[...] = x

    hn = hn_ref[...]
    gate = jnp.dot(hn, wg_ref[...], preferred_element_type=F32)
    up = jnp.dot(hn, wu_ref[...], preferred_element_type=F32)
    act = gate * _sigmoid(gate) * up
    acc_ref[...] += jnp.dot(act.astype(BF16), wo_ref[...], preferred_element_type=F32)

    @pl.when(f == pl.num_programs(1) - 1)
    def _():
        o_ref[...] = acc_ref[...]


def _ffn(h2, gain, w_in, w_out):
    m = h2.shape[0]
    tm = _pick(m, (512, 256, 128))
    tf = FFN_HIDDEN // 2
    nf = FFN_HIDDEN // tf
    return pl.pallas_call(
        _ffn_kernel,
        grid=(m // tm, nf),
        in_specs=[pl.BlockSpec((tm, D_MODEL), lambda i, f: (i, 0)),
                  pl.BlockSpec((1, D_MODEL), lambda i, f: (0, 0)),
                  pl.BlockSpec((D_MODEL, tf), lambda i, f: (0, f)),
                  pl.BlockSpec((D_MODEL, tf), lambda i, f: (0, nf + f)),
                  pl.BlockSpec((tf, D_MODEL), lambda i, f: (f, 0))],
        out_specs=pl.BlockSpec((tm, D_MODEL), lambda i, f: (i, 0)),
        out_shape=jax.ShapeDtypeStruct((m, D_MODEL), F32),
        scratch_shapes=[pltpu.VMEM((tm, D_MODEL), BF16), pltpu.VMEM((tm, D_MODEL), F32)],
        compiler_params=_params("arbitrary", "arbitrary"),
        name="ffn",
    )(h2, gain.reshape(1, D_MODEL), w_in, w_in, w_out)


def _final_norm_kernel(h_ref, g_ref, o_ref):
    x = h_ref[...]
    ms = jnp.mean(x * x, axis=-1, keepdims=True)
    o_ref[...] = x * lax.rsqrt(ms + RMS_EPS) * g_ref[...]


def _final_norm(h2, gain):
    m = h2.shape[0]
    tm = _pick(m, (512, 256, 128))
    return pl.pallas_call(
        _final_norm_kernel,
        grid=(m // tm,),
        in_specs=[pl.BlockSpec((tm, D_MODEL), lambda i: (i, 0)), pl.BlockSpec((1, D_MODEL), lambda i: (0, 0))],
        out_specs=pl.BlockSpec((tm, D_MODEL), lambda i: (i, 0)),
        out_shape=jax.ShapeDtypeStruct((m, D_MODEL), F32),
        compiler_params=_params("arbitrary"),
        name="final_norm",
    )(h2, gain.reshape(1, D_MODEL))


def _head_block_diag():
    idx = jnp.arange(C_MIX) // HEAD_DIM
    return (idx[:, None] == idx[None, :]).astype(BF16)


def _lora_weight(w_up, a_up, g_up, vres_up):
    w = jnp.zeros((C_MIX, 4 * C_MIX), F32)
    o = 0
    for seg, (mat, width) in enumerate(((w_up, W_LORA), (a_up, A_LORA), (g_up, G_LORA), (vres_up, V_LORA))):
        if mat is not None:
            w = w.at[o:o + width, seg * C_MIX:(seg + 1) * C_MIX].set(mat)
        o += width
    return w.astype(BF16)


def kernel(x, meta_tokens, norm_mix, norm_ffn, norm_final, w_in, mu_rw, w0, w_up, a0, a_up, g_up, k_k, k_a, r_k, ln_x_w, ln_x_b, vres_down, vres_mu, vres_up, vres0, w_sb_out, w_rw_out, w_out, w_ffn_in, w_ffn_out):
    b, s, d = x.shape
    depth = w_in.shape[0]
    l_real = N_META + s
    l_pad = -(-l_real // SB_BLOCK) * SB_BLOCK
    meta = jnp.broadcast_to(meta_tokens.astype(x.dtype)[None], (b, N_META, d))
    h = jnp.concatenate([meta, x, jnp.zeros((b, l_pad - l_real, d), x.dtype)], axis=1)
    h2 = h.reshape(b * l_pad, d)
    bd = _head_block_diag()
    n_in = 3 * C_MIX + RW_COLS
    row = lambda vec: vec.reshape(1, -1)

    v_first = None
    for layer in range(depth):
        wl = w_in[layer]
        vdown = vres_down[layer - 1] if layer > 0 else jnp.zeros((d, V_LORA), F32)
        w_cat = jnp.concatenate([wl[:, :6 * C_MIX], wl[:, n_in:], wl[:, 6 * C_MIX:n_in], vdown,
                                 jnp.zeros((d, C_MIX - LORA_USED), F32)], axis=1).astype(BF16)
        vmu = vres_mu[layer - 1] if layer > 0 else jnp.zeros((V_LORA,), F32)
        mu_main = row(mu_rw[layer, :3 * C_MIX])
        mu_lora = row(jnp.concatenate([mu_rw[layer, 3 * C_MIX:], vmu, jnp.zeros((C_MIX - LORA_USED,), F32)]))
        w_lora = _lora_weight(w_up[layer], a_up[layer], g_up[layer], vres_up[layer - 1] if layer > 0 else None)

        proj2 = _norm_proj(h2, norm_mix[layer], w_cat)
        proj3 = proj2.reshape(b, l_pad, N_PROJ)
        y_sb = _sb_attention(proj3)
        r, ld, k, v, kk, bvec, g = _rw_prep(
            proj3, v_first, mu_main, mu_lora, row(w0[layer]), row(a0[layer]), row(k_k[layer]), row(k_a[layer]),
            row(vres0[layer - 1]) if layer > 0 else None, w_lora, bd)
        if layer == 0:
            v_first = v
        y_rw = _rw_scan(r, ld, k, v, kk, bvec)
        flat = lambda t: t.reshape(b * l_pad, C_MIX)
        h2 = _merge(h2, proj2, flat(y_sb), flat(y_rw), flat(r), flat(k), flat(v), flat(g),
                    row(ln_x_w[layer]), row(ln_x_b[layer]), row(r_k[layer].reshape(-1)), bd,
                    w_sb_out[layer].astype(BF16), w_rw_out[layer].astype(BF16), w_out[layer].astype(BF16))
        h2 = _ffn(h2, norm_ffn[layer], w_ffn_in[layer].astype(BF16), w_ffn_out[layer].astype(BF16))

    out = _final_norm(h2, norm_final).reshape(b, l_pad, d)
    return out[:, N_META:N_META + s]
```

```python
import functools

import jax
import jax.numpy as jnp
from jax import lax
from jax.experimental import pallas as pl
from jax.experimental.pallas import tpu as pltpu

D_MODEL = 1024
HEAD_DIM = 64
N_HEADS = 8
C_MIX = N_HEADS * HEAD_DIM
N_META = 16
SB_BLOCK = 128
W_LORA, A_LORA, V_LORA, G_LORA = 64, 64, 32, 160
RW_COLS = 3 * C_MIX + W_LORA + A_LORA + G_LORA
FFN_HIDDEN = 2816
RMS_EPS = 1e-6
GN_EPS = 64e-5
RW_CHUNK = 64
SB_UNDERFLOW = 104.0
SB_EAGER_BLOCKS = 2

COL_QKV = 0
COL_RW = 1536
COL_GATES = 3072
COL_LORA = 5120
N_PROJ = 5632
LORA_USED = W_LORA + A_LORA + G_LORA + V_LORA

V7X_VMEM_LIMIT = 56 * 1024 * 1024

F32 = jnp.float32
BF16 = jnp.bfloat16


def _pick(n, cands):
    for c in cands:
        if n % c == 0:
            return c
    raise ValueError(f"no tile for {n} in {cands}")


def _params(*sem):
    return pltpu.CompilerParams(dimension_semantics=sem, vmem_limit_bytes=V7X_VMEM_LIMIT)


def _mm(a, b):
    return jnp.dot(a.astype(BF16), b.astype(BF16), preferred_element_type=F32)


def _mm_nt(a, b):
    return lax.dot_general(a.astype(BF16), b.astype(BF16), (((1,), (1,)), ((), ())),
                           preferred_element_type=F32)


def _mm_tn(a, b):
    return lax.dot_general(a.astype(BF16), b.astype(BF16), (((0,), (0,)), ((), ())),
                           preferred_element_type=F32)


def _each(fn, *lists):
    return [fn(*xs) for xs in zip(*lists)]


def _split_dot(x, w_bf16):
    hi = x.astype(BF16)
    lo = (x - hi.astype(F32)).astype(BF16)
    return (jnp.dot(hi, w_bf16, preferred_element_type=F32)
            + jnp.dot(lo, w_bf16, preferred_element_type=F32))


def _softplus(u):
    return jnp.maximum(u, 0.0) + jnp.log(1.0 + jnp.exp(-jnp.abs(u)))


def _sigmoid(u):
    return 1.0 / (1.0 + jnp.exp(-u))


def _norm_proj_kernel(h_ref, g_ref, w_ref, o_ref, xn_ref):
    @pl.when(pl.program_id(1) == 0)
    def _():
        x = h_ref[...]
        ms = jnp.mean(x * x, axis=-1, keepdims=True)
        xn_ref[...] = (x * lax.rsqrt(ms + RMS_EPS) * g_ref[...]).astype(BF16)

    o_ref[...] = jnp.dot(xn_ref[...], w_ref[...], preferred_element_type=F32)


def _norm_proj(h2, gain, w_cat):
    m = h2.shape[0]
    n = w_cat.shape[1]
    tm = _pick(m, (1536, 1024, 512, 256, 128))
    tn = 512
    return pl.pallas_call(
        _norm_proj_kernel,
        grid=(m // tm, n // tn),
        in_specs=[pl.BlockSpec((tm, D_MODEL), lambda i, j: (i, 0)),
                  pl.BlockSpec((1, D_MODEL), lambda i, j: (0, 0)),
                  pl.BlockSpec((D_MODEL, tn), lambda i, j: (0, j))],
        out_specs=pl.BlockSpec((tm, tn), lambda i, j: (i, j)),
        out_shape=jax.ShapeDtypeStruct((m, n), F32),
        scratch_shapes=[pltpu.VMEM((tm, D_MODEL), BF16)],
        compiler_params=_params("arbitrary", "arbitrary"),
        name="norm_proj",
    )(h2, gain.reshape(1, D_MODEL), w_cat)


def _sb_kernel(q_ref, k_ref, v_ref, o_ref, *, tq):
    qi = pl.program_id(2)
    scale = HEAD_DIM ** -0.5
    q = q_ref[0] * scale
    lane = lax.broadcasted_iota(jnp.int32, (1, 2 * HEAD_DIM), 1)
    head_a = lane < HEAD_DIM
    q_heads = (jnp.where(head_a, q, 0.0).astype(BF16), jnp.where(head_a, 0.0, q).astype(BF16))
    row = lax.broadcasted_iota(jnp.int32, (tq, tq), 0)
    col = lax.broadcasted_iota(jnp.int32, (tq, tq), 1)
    causal = col < row
    later = (row > col).astype(BF16)

    def visit_many(blocks, carry):
        pairs = [(bi, hd) for bi in range(len(blocks)) for hd in range(2)]
        starts = [pl.multiple_of(j * tq, tq) for j, _ in blocks]
        ks = [k_ref[0, pl.ds(st, tq), :].astype(BF16) for st in starts]
        vs = [v_ref[0, pl.ds(st, tq), :].astype(BF16) for st in starts]
        keep = lambda x, bi: x if blocks[bi][1] is None else jnp.where(blocks[bi][1], x, 0.0)
        z = [lax.dot_general(q_heads[hd], ks[bi], (((1,), (1,)), ((), ())), preferred_element_type=F32)
             for bi, hd in pairs]
        sp = [keep(_softplus(zi), bi) for zi, (bi, hd) in zip(z, pairs)]
        within = [_split_dot(s, later) for s in sp]
        mass = [jnp.sum(s, axis=-1, keepdims=True) for s in sp]
        r_run = [carry[0][0], carry[1][0]]
        acc = [carry[0][1], carry[1][1]]
        for i, (bi, hd) in enumerate(pairs):
            a = keep(jnp.exp(z[i] - sp[i] - within[i] - r_run[hd]), bi)
            acc[hd] = acc[hd] + jnp.dot(a.astype(BF16), vs[bi], preferred_element_type=F32)
            r_run[hd] = r_run[hd] + mass[i]
        return ((r_run[0], acc[0]), (r_run[1], acc[1]))

    zero = (jnp.zeros((tq, 1), F32), jnp.zeros((tq, 2 * HEAD_DIM), F32))
    eager = [(qi, causal)] + [(jnp.maximum(qi - back, 0), qi >= back) for back in range(1, SB_EAGER_BLOCKS + 1)]
    carry = visit_many(eager, (zero, zero))

    def more(c):
        j, cr = c
        nearest = jnp.minimum(jnp.min(cr[0][0]), jnp.min(cr[1][0]))
        return jnp.logical_and(j >= 0, nearest < SB_UNDERFLOW)

    _, carry = lax.while_loop(more, lambda c: (c[0] - 1, visit_many([(c[0], None)], c[1])),
                              (qi - 1 - SB_EAGER_BLOCKS, carry))
    o_ref[0] = jnp.where(head_a, carry[0][1], carry[1][1])


def _sb_attention(proj3):
    b, l, _ = proj3.shape
    tq = SB_BLOCK
    qb = COL_QKV // 128
    return pl.pallas_call(
        functools.partial(_sb_kernel, tq=tq),
        grid=(b, N_HEADS // 2, l // tq),
        in_specs=[pl.BlockSpec((1, tq, 128), lambda bi, hp, qi: (bi, qi, qb + hp)),
                  pl.BlockSpec((1, l, 128), lambda bi, hp, qi: (bi, 0, qb + 4 + hp)),
                  pl.BlockSpec((1, l, 128), lambda bi, hp, qi: (bi, 0, qb + 8 + hp))],
        out_specs=pl.BlockSpec((1, tq, 128), lambda bi, hp, qi: (bi, qi, hp)),
        out_shape=jax.ShapeDtypeStruct((b, l, C_MIX), F32),
        compiler_params=_params("arbitrary", "arbitrary", "arbitrary"),
        name="sb_attention",
    )(proj3, proj3, proj3)


def _token_shift(x, prev8, mu, first_block):
    tm = x.shape[0]
    rolled = pltpu.roll(x, shift=1, axis=0)
    carry_in = jnp.where(first_block, 0.0, prev8[7:8, :])
    rowi = lax.broadcasted_iota(jnp.int32, (tm, 1), 0)
    prev = jnp.where(rowi == 0, carry_in, rolled)
    return x + (prev - x) * mu


def _rw_prep_kernel(*refs, has_vres):
    if has_vres:
        (main_ref, lora_ref, pmain_ref, plora_ref, vfirst_ref, mu_main_ref, mu_lora_ref, w0_ref, a0_ref,
         kk_ref, ka_ref, vres0_ref, wl_ref, bd_ref,
         r_out, ld_out, k_out, v_out, kk_out, b_out, g_out) = refs
    else:
        (main_ref, lora_ref, pmain_ref, plora_ref, mu_main_ref, mu_lora_ref, w0_ref, a0_ref,
         kk_ref, ka_ref, wl_ref, bd_ref,
         r_out, ld_out, k_out, v_out, kk_out, b_out, g_out) = refs
    first = pl.program_id(1) == 0
    xs = _token_shift(main_ref[0], pmain_ref[0], mu_main_ref[...], first)
    lo = _token_shift(lora_ref[0], plora_ref[0], mu_lora_ref[...], first)
    r = xs[:, 0:C_MIX]
    kr = xs[:, C_MIX:2 * C_MIX]
    vr = xs[:, 2 * C_MIX:3 * C_MIX]

    lane = lax.broadcasted_iota(jnp.int32, (1, C_MIX), 1)
    act = jnp.where(lane < W_LORA, jnp.tanh(lo),
                    jnp.where(jnp.logical_and(lane >= W_LORA + A_LORA, lane < W_LORA + A_LORA + G_LORA),
                              _sigmoid(lo), lo))
    pre = jnp.dot(act.astype(BF16), wl_ref[...], preferred_element_type=F32)
    w_log = -_softplus(-(w0_ref[...] + pre[:, 0:C_MIX])) - 0.5
    ld = -jnp.exp(w_log)
    a = _sigmoid(a0_ref[...] + pre[:, C_MIX:2 * C_MIX])
    g = pre[:, 2 * C_MIX:3 * C_MIX]
    if has_vres:
        vr = vr + (vfirst_ref[0] - vr) * _sigmoid(vres0_ref[...] + pre[:, 3 * C_MIX:4 * C_MIX])
    kk = kr * kk_ref[...]
    nsq = _split_dot(kk * kk, bd_ref[...])
    kk = kk / jnp.maximum(jnp.sqrt(nsq), 1e-12)
    k = kr * (1.0 + (a - 1.0) * ka_ref[...])
    r_out[0] = r
    ld_out[0] = ld
    k_out[0] = k
    v_out[0] = vr
    kk_out[0] = kk
    b_out[0] = kk * a
    g_out[0] = g


def _rw_prep(proj3, v_first, mu_main, mu_lora, w0, a0, k_k, k_a, vres0, w_lora, bd):
    b, l, _ = proj3.shape
    tm = _pick(l, (384, 256, 128))
    has_vres = v_first is not None
    mb, lb = COL_RW // (3 * C_MIX), COL_LORA // C_MIX
    row = lambda width: pl.BlockSpec((1, width), lambda bi, i: (0, 0))
    tok = pl.BlockSpec((1, tm, C_MIX), lambda bi, i: (bi, i, 0))
    prev_idx = lambda i: jnp.maximum(i * (tm // 8) - 1, 0)
    in_specs = [pl.BlockSpec((1, tm, 3 * C_MIX), lambda bi, i: (bi, i, mb)),
                pl.BlockSpec((1, tm, C_MIX), lambda bi, i: (bi, i, lb)),
                pl.BlockSpec((1, 8, 3 * C_MIX), lambda bi, i: (bi, prev_idx(i), mb)),
                pl.BlockSpec((1, 8, C_MIX), lambda bi, i: (bi, prev_idx(i), lb))]
    args = [proj3, proj3, proj3, proj3]
    if has_vres:
        in_specs.append(tok)
        args.append(v_first)
    in_specs += [row(3 * C_MIX), row(C_MIX), row(C_MIX), row(C_MIX), row(C_MIX), row(C_MIX)]
    args += [mu_main, mu_lora, w0, a0, k_k, k_a]
    if has_vres:
        in_specs.append(row(C_MIX))
        args.append(vres0)
    in_specs += [pl.BlockSpec((C_MIX, 4 * C_MIX), lambda bi, i: (0, 0)),
                 pl.BlockSpec((C_MIX, C_MIX), lambda bi, i: (0, 0))]
    args += [w_lora, bd]
    out = jax.ShapeDtypeStruct((b, l, C_MIX), F32)
    return pl.pallas_call(
        functools.partial(_rw_prep_kernel, has_vres=has_vres),
        grid=(b, l // tm),
        in_specs=in_specs,
        out_specs=[tok] * 7,
        out_shape=[out] * 7,
        compiler_params=_params("arbitrary", "arbitrary"),
        name="rw_prep",
    )(*args)


def _unit_lower_inverse(a_strict, rowi, coli):
    eye = (rowi == coli).astype(F32)
    same = lambda sh: (rowi >> sh) == (coli >> sh)
    m8, m16, m32 = same(3), same(4), same(5)
    n1 = [-jnp.where(m8, a, 0.0) for a in a_strict]
    n2 = _each(_mm, n1, n1)
    n4 = _each(_mm, n2, n2)
    t = [eye + n for n in n1]
    t = _each(lambda ti, ni: ti + _mm(ti, ni), t, n2)
    t = _each(lambda ti, ni: ti + _mm(ti, ni), t, n4)
    for inner, outer in ((m8, m16), (m16, m32), (m32, None)):
        off = jnp.logical_not(inner) if outer is None else jnp.logical_and(outer, jnp.logical_not(inner))
        ta = _each(lambda ti, a: _mm(ti, jnp.where(off, a, 0.0)), t, a_strict)
        t = _each(lambda ti, tai: ti - _mm(tai, ti), t, ta)
    return t


def _rw_scan_kernel(r_ref, ld_ref, k_ref, v_ref, kk_ref, b_ref, y_ref, s_ref):
    c = RW_CHUNK

    @pl.when(pl.program_id(1) == 0)
    def _():
        s_ref[...] = jnp.zeros_like(s_ref)

    rowi = lax.broadcasted_iota(jnp.int32, (c, c), 0)
    coli = lax.broadcasted_iota(jnp.int32, (c, c), 1)
    lower_incl = coli <= rowi
    lower_strict = coli < rowi

    ld = ld_ref[0]
    cum = jnp.dot(lower_incl.astype(F32), ld, precision=lax.Precision.HIGHEST,
                  preferred_element_type=F32)
    cum_end = cum[c - 1:c, :]
    e_neg = jnp.exp(-cum)
    to_end = jnp.exp(cum_end - cum)
    kt_all = kk_ref[0] * jnp.exp(cum - ld)
    bt_all = b_ref[0] * e_neg
    kn_all = k_ref[0] * e_neg
    rt_all = r_ref[0] * jnp.exp(cum)
    bh_all = b_ref[0] * to_end
    kh_all = k_ref[0] * to_end
    g_end = jnp.exp(cum_end)
    v_all = v_ref[0]

    heads = lambda x: [x[:, h * HEAD_DIM:(h + 1) * HEAD_DIM] for h in range(N_HEADS)]
    kt, bt, kn, rt, bh, kh, v, g_h = (heads(x) for x in (kt_all, bt_all, kn_all, rt_all, bh_all, kh_all, v_all,
                                                        g_end))
    a_b = _each(lambda x, y: jnp.where(lower_strict, _mm_nt(x, y), 0.0), kt, bt)
    a_k = _each(lambda x, y: jnp.where(lower_strict, _mm_nt(x, y), 0.0), kt, kn)
    a_rb = _each(lambda x, y: jnp.where(lower_incl, _mm_nt(x, y), 0.0), rt, bt)
    a_rk = _each(lambda x, y: jnp.where(lower_incl, _mm_nt(x, y), 0.0), rt, kn)
    akv = _each(_mm, a_k, v)
    arkv = _each(_mm, a_rk, v)
    vkh = _each(_mm_tn, v, kh)
    t_inv = _unit_lower_inverse(a_b, rowi, coli)
    k_hat = _each(_mm, t_inv, kt)
    u_bar = _each(_mm, t_inv, akv)
    r_hat = _each(lambda x, a, kh_: x - _mm(a, kh_), rt, a_rb, k_hat)
    y_bar = _each(lambda x, a, u: x - _mm(a, u), arkv, a_rb, u_bar)
    e_bar = _each(_mm_tn, k_hat, bh)
    h_add = _each(lambda x, u, b_: x - _mm_tn(u, b_), vkh, u_bar, bh)
    s = [s_ref[h] for h in range(N_HEADS)]
    ys = _each(lambda rh, sh, yb: _mm_nt(rh, sh) + yb, r_hat, s, y_bar)
    s_new = _each(lambda sh, gh, eb, ha: sh * gh - _mm(sh, eb) + ha, s, g_h, e_bar, h_add)
    for h in range(N_HEADS):
        s_ref[h] = s_new[h]
    y_ref[0] = jnp.concatenate(ys, axis=-1)


def _rw_scan(r, ld, k, v, kk, bvec):
    b, l, _ = r.shape
    tok = pl.BlockSpec((1, RW_CHUNK, C_MIX), lambda bi, ci: (bi, ci, 0))
    return pl.pallas_call(
        _rw_scan_kernel,
        grid=(b, l // RW_CHUNK),
        in_specs=[tok] * 6,
        out_specs=tok,
        out_shape=jax.ShapeDtypeStruct((b, l, C_MIX), F32),
        scratch_shapes=[pltpu.VMEM((N_HEADS, HEAD_DIM, HEAD_DIM), F32)],
        compiler_params=_params("arbitrary", "arbitrary"),
        name="rw_scan",
    )(r, ld, k, v, kk, bvec)


def _merge_kernel(h_ref, gsb_ref, grw_ref, ysb_ref, yrw_ref, r_ref, k_ref, v_ref, g_ref, lnw_ref, lnb_ref,
                  rk_ref, bd_ref, wsb_ref, wrw_ref, wout_ref, o_ref):
    bd = bd_ref[...]
    inv_n = 1.0 / HEAD_DIM
    y = yrw_ref[...]
    mean = _split_dot(y, bd) * inv_n
    d = y - mean
    var = _split_dot(d * d, bd) * inv_n
    yn = d * lax.rsqrt(var + GN_EPS) * lnw_ref[...] + lnb_ref[...]
    bonus = _split_dot(r_ref[...] * k_ref[...] * rk_ref[...], bd) * v_ref[...]
    y_rw = (yn + bonus) * g_ref[...]
    o_sb = jnp.dot(ysb_ref[...].astype(BF16), wsb_ref[...], preferred_element_type=F32)
    o_rw = jnp.dot(y_rw.astype(BF16), wrw_ref[...], preferred_element_type=F32)
    merged = _sigmoid(gsb_ref[...]) * o_sb + _sigmoid(grw_ref[...]) * o_rw
    o_ref[...] = h_ref[...] + jnp.dot(merged.astype(BF16), wout_ref[...], preferred_element_type=F32)


def _merge(h2, proj2, y_sb, y_rw, r, k, v, g, ln_w, ln_b, r_k, bd, w_sb, w_rw, w_out):
    m = h2.shape[0]
    tm = _pick(m, (512, 256, 128))
    rows = lambda width: pl.BlockSpec((tm, width), lambda i: (i, 0))
    full = lambda shape: pl.BlockSpec(shape, lambda i: (0, 0))
    return pl.pallas_call(
        _merge_kernel,
        grid=(m // tm,),
        in_specs=[rows(D_MODEL),
                  pl.BlockSpec((tm, D_MODEL), lambda i: (i, COL_GATES // D_MODEL)),
                  pl.BlockSpec((tm, D_MODEL), lambda i: (i, COL_GATES // D_MODEL + 1)),
                  rows(C_MIX), rows(C_MIX), rows(C_MIX), rows(C_MIX), rows(C_MIX), rows(C_MIX),
                  full((1, C_MIX)), full((1, C_MIX)), full((1, C_MIX)), full((C_MIX, C_MIX)),
                  full((C_MIX, D_MODEL)), full((C_MIX, D_MODEL)), full((D_MODEL, D_MODEL))],
        out_specs=rows(D_MODEL),
        out_shape=jax.ShapeDtypeStruct((m, D_MODEL), F32),
        compiler_params=_params("arbitrary"),
        name="merge",
    )(h2, proj2, proj2, y_sb, y_rw, r, k, v, g, ln_w, ln_b, r_k, bd, w_sb, w_rw, w_out)


def _ffn_kernel(h_ref, g_ref, wg_ref, wu_ref, wo_ref, o_ref, hn_ref, acc_ref):
    f = pl.program_id(1)

    @pl.when(f == 0)
    def _():
        x = h_ref[...]
        ms = jnp.mean(x * x, axis=-1, keepdims=True)
        hn_ref[...] = (x * lax.rsqrt(ms + RMS_EPS) * g_ref[...]).astype(BF16)
        acc_ref[...] = x

    hn = hn_ref[...]
    gate = jnp.dot(hn, wg_ref[...], preferred_element_type=F32)
    up = jnp.dot(hn, wu_ref[...], preferred_element_type=F32)
    act = gate * _sigmoid(gate) * up
    acc_ref[...] += jnp.dot(act.astype(BF16), wo_ref[...], preferred_element_type=F32)

    @pl.when(f == pl.num_programs(1) - 1)
    def _():
        o_ref[...] = acc_ref[...]


def _ffn(h2, gain, w_in, w_out):
    m = h2.shape[0]
    tm = _pick(m, (512, 256, 128))
    tf = FFN_HIDDEN // 2
    nf = FFN_HIDDEN // tf
    return pl.pallas_call(
        _ffn_kernel,
        grid=(m // tm, nf),
        in_specs=[pl.BlockSpec((tm, D_MODEL), lambda i, f: (i, 0)),
                  pl.BlockSpec((1, D_MODEL), lambda i, f: (0, 0)),
                  pl.BlockSpec((D_MODEL, tf), lambda i, f: (0, f)),
                  pl.BlockSpec((D_MODEL, tf), lambda i, f: (0, nf + f)),
                  pl.BlockSpec((tf, D_MODEL), lambda i, f: (f, 0))],
        out_specs=pl.BlockSpec((tm, D_MODEL), lambda i, f: (i, 0)),
        out_shape=jax.ShapeDtypeStruct((m, D_MODEL), F32),
        scratch_shapes=[pltpu.VMEM((tm, D_MODEL), BF16), pltpu.VMEM((tm, D_MODEL), F32)],
        compiler_params=_params("arbitrary", "arbitrary"),
        name="ffn",
    )(h2, gain.reshape(1, D_MODEL), w_in, w_in, w_out)


def _final_norm_kernel(h_ref, g_ref, o_ref):
    x = h_ref[...]
    ms = jnp.mean(x * x, axis=-1, keepdims=True)
    o_ref[...] = x * lax.rsqrt(ms + RMS_EPS) * g_ref[...]


def _final_norm(h2, gain):
    m = h2.shape[0]
    tm = _pick(m, (512, 256, 128))
    return pl.pallas_call(
        _final_norm_kernel,
        grid=(m // tm,),
        in_specs=[pl.BlockSpec((tm, D_MODEL), lambda i: (i, 0)), pl.BlockSpec((1, D_MODEL), lambda i: (0, 0))],
        out_specs=pl.BlockSpec((tm, D_MODEL), lambda i: (i, 0)),
        out_shape=jax.ShapeDtypeStruct((m, D_MODEL), F32),
        compiler_params=_params("arbitrary"),
        name="final_norm",
    )(h2, gain.reshape(1, D_MODEL))


def _head_block_diag():
    idx = jnp.arange(C_MIX) // HEAD_DIM
    return (idx[:, None] == idx[None, :]).astype(BF16)


def _lora_weight(w_up, a_up, g_up, vres_up):
    w = jnp.zeros((C_MIX, 4 * C_MIX), F32)
    o = 0
    for seg, (mat, width) in enumerate(((w_up, W_LORA), (a_up, A_LORA), (g_up, G_LORA), (vres_up, V_LORA))):
        if mat is not None:
            w = w.at[o:o + width, seg * C_MIX:(seg + 1) * C_MIX].set(mat)
        o += width
    return w.astype(BF16)


def kernel(x, meta_tokens, norm_mix, norm_ffn, norm_final, w_in, mu_rw, w0, w_up, a0, a_up, g_up, k_k, k_a, r_k, ln_x_w, ln_x_b, vres_down, vres_mu, vres_up, vres0, w_sb_out, w_rw_out, w_out, w_ffn_in, w_ffn_out):
    b, s, d = x.shape
    depth = w_in.shape[0]
    l_real = N_META + s
    l_pad = -(-l_real // SB_BLOCK) * SB_BLOCK
    meta = jnp.broadcast_to(meta_tokens.astype(x.dtype)[None], (b, N_META, d))
    h = jnp.concatenate([meta, x, jnp.zeros((b, l_pad - l_real, d), x.dtype)], axis=1)
    h2 = h.reshape(b * l_pad, d)
    bd = _head_block_diag()
    n_in = 3 * C_MIX + RW_COLS
    row = lambda vec: vec.reshape(1, -1)

    v_first = None
    for layer in range(depth):
        wl = w_in[layer]
        vdown = vres_down[layer - 1] if layer > 0 else jnp.zeros((d, V_LORA), F32)
        w_cat = jnp.concatenate([wl[:, :6 * C_MIX], wl[:, n_in:], wl[:, 6 * C_MIX:n_in], vdown,
                                 jnp.zeros((d, C_MIX - LORA_USED), F32)], axis=1).astype(BF16)
        vmu = vres_mu[layer - 1] if layer > 0 else jnp.zeros((V_LORA,), F32)
        mu_main = row(mu_rw[layer, :3 * C_MIX])
        mu_lora = row(jnp.concatenate([mu_rw[layer, 3 * C_MIX:], vmu, jnp.zeros((C_MIX - LORA_USED,), F32)]))
        w_lora = _lora_weight(w_up[layer], a_up[layer], g_up[layer], vres_up[layer - 1] if layer > 0 else None)

        proj2 = _norm_proj(h2, norm_mix[layer], w_cat)
        proj3 = proj2.reshape(b, l_pad, N_PROJ)
        y_sb = _sb_attention(proj3)
        r, ld, k, v, kk, bvec, g = _rw_prep(
            proj3, v_first, mu_main, mu_lora, row(w0[layer]), row(a0[layer]), row(k_k[layer]), row(k_a[layer]),
            row(vres0[layer - 1]) if layer > 0 else None, w_lora, bd)
        if layer == 0:
            v_first = v
        y_rw = _rw_scan(r, ld, k, v, kk, bvec)
        flat = lambda t: t.reshape(b * l_pad, C_MIX)
        h2 = _merge(h2, proj2, flat(y_sb), flat(y_rw), flat(r), flat(k), flat(v), flat(g),
                    row(ln_x_w[layer]), row(ln_x_b[layer]), row(r_k[layer].reshape(-1)), bd,
                    w_sb_out[layer].astype(BF16), w_rw_out[layer].astype(BF16), w_out[layer].astype(BF16))
        h2 = _ffn(h2, norm_ffn[layer], w_ffn_in[layer].astype(BF16), w_ffn_out[layer].astype(BF16))

    out = _final_norm(h2, norm_final).reshape(b, l_pad, d)
    return out[:, N_META:N_META + s]
```

```python
import functools

import jax
import jax.numpy as jnp
from jax import lax
from jax.experimental import pallas as pl
from jax.experimental.pallas import tpu as pltpu

D_MODEL = 1024
HEAD_DIM = 64
N_HEADS = 8
C_MIX = N_HEADS * HEAD_DIM
N_META = 16
SB_BLOCK = 128
W_LORA, A_LORA, V_LORA, G_LORA = 64, 64, 32, 160
RW_COLS = 3 * C_MIX + W_LORA + A_LORA + G_LORA
FFN_HIDDEN = 2816
RMS_EPS = 1e-6
GN_EPS = 64e-5
RW_CHUNK = 64
SB_UNDERFLOW = 104.0
SB_EAGER_BLOCKS = 2

N_QKV = 3 * C_MIX
COL_RW = 0
COL_LORA = 1536
COL_GATES = 2048
N_REST = 4096
LORA_USED = W_LORA + A_LORA + G_LORA + V_LORA

V7X_VMEM_LIMIT = 56 * 1024 * 1024

F32 = jnp.float32
BF16 = jnp.bfloat16


def _pick(n, cands):
    for c in cands:
        if n % c == 0:
            return c
    raise ValueError(f"no tile for {n} in {cands}")


def _params(*sem):
    return pltpu.CompilerParams(dimension_semantics=sem, vmem_limit_bytes=V7X_VMEM_LIMIT)


def _mm(a, b):
    return jnp.dot(a.astype(BF16), b.astype(BF16), preferred_element_type=F32)


def _mm_nt(a, b):
    return lax.dot_general(a.astype(BF16), b.astype(BF16), (((1,), (1,)), ((), ())),
                           preferred_element_type=F32)


def _mm_tn(a, b):
    return lax.dot_general(a.astype(BF16), b.astype(BF16), (((0,), (0,)), ((), ())),
                           preferred_element_type=F32)


def _each(fn, *lists):
    return [fn(*xs) for xs in zip(*lists)]


def _split_dot(x, w_bf16):
    hi = x.astype(BF16)
    lo = (x - hi.astype(F32)).astype(BF16)
    return (jnp.dot(hi, w_bf16, preferred_element_type=F32)
            + jnp.dot(lo, w_bf16, preferred_element_type=F32))


def _softplus(u):
    return jnp.maximum(u, 0.0) + jnp.log(1.0 + jnp.exp(-jnp.abs(u)))


def _sigmoid(u):
    return 1.0 / (1.0 + jnp.exp(-u))


def _norm_proj_kernel(h_ref, g_ref, w_ref, qkv_ref, rest_ref, xn_ref, *, n_qkv_blocks):
    j = pl.program_id(1)

    @pl.when(j == 0)
    def _():
        x = h_ref[...]
        ms = jnp.mean(x * x, axis=-1, keepdims=True)
        xn_ref[...] = (x * lax.rsqrt(ms + RMS_EPS) * g_ref[...]).astype(BF16)

    acc = jnp.dot(xn_ref[...], w_ref[...], preferred_element_type=F32)

    @pl.when(j < n_qkv_blocks)
    def _():
        qkv_ref[...] = acc.astype(BF16)

    @pl.when(j >= n_qkv_blocks)
    def _():
        rest_ref[...] = acc


def _norm_proj(h2, gain, w_cat):
    m = h2.shape[0]
    n = w_cat.shape[1]
    tm = _pick(m, (1536, 1024, 512, 256, 128))
    tn = 512
    nq = N_QKV // tn
    return pl.pallas_call(
        functools.partial(_norm_proj_kernel, n_qkv_blocks=nq),
        grid=(m // tm, n // tn),
        in_specs=[pl.BlockSpec((tm, D_MODEL), lambda i, j: (i, 0)),
                  pl.BlockSpec((1, D_MODEL), lambda i, j: (0, 0)),
                  pl.BlockSpec((D_MODEL, tn), lambda i, j: (0, j))],
        out_specs=[pl.BlockSpec((tm, tn), lambda i, j: (i, jnp.minimum(j, nq - 1))),
                   pl.BlockSpec((tm, tn), lambda i, j: (i, jnp.maximum(j - nq, 0)))],
        out_shape=[jax.ShapeDtypeStruct((m, N_QKV), BF16), jax.ShapeDtypeStruct((m, n - N_QKV), F32)],
        scratch_shapes=[pltpu.VMEM((tm, D_MODEL), BF16)],
        compiler_params=_params("arbitrary", "arbitrary"),
        name="norm_proj",
    )(h2, gain.reshape(1, D_MODEL), w_cat)


def _sb_kernel(q_ref, k_ref, v_ref, lm_ref, o_ref, *, tq):
    qi = pl.program_id(1)
    n_pairs = N_HEADS // 2
    pw = 2 * HEAD_DIM
    lane = lax.broadcasted_iota(jnp.int32, (1, pw), 1)
    head_a = lane < HEAD_DIM
    zero_bf = jnp.zeros((), BF16)
    split_rows = lambda x: jnp.concatenate([jnp.where(head_a, x, zero_bf), jnp.where(head_a, zero_bf, x)], axis=0)
    q = q_ref[0] * jnp.asarray(HEAD_DIM ** -0.5, BF16)
    q2 = [split_rows(q[:, p * pw:(p + 1) * pw]) for p in range(n_pairs)]
    row = lax.broadcasted_iota(jnp.int32, (2 * tq, tq), 0)
    col = lax.broadcasted_iota(jnp.int32, (2 * tq, tq), 1)
    causal2 = col < jnp.where(row >= tq, row - tq, row)
    lm = lm_ref[...]

    def visit_many(blocks, r_run, acc):
        chains = [(bi, p) for bi in range(len(blocks)) for p in range(n_pairs)]
        starts = [pl.multiple_of(blk[0] * tq, tq) for blk in blocks]
        ks = [k_ref[0, pl.ds(st, tq), :] for st in starts]
        vs = [v_ref[0, pl.ds(st, tq), :] for st in starts]
        vs = [v if blk[2] is None else jnp.where(blk[2], v, zero_bf) for v, blk in zip(vs, blocks)]
        z = [lax.dot_general(q2[p], ks[bi][:, p * pw:(p + 1) * pw], (((1,), (1,)), ((), ())),
                             preferred_element_type=F32) for bi, p in chains]
        sp = [_softplus(zi) for zi in z]
        sp = [jnp.where(causal2, s, 0.0) if blocks[bi][1] else s for s, (bi, p) in zip(sp, chains)]
        hi = [s.astype(BF16) for s in sp]
        lo = [(s - h.astype(F32)).astype(BF16) for s, h in zip(sp, hi)]
        wm = [jnp.dot(jnp.concatenate([h, l_], axis=1), lm, preferred_element_type=F32) for h, l_ in zip(hi, lo)]
        r_run, acc = list(r_run), list(acc)
        for i, (bi, p) in enumerate(chains):
            a = jnp.exp(z[i] - sp[i] - wm[i][:, :tq] - r_run[p])
            if blocks[bi][1]:
                a = jnp.where(causal2, a, 0.0)
            a = a.astype(BF16)
            v2 = split_rows(vs[bi][:, p * pw:(p + 1) * pw])
            acc[p] = acc[p] + jnp.dot(jnp.concatenate([a[:tq], a[tq:]], axis=1), v2, preferred_element_type=F32)
            r_run[p] = r_run[p] + wm[i][:, tq:]
        return r_run, acc

    eager = [(qi, True, None)] + [(jnp.maximum(qi - back, 0), False, qi >= back)
                                  for back in range(1, SB_EAGER_BLOCKS + 1)]
    r_run, acc = visit_many(eager, [jnp.zeros((2 * tq, tq), F32)] * n_pairs, [jnp.zeros((tq, pw), F32)] * n_pairs)

    def more(c):
        j, rr, _ = c
        nearest = functools.reduce(jnp.minimum, [jnp.min(r[:, :1]) for r in rr])
        return jnp.logical_and(j >= 0, nearest < SB_UNDERFLOW)

    def far(c):
        j, rr, ac = c
        rr, ac = visit_many([(j, False, None)], rr, ac)
        return j - 1, rr, ac

    _, _, acc = lax.while_loop(more, far, (qi - 1 - SB_EAGER_BLOCKS, r_run, acc))
    o_ref[0] = jnp.concatenate(acc, axis=1)


def _later_and_ones(tq):
    s_from = jnp.arange(2 * tq)[:, None] % tq
    s_to = jnp.arange(2 * tq)[None, :]
    return jnp.where(s_to < tq, s_from > s_to, True).astype(BF16)


def _sb_attention(qkv3):
    b, l, _ = qkv3.shape
    tq = SB_BLOCK
    return pl.pallas_call(
        functools.partial(_sb_kernel, tq=tq),
        grid=(b, l // tq),
        in_specs=[pl.BlockSpec((1, tq, C_MIX), lambda bi, qi: (bi, qi, 0)),
                  pl.BlockSpec((1, l, C_MIX), lambda bi, qi: (bi, 0, 1)),
                  pl.BlockSpec((1, l, C_MIX), lambda bi, qi: (bi, 0, 2)),
                  pl.BlockSpec((2 * tq, 2 * tq), lambda bi, qi: (0, 0))],
        out_specs=pl.BlockSpec((1, tq, C_MIX), lambda bi, qi: (bi, qi, 0)),
        out_shape=jax.ShapeDtypeStruct((b, l, C_MIX), F32),
        compiler_params=_params("arbitrary", "arbitrary"),
        name="sb_attention",
    )(qkv3, qkv3, qkv3, _later_and_ones(tq))


def _token_shift(x, prev8, mu, first_block):
    tm = x.shape[0]
    rolled = pltpu.roll(x, shift=1, axis=0)
    carry_in = jnp.where(first_block, 0.0, prev8[7:8, :])
    rowi = lax.broadcasted_iota(jnp.int32, (tm, 1), 0)
    prev = jnp.where(rowi == 0, carry_in, rolled)
    return x + (prev - x) * mu


def _rw_prep_kernel(*refs, has_vres):
    if has_vres:
        (main_ref, lora_ref, pmain_ref, plora_ref, vfirst_ref, mu_main_ref, mu_lora_ref, w0_ref, a0_ref,
         kk_ref, ka_ref, vres0_ref, wl_ref, bd_ref,
         r_out, ld_out, k_out, v_out, kk_out, b_out, g_out) = refs
    else:
        (main_ref, lora_ref, pmain_ref, plora_ref, mu_main_ref, mu_lora_ref, w0_ref, a0_ref,
         kk_ref, ka_ref, wl_ref, bd_ref,
         r_out, ld_out, k_out, v_out, kk_out, b_out, g_out) = refs
    first = pl.program_id(1) == 0
    xs = _token_shift(main_ref[0], pmain_ref[0], mu_main_ref[...], first)
    lo = _token_shift(lora_ref[0], plora_ref[0], mu_lora_ref[...], first)
    r = xs[:, 0:C_MIX]
    kr = xs[:, C_MIX:2 * C_MIX]
    vr = xs[:, 2 * C_MIX:3 * C_MIX]

    lane = lax.broadcasted_iota(jnp.int32, (1, C_MIX), 1)
    act = jnp.where(lane < W_LORA, jnp.tanh(lo),
                    jnp.where(jnp.logical_and(lane >= W_LORA + A_LORA, lane < W_LORA + A_LORA + G_LORA),
                              _sigmoid(lo), lo))
    pre = jnp.dot(act.astype(BF16), wl_ref[...], preferred_element_type=F32)
    w_log = -_softplus(-(w0_ref[...] + pre[:, 0:C_MIX])) - 0.5
    ld = -jnp.exp(w_log)
    a = _sigmoid(a0_ref[...] + pre[:, C_MIX:2 * C_MIX])
    g = pre[:, 2 * C_MIX:3 * C_MIX]
    if has_vres:
        vr = vr + (vfirst_ref[0] - vr) * _sigmoid(vres0_ref[...] + pre[:, 3 * C_MIX:4 * C_MIX])
    kk = kr * kk_ref[...]
    nsq = _split_dot(kk * kk, bd_ref[...])
    kk = kk / jnp.maximum(jnp.sqrt(nsq), 1e-12)
    k = kr * (1.0 + (a - 1.0) * ka_ref[...])
    r_out[0] = r
    ld_out[0] = ld
    k_out[0] = k
    v_out[0] = vr
    kk_out[0] = kk
    b_out[0] = kk * a
    g_out[0] = g


def _rw_prep(proj3, v_first, mu_main, mu_lora, w0, a0, k_k, k_a, vres0, w_lora, bd):
    b, l, _ = proj3.shape
    tm = _pick(l, (384, 256, 128))
    has_vres = v_first is not None
    mb, lb = COL_RW // (3 * C_MIX), COL_LORA // C_MIX
    row = lambda width: pl.BlockSpec((1, width), lambda bi, i: (0, 0))
    tok = pl.BlockSpec((1, tm, C_MIX), lambda bi, i: (bi, i, 0))
    prev_idx = lambda i: jnp.maximum(i * (tm // 8) - 1, 0)
    in_specs = [pl.BlockSpec((1, tm, 3 * C_MIX), lambda bi, i: (bi, i, mb)),
                pl.BlockSpec((1, tm, C_MIX), lambda bi, i: (bi, i, lb)),
                pl.BlockSpec((1, 8, 3 * C_MIX), lambda bi, i: (bi, prev_idx(i), mb)),
                pl.BlockSpec((1, 8, C_MIX), lambda bi, i: (bi, prev_idx(i), lb))]
    args = [proj3, proj3, proj3, proj3]
    if has_vres:
        in_specs.append(tok)
        args.append(v_first)
    in_specs += [row(3 * C_MIX), row(C_MIX), row(C_MIX), row(C_MIX), row(C_MIX), row(C_MIX)]
    args += [mu_main, mu_lora, w0, a0, k_k, k_a]
    if has_vres:
        in_specs.append(row(C_MIX))
        args.append(vres0)
    in_specs += [pl.BlockSpec((C_MIX, 4 * C_MIX), lambda bi, i: (0, 0)),
                 pl.BlockSpec((C_MIX, C_MIX), lambda bi, i: (0, 0))]
    args += [w_lora, bd]
    out = jax.ShapeDtypeStruct((b, l, C_MIX), F32)
    return pl.pallas_call(
        functools.partial(_rw_prep_kernel, has_vres=has_vres),
        grid=(b, l // tm),
        in_specs=in_specs,
        out_specs=[tok] * 7,
        out_shape=[out] * 7,
        compiler_params=_params("arbitrary", "arbitrary"),
        name="rw_prep",
    )(*args)


def _unit_lower_inverse(a_strict, rowi, coli):
    eye = (rowi == coli).astype(F32)
    same = lambda sh: (rowi >> sh) == (coli >> sh)
    m8, m16, m32 = same(3), same(4), same(5)
    n1 = [-jnp.where(m8, a, 0.0) for a in a_strict]
    n2 = _each(_mm, n1, n1)
    n4 = _each(_mm, n2, n2)
    t = [eye + n for n in n1]
    t = _each(lambda ti, ni: ti + _mm(ti, ni), t, n2)
    t = _each(lambda ti, ni: ti + _mm(ti, ni), t, n4)
    for inner, outer in ((m8, m16), (m16, m32), (m32, None)):
        off = jnp.logical_not(inner) if outer is None else jnp.logical_and(outer, jnp.logical_not(inner))
        ta = _each(lambda ti, a: _mm(ti, jnp.where(off, a, 0.0)), t, a_strict)
        t = _each(lambda ti, tai: ti - _mm(tai, ti), t, ta)
    return t


def _rw_scan_kernel(r_ref, ld_ref, k_ref, v_ref, kk_ref, b_ref, y_ref, s_ref):
    c = RW_CHUNK

    @pl.when(pl.program_id(0) == 0)
    def _():
        s_ref[...] = jnp.zeros_like(s_ref)

    rowi = lax.broadcasted_iota(jnp.int32, (c, c), 0)
    coli = lax.broadcasted_iota(jnp.int32, (c, c), 1)
    lower_incl = coli <= rowi
    lower_strict = coli < rowi

    n_batch = r_ref.shape[0]
    n_chains = n_batch * N_HEADS
    wide = lambda ref: jnp.concatenate([ref[bi] for bi in range(n_batch)], axis=-1)
    ld, kk_w, b_w, k_w, r_w, v_all = (wide(ref) for ref in (ld_ref, kk_ref, b_ref, k_ref, r_ref, v_ref))
    cum = jnp.dot(lower_incl.astype(F32), ld, precision=lax.Precision.HIGHEST,
                  preferred_element_type=F32)
    cum_end = cum[c - 1:c, :]
    e_neg = jnp.exp(-cum)
    to_end = jnp.exp(cum_end - cum)
    kt_all = kk_w * jnp.exp(cum - ld)
    bt_all = b_w * e_neg
    kn_all = k_w * e_neg
    rt_all = r_w * jnp.exp(cum)
    bh_all = b_w * to_end
    kh_all = k_w * to_end
    g_end = jnp.exp(cum_end)

    heads = lambda x: [x[:, h * HEAD_DIM:(h + 1) * HEAD_DIM] for h in range(n_chains)]
    kt, bt, kn, rt, bh, kh, v, g_h = (heads(x) for x in (kt_all, bt_all, kn_all, rt_all, bh_all, kh_all, v_all,
                                                        g_end))
    a_b = _each(lambda x, y: jnp.where(lower_strict, _mm_nt(x, y), 0.0), kt, bt)
    a_k = _each(lambda x, y: jnp.where(lower_strict, _mm_nt(x, y), 0.0), kt, kn)
    a_rb = _each(lambda x, y: jnp.where(lower_incl, _mm_nt(x, y), 0.0), rt, bt)
    a_rk = _each(lambda x, y: jnp.where(lower_incl, _mm_nt(x, y), 0.0), rt, kn)
    akv = _each(_mm, a_k, v)
    arkv = _each(_mm, a_rk, v)
    vkh = _each(_mm_tn, v, kh)
    t_inv = _unit_lower_inverse(a_b, rowi, coli)
    k_hat = _each(_mm, t_inv, kt)
    u_bar = _each(_mm, t_inv, akv)
    r_hat = _each(lambda x, a, kh_: x - _mm(a, kh_), rt, a_rb, k_hat)
    y_bar = _each(lambda x, a, u: x - _mm(a, u), arkv, a_rb, u_bar)
    e_bar = _each(_mm_tn, k_hat, bh)
    h_add = _each(lambda x, u, b_: x - _mm_tn(u, b_), vkh, u_bar, bh)
    s = [s_ref[h] for h in range(n_chains)]
    ys = _each(lambda rh, sh, yb: _mm_nt(rh, sh) + yb, r_hat, s, y_bar)
    s_new = _each(lambda sh, gh, eb, ha: sh * gh - _mm(sh, eb) + ha, s, g_h, e_bar, h_add)
    for h in range(n_chains):
        s_ref[h] = s_new[h]
    for bi in range(n_batch):
        y_ref[bi] = jnp.concatenate(ys[bi * N_HEADS:(bi + 1) * N_HEADS], axis=-1)


def _rw_scan(r, ld, k, v, kk, bvec):
    b, l, _ = r.shape
    tok = pl.BlockSpec((b, RW_CHUNK, C_MIX), lambda ci: (0, ci, 0))
    return pl.pallas_call(
        _rw_scan_kernel,
        grid=(l // RW_CHUNK,),
        in_specs=[tok] * 6,
        out_specs=tok,
        out_shape=jax.ShapeDtypeStruct((b, l, C_MIX), F32),
        scratch_shapes=[pltpu.VMEM((b * N_HEADS, HEAD_DIM, HEAD_DIM), F32)],
        compiler_params=_params("arbitrary"),
        name="rw_scan",
    )(r, ld, k, v, kk, bvec)


def _merge_kernel(h_ref, gsb_ref, grw_ref, ysb_ref, yrw_ref, r_ref, k_ref, v_ref, g_ref, lnw_ref, lnb_ref,
                  rk_ref, bd_ref, wsb_ref, wrw_ref, wout_ref, o_ref):
    bd = bd_ref[...]
    inv_n = 1.0 / HEAD_DIM
    y = yrw_ref[...]
    mean = _split_dot(y, bd) * inv_n
    d = y - mean
    var = _split_dot(d * d, bd) * inv_n
    yn = d * lax.rsqrt(var + GN_EPS) * lnw_ref[...] + lnb_ref[...]
    bonus = _split_dot(r_ref[...] * k_ref[...] * rk_ref[...], bd) * v_ref[...]
    y_rw = (yn + bonus) * g_ref[...]
    o_sb = jnp.dot(ysb_ref[...].astype(BF16), wsb_ref[...], preferred_element_type=F32)
    o_rw = jnp.dot(y_rw.astype(BF16), wrw_ref[...], preferred_element_type=F32)
    merged = _sigmoid(gsb_ref[...]) * o_sb + _sigmoid(grw_ref[...]) * o_rw
    o_ref[...] = h_ref[...] + jnp.dot(merged.astype(BF16), wout_ref[...], preferred_element_type=F32)


def _merge(h2, proj2, y_sb, y_rw, r, k, v, g, ln_w, ln_b, r_k, bd, w_sb, w_rw, w_out):
    m = h2.shape[0]
    tm = _pick(m, (512, 256, 128))
    rows = lambda width: pl.BlockSpec((tm, width), lambda i: (i, 0))
    full = lambda shape: pl.BlockSpec(shape, lambda i: (0, 0))
    return pl.pallas_call(
        _merge_kernel,
        grid=(m // tm,),
        in_specs=[rows(D_MODEL),
                  pl.BlockSpec((tm, D_MODEL), lambda i: (i, COL_GATES // D_MODEL)),
                  pl.BlockSpec((tm, D_MODEL), lambda i: (i, COL_GATES // D_MODEL + 1)),
                  rows(C_MIX), rows(C_MIX), rows(C_MIX), rows(C_MIX), rows(C_MIX), rows(C_MIX),
                  full((1, C_MIX)), full((1, C_MIX)), full((1, C_MIX)), full((C_MIX, C_MIX)),
                  full((C_MIX, D_MODEL)), full((C_MIX, D_MODEL)), full((D_MODEL, D_MODEL))],
        out_specs=rows(D_MODEL),
        out_shape=jax.ShapeDtypeStruct((m, D_MODEL), F32),
        compiler_params=_params("arbitrary"),
        name="merge",
    )(h2, proj2, proj2, y_sb, y_rw, r, k, v, g, ln_w, ln_b, r_k, bd, w_sb, w_rw, w_out)


def _ffn_kernel(h_ref, g_ref, wg_ref, wu_ref, wo_ref, o_ref, hn_ref, acc_ref):
    f = pl.program_id(1)

    @pl.when(f == 0)
    def _():
        x = h_ref[...]
        ms = jnp.mean(x * x, axis=-1, keepdims=True)
        hn_ref[...] = (x * lax.rsqrt(ms + RMS_EPS) * g_ref[...]).astype(BF16)
        acc_ref[...] = x

    hn = hn_ref[...]
    gate = jnp.dot(hn, wg_ref[...], preferred_element_type=F32)
    up = jnp.dot(hn, wu_ref[...], preferred_element_type=F32)
    act = gate * _sigmoid(gate) * up
    acc_ref[...] += jnp.dot(act.astype(BF16), wo_ref[...], preferred_element_type=F32)

    @pl.when(f == pl.num_programs(1) - 1)
    def _():
        o_ref[...] = acc_ref[...]


def _ffn(h2, gain, w_in, w_out):
    m = h2.shape[0]
    tm = _pick(m, (512, 256, 128))
    tf = FFN_HIDDEN // 2
    nf = FFN_HIDDEN // tf
    return pl.pallas_call(
        _ffn_kernel,
        grid=(m // tm, nf),
        in_specs=[pl.BlockSpec((tm, D_MODEL), lambda i, f: (i, 0)),
                  pl.BlockSpec((1, D_MODEL), lambda i, f: (0, 0)),
                  pl.BlockSpec((D_MODEL, tf), lambda i, f: (0, f)),
                  pl.BlockSpec((D_MODEL, tf), lambda i, f: (0, nf + f)),
                  pl.BlockSpec((tf, D_MODEL), lambda i, f: (f, 0))],
        out_specs=pl.BlockSpec((tm, D_MODEL), lambda i, f: (i, 0)),
        out_shape=jax.ShapeDtypeStruct((m, D_MODEL), F32),
        scratch_shapes=[pltpu.VMEM((tm, D_MODEL), BF16), pltpu.VMEM((tm, D_MODEL), F32)],
        compiler_params=_params("arbitrary", "arbitrary"),
        name="ffn",
    )(h2, gain.reshape(1, D_MODEL), w_in, w_in, w_out)


def _final_norm_kernel(h_ref, g_ref, o_ref):
    x = h_ref[...]
    ms = jnp.mean(x * x, axis=-1, keepdims=True)
    o_ref[...] = x * lax.rsqrt(ms + RMS_EPS) * g_ref[...]


def _final_norm(h2, gain):
    m = h2.shape[0]
    tm = _pick(m, (512, 256, 128))
    return pl.pallas_call(
        _final_norm_kernel,
        grid=(m // tm,),
        in_specs=[pl.BlockSpec((tm, D_MODEL), lambda i: (i, 0)), pl.BlockSpec((1, D_MODEL), lambda i: (0, 0))],
        out_specs=pl.BlockSpec((tm, D_MODEL), lambda i: (i, 0)),
        out_shape=jax.ShapeDtypeStruct((m, D_MODEL), F32),
        compiler_params=_params("arbitrary"),
        name="final_norm",
    )(h2, gain.reshape(1, D_MODEL))


def _head_block_diag():
    idx = jnp.arange(C_MIX) // HEAD_DIM
    return (idx[:, None] == idx[None, :]).astype(BF16)


def _lora_weight(w_up, a_up, g_up, vres_up):
    w = jnp.zeros((C_MIX, 4 * C_MIX), F32)
    o = 0
    for seg, (mat, width) in enumerate(((w_up, W_LORA), (a_up, A_LORA), (g_up, G_LORA), (vres_up, V_LORA))):
        if mat is not None:
            w = w.at[o:o + width, seg * C_MIX:(seg + 1) * C_MIX].set(mat)
        o += width
    return w.astype(BF16)


def kernel(x, meta_tokens, norm_mix, norm_ffn, norm_final, w_in, mu_rw, w0, w_up, a0, a_up, g_up, k_k, k_a, r_k, ln_x_w, ln_x_b, vres_down, vres_mu, vres_up, vres0, w_sb_out, w_rw_out, w_out, w_ffn_in, w_ffn_out):
    b, s, d = x.shape
    depth = w_in.shape[0]
    l_real = N_META + s
    l_pad = -(-l_real // SB_BLOCK) * SB_BLOCK
    meta = jnp.broadcast_to(meta_tokens.astype(x.dtype)[None], (b, N_META, d))
    h = jnp.concatenate([meta, x, jnp.zeros((b, l_pad - l_real, d), x.dtype)], axis=1)
    h2 = h.reshape(b * l_pad, d)
    bd = _head_block_diag()
    n_in = 3 * C_MIX + RW_COLS
    row = lambda vec: vec.reshape(1, -1)

    v_first = None
    for layer in range(depth):
        wl = w_in[layer]
        vdown = vres_down[layer - 1] if layer > 0 else jnp.zeros((d, V_LORA), F32)
        w_cat = jnp.concatenate([wl[:, :n_in], vdown, jnp.zeros((d, C_MIX - LORA_USED), F32), wl[:, n_in:]],
                                axis=1).astype(BF16)
        vmu = vres_mu[layer - 1] if layer > 0 else jnp.zeros((V_LORA,), F32)
        mu_main = row(mu_rw[layer, :3 * C_MIX])
        mu_lora = row(jnp.concatenate([mu_rw[layer, 3 * C_MIX:], vmu, jnp.zeros((C_MIX - LORA_USED,), F32)]))
        w_lora = _lora_weight(w_up[layer], a_up[layer], g_up[layer], vres_up[layer - 1] if layer > 0 else None)

        qkv2, proj2 = _norm_proj(h2, norm_mix[layer], w_cat)
        y_sb = _sb_attention(qkv2.reshape(b, l_pad, N_QKV))
        r, ld, k, v, kk, bvec, g = _rw_prep(
            proj2.reshape(b, l_pad, N_REST), v_first, mu_main, mu_lora, row(w0[layer]), row(a0[layer]), row(k_k[layer]), row(k_a[layer]),
            row(vres0[layer - 1]) if layer > 0 else None, w_lora, bd)
        if layer == 0:
            v_first = v
        y_rw = _rw_scan(r, ld, k, v, kk, bvec)
        flat = lambda t: t.reshape(b * l_pad, C_MIX)
        h2 = _merge(h2, proj2, flat(y_sb), flat(y_rw), flat(r), flat(k), flat(v), flat(g),
                    row(ln_x_w[layer]), row(ln_x_b[layer]), row(r_k[layer].reshape(-1)), bd,
                    w_sb_out[layer].astype(BF16), w_rw_out[layer].astype(BF16), w_out[layer].astype(BF16))
        h2 = _ffn(h2, norm_ffn[layer], w_ffn_in[layer].astype(BF16), w_ffn_out[layer].astype(BF16))

    out = _final_norm(h2, norm_final).reshape(b, l_pad, d)
    return out[:, N_META:N_META + s]
```

```python
import functools

import jax
import jax.numpy as jnp
from jax import lax
from jax.experimental import pallas as pl
from jax.experimental.pallas import tpu as pltpu

D_MODEL = 1024
HEAD_DIM = 64
N_HEADS = 8
C_MIX = N_HEADS * HEAD_DIM
N_META = 16
SB_BLOCK = 128
W_LORA, A_LORA, V_LORA, G_LORA = 64, 64, 32, 160
RW_COLS = 3 * C_MIX + W_LORA + A_LORA + G_LORA
FFN_HIDDEN = 2816
RMS_EPS = 1e-6
GN_EPS = 64e-5
RW_CHUNK = 64
SB_UNDERFLOW = 104.0
SB_EAGER_BLOCKS = 2

N_QKV = 3 * C_MIX
COL_RW = 0
COL_LORA = 1536
COL_GATES = 2048
N_REST = 4096
LORA_USED = W_LORA + A_LORA + G_LORA + V_LORA

V7X_VMEM_LIMIT = 56 * 1024 * 1024

F32 = jnp.float32
BF16 = jnp.bfloat16


def _pick(n, cands):
    for c in cands:
        if n % c == 0:
            return c
    raise ValueError(f"no tile for {n} in {cands}")


def _params(*sem):
    return pltpu.CompilerParams(dimension_semantics=sem, vmem_limit_bytes=V7X_VMEM_LIMIT)


def _mm(a, b):
    return jnp.dot(a.astype(BF16), b.astype(BF16), preferred_element_type=F32)


def _mm_nt(a, b):
    return lax.dot_general(a.astype(BF16), b.astype(BF16), (((1,), (1,)), ((), ())),
                           preferred_element_type=F32)


def _mm_tn(a, b):
    return lax.dot_general(a.astype(BF16), b.astype(BF16), (((0,), (0,)), ((), ())),
                           preferred_element_type=F32)


def _each(fn, *lists):
    return [fn(*xs) for xs in zip(*lists)]


def _split_dot(x, w_bf16):
    hi = x.astype(BF16)
    lo = (x - hi.astype(F32)).astype(BF16)
    return (jnp.dot(hi, w_bf16, preferred_element_type=F32)
            + jnp.dot(lo, w_bf16, preferred_element_type=F32))


def _softplus(u):
    return jnp.maximum(u, 0.0) + jnp.log(1.0 + jnp.exp(-jnp.abs(u)))


def _sigmoid(u):
    return 1.0 / (1.0 + jnp.exp(-u))


def _rms_norm(x, gain):
    ms = jnp.mean(x * x, axis=-1, keepdims=True)
    return x * lax.rsqrt(ms + RMS_EPS) * gain


def _norm_proj_kernel(h_ref, g_ref, w_ref, qkv_ref, rest_ref, xn_ref, *, n_qkv_blocks):
    j = pl.program_id(1)

    @pl.when(j == 0)
    def _():
        xn_ref[...] = _rms_norm(h_ref[...], g_ref[...]).astype(BF16)

    acc = jnp.dot(xn_ref[...], w_ref[...], preferred_element_type=F32)

    @pl.when(j < n_qkv_blocks)
    def _():
        qkv_ref[...] = acc.astype(BF16)

    @pl.when(j >= n_qkv_blocks)
    def _():
        rest_ref[...] = acc


def _norm_proj(h2, gain, w_cat):
    m = h2.shape[0]
    n = w_cat.shape[1]
    tm = _pick(m, (1536, 1024, 512, 256, 128))
    tn = 512
    nq = N_QKV // tn
    return pl.pallas_call(
        functools.partial(_norm_proj_kernel, n_qkv_blocks=nq),
        grid=(m // tm, n // tn),
        in_specs=[pl.BlockSpec((tm, D_MODEL), lambda i, j: (i, 0)),
                  pl.BlockSpec((1, D_MODEL), lambda i, j: (0, 0)),
                  pl.BlockSpec((D_MODEL, tn), lambda i, j: (0, j))],
        out_specs=[pl.BlockSpec((tm, tn), lambda i, j: (i, jnp.minimum(j, nq - 1))),
                   pl.BlockSpec((tm, tn), lambda i, j: (i, jnp.maximum(j - nq, 0)))],
        out_shape=[jax.ShapeDtypeStruct((m, N_QKV), BF16), jax.ShapeDtypeStruct((m, n - N_QKV), F32)],
        scratch_shapes=[pltpu.VMEM((tm, D_MODEL), BF16)],
        compiler_params=_params("arbitrary", "arbitrary"),
        name="norm_proj",
    )(h2, gain.reshape(1, D_MODEL), w_cat)


def _sb_kernel(q_ref, k_ref, v_ref, lm_ref, o_ref, *, tq):
    qi = pl.program_id(1)
    n_pairs = N_HEADS // 2
    pw = 2 * HEAD_DIM
    lane = lax.broadcasted_iota(jnp.int32, (1, pw), 1)
    head_a = lane < HEAD_DIM
    zero_bf = jnp.zeros((), BF16)
    split_rows = lambda x: jnp.concatenate([jnp.where(head_a, x, zero_bf), jnp.where(head_a, zero_bf, x)], axis=0)
    q = q_ref[0] * jnp.asarray(HEAD_DIM ** -0.5, BF16)
    q2 = [split_rows(q[:, p * pw:(p + 1) * pw]) for p in range(n_pairs)]
    row = lax.broadcasted_iota(jnp.int32, (2 * tq, tq), 0)
    col = lax.broadcasted_iota(jnp.int32, (2 * tq, tq), 1)
    causal2 = col < jnp.where(row >= tq, row - tq, row)
    lm = lm_ref[...]

    def visit_many(blocks, r_run, acc):
        chains = [(bi, p) for bi in range(len(blocks)) for p in range(n_pairs)]
        starts = [pl.multiple_of(blk[0] * tq, tq) for blk in blocks]
        ks = [k_ref[0, pl.ds(st, tq), :] for st in starts]
        vs = [v_ref[0, pl.ds(st, tq), :] for st in starts]
        vs = [v if blk[2] is None else jnp.where(blk[2], v, zero_bf) for v, blk in zip(vs, blocks)]
        z = [lax.dot_general(q2[p], ks[bi][:, p * pw:(p + 1) * pw], (((1,), (1,)), ((), ())),
                             preferred_element_type=F32) for bi, p in chains]
        sp = [_softplus(zi) for zi in z]
        sp = [jnp.where(causal2, s, 0.0) if blocks[bi][1] else s for s, (bi, p) in zip(sp, chains)]
        hi = [s.astype(BF16) for s in sp]
        lo = [(s - h.astype(F32)).astype(BF16) for s, h in zip(sp, hi)]
        wm = [jnp.dot(jnp.concatenate([h, l_], axis=1), lm, preferred_element_type=F32) for h, l_ in zip(hi, lo)]
        r_run, acc = list(r_run), list(acc)
        for i, (bi, p) in enumerate(chains):
            a = jnp.exp(z[i] - sp[i] - wm[i][:, :tq] - r_run[p])
            if blocks[bi][1]:
                a = jnp.where(causal2, a, 0.0)
            a = a.astype(BF16)
            v2 = split_rows(vs[bi][:, p * pw:(p + 1) * pw])
            acc[p] = acc[p] + jnp.dot(jnp.concatenate([a[:tq], a[tq:]], axis=1), v2, preferred_element_type=F32)
            r_run[p] = r_run[p] + wm[i][:, tq:]
        return r_run, acc

    eager = [(qi, True, None)] + [(jnp.maximum(qi - back, 0), False, qi >= back)
                                  for back in range(1, SB_EAGER_BLOCKS + 1)]
    r_run, acc = visit_many(eager, [jnp.zeros((2 * tq, tq), F32)] * n_pairs, [jnp.zeros((tq, pw), F32)] * n_pairs)

    def more(c):
        j, rr, _ = c
        nearest = functools.reduce(jnp.minimum, [jnp.min(r[:, :1]) for r in rr])
        return jnp.logical_and(j >= 0, nearest < SB_UNDERFLOW)

    def far(c):
        j, rr, ac = c
        rr, ac = visit_many([(j, False, None)], rr, ac)
        return j - 1, rr, ac

    _, _, acc = lax.while_loop(more, far, (qi - 1 - SB_EAGER_BLOCKS, r_run, acc))
    o_ref[0] = jnp.concatenate(acc, axis=1).astype(o_ref.dtype)


def _later_and_ones(tq):
    s_from = jnp.arange(2 * tq)[:, None] % tq
    s_to = jnp.arange(2 * tq)[None, :]
    return jnp.where(s_to < tq, s_from > s_to, True).astype(BF16)


def _sb_attention(qkv3):
    b, l, _ = qkv3.shape
    tq = SB_BLOCK
    return pl.pallas_call(
        functools.partial(_sb_kernel, tq=tq),
        grid=(b, l // tq),
        in_specs=[pl.BlockSpec((1, tq, C_MIX), lambda bi, qi: (bi, qi, 0)),
                  pl.BlockSpec((1, l, C_MIX), lambda bi, qi: (bi, 0, 1)),
                  pl.BlockSpec((1, l, C_MIX), lambda bi, qi: (bi, 0, 2)),
                  pl.BlockSpec((2 * tq, 2 * tq), lambda bi, qi: (0, 0))],
        out_specs=pl.BlockSpec((1, tq, C_MIX), lambda bi, qi: (bi, qi, 0)),
        out_shape=jax.ShapeDtypeStruct((b, l, C_MIX), BF16),
        compiler_params=_params("arbitrary", "arbitrary"),
        name="sb_attention",
    )(qkv3, qkv3, qkv3, _later_and_ones(tq))


def _unit_lower_inverse(a_strict, rowi, coli):
    eye = (rowi == coli).astype(F32)
    same = lambda sh: (rowi >> sh) == (coli >> sh)
    m8, m16, m32 = same(3), same(4), same(5)
    n1 = [-jnp.where(m8, a, 0.0) for a in a_strict]
    n2 = _each(_mm, n1, n1)
    n4 = _each(_mm, n2, n2)
    t = [eye + n for n in n1]
    t = _each(lambda ti, ni: ti + _mm(ti, ni), t, n2)
    t = _each(lambda ti, ni: ti + _mm(ti, ni), t, n4)
    for inner, outer in ((m8, m16), (m16, m32), (m32, None)):
        off = jnp.logical_not(inner) if outer is None else jnp.logical_and(outer, jnp.logical_not(inner))
        ta = _each(lambda ti, a: _mm(ti, jnp.where(off, a, 0.0)), t, a_strict)
        t = _each(lambda ti, tai: ti - _mm(tai, ti), t, ta)
    return t


def _rw_mix_kernel(*refs, has_vres):
    if has_vres:
        (main_ref, lora_ref, vfirst_ref, mu_main_ref, mu_lora_ref, w0_ref, a0_ref, kk_ref, ka_ref, vres0_ref,
         wl_ref, bd_ref, lnw_ref, lnb_ref, rk_ref, y_ref, s_ref, pm_ref, pl_ref) = refs
        vout_ref = None
    else:
        (main_ref, lora_ref, mu_main_ref, mu_lora_ref, w0_ref, a0_ref, kk_ref, ka_ref,
         wl_ref, bd_ref, lnw_ref, lnb_ref, rk_ref, y_ref, vout_ref, s_ref, pm_ref, pl_ref) = refs
    c = RW_CHUNK
    n_batch = main_ref.shape[0]
    n_chains = n_batch * N_HEADS

    @pl.when(pl.program_id(0) == 0)
    def _():
        s_ref[...] = jnp.zeros_like(s_ref)
        pm_ref[...] = jnp.zeros_like(pm_ref)
        pl_ref[...] = jnp.zeros_like(pl_ref)

    first_row = lax.broadcasted_iota(jnp.int32, (c, 1), 0) == 0

    def shifted(x, prev8, mu):
        prev = jnp.where(first_row, prev8[7:8, :], pltpu.roll(x, shift=1, axis=0))
        return x + (prev - x) * mu

    mains = [main_ref[bi] for bi in range(n_batch)]
    loras = [lora_ref[bi] for bi in range(n_batch)]
    xs = jnp.concatenate([shifted(x, pm_ref[bi], mu_main_ref[...]) for bi, x in enumerate(mains)], axis=0)
    lo = jnp.concatenate([shifted(x, pl_ref[bi], mu_lora_ref[...]) for bi, x in enumerate(loras)], axis=0)
    for bi in range(n_batch):
        pm_ref[bi] = mains[bi][c - 8:c, :]
        pl_ref[bi] = loras[bi][c - 8:c, :]

    r = xs[:, 0:C_MIX]
    kr = xs[:, C_MIX:2 * C_MIX]
    vr = xs[:, 2 * C_MIX:3 * C_MIX]
    lane = lax.broadcasted_iota(jnp.int32, (1, C_MIX), 1)
    act = jnp.where(lane < W_LORA, jnp.tanh(lo),
                    jnp.where(jnp.logical_and(lane >= W_LORA + A_LORA, lane < W_LORA + A_LORA + G_LORA),
                              _sigmoid(lo), lo))
    pre = jnp.dot(act.astype(BF16), wl_ref[...], preferred_element_type=F32)
    w_log = -_softplus(-(w0_ref[...] + pre[:, 0:C_MIX])) - 0.5
    ld = -jnp.exp(w_log)
    a = _sigmoid(a0_ref[...] + pre[:, C_MIX:2 * C_MIX])
    g = pre[:, 2 * C_MIX:3 * C_MIX]
    if has_vres:
        v_first = jnp.concatenate([vfirst_ref[bi] for bi in range(n_batch)], axis=0)
        vr = vr + (v_first - vr) * _sigmoid(vres0_ref[...] + pre[:, 3 * C_MIX:4 * C_MIX])
    else:
        for bi in range(n_batch):
            vout_ref[bi] = vr[bi * c:(bi + 1) * c]
    bd = bd_ref[...]
    kk = kr * kk_ref[...]
    kk = kk / jnp.maximum(jnp.sqrt(_split_dot(kk * kk, bd)), 1e-12)
    k = kr * (1.0 + (a - 1.0) * ka_ref[...])
    bvec = kk * a

    rowi = lax.broadcasted_iota(jnp.int32, (c, c), 0)
    coli = lax.broadcasted_iota(jnp.int32, (c, c), 1)
    lower_incl = coli <= rowi
    lower_strict = coli < rowi
    tri = lower_incl.astype(F32)
    rows = lambda x, bi: x[bi * c:(bi + 1) * c]
    kt_all, bt_all, kn_all, rt_all, bh_all, kh_all, g_end = [], [], [], [], [], [], []
    for bi in range(n_batch):
        ld_b = rows(ld, bi)
        cum = jnp.dot(tri, ld_b, precision=lax.Precision.HIGHEST, preferred_element_type=F32)
        cum_end = cum[c - 1:c, :]
        e_neg = jnp.exp(-cum)
        to_end = jnp.exp(cum_end - cum)
        kt_all.append(rows(kk, bi) * jnp.exp(cum - ld_b))
        bt_all.append(rows(bvec, bi) * e_neg)
        kn_all.append(rows(k, bi) * e_neg)
        rt_all.append(rows(r, bi) * jnp.exp(cum))
        bh_all.append(rows(bvec, bi) * to_end)
        kh_all.append(rows(k, bi) * to_end)
        g_end.append(jnp.exp(cum_end))
    v_all = [rows(vr, bi) for bi in range(n_batch)]

    heads = lambda xb: [x[:, h * HEAD_DIM:(h + 1) * HEAD_DIM] for x in xb for h in range(N_HEADS)]
    kt, bt, kn, rt, bh, kh, v, g_h = (heads(x) for x in (kt_all, bt_all, kn_all, rt_all, bh_all, kh_all, v_all,
                                                        g_end))
    a_b = _each(lambda x, y: jnp.where(lower_strict, _mm_nt(x, y), 0.0), kt, bt)
    a_k = _each(lambda x, y: jnp.where(lower_strict, _mm_nt(x, y), 0.0), kt, kn)
    a_rb = _each(lambda x, y: jnp.where(lower_incl, _mm_nt(x, y), 0.0), rt, bt)
    a_rk = _each(lambda x, y: jnp.where(lower_incl, _mm_nt(x, y), 0.0), rt, kn)
    akv = _each(_mm, a_k, v)
    arkv = _each(_mm, a_rk, v)
    vkh = _each(_mm_tn, v, kh)
    t_inv = _unit_lower_inverse(a_b, rowi, coli)
    k_hat = _each(_mm, t_inv, kt)
    u_bar = _each(_mm, t_inv, akv)
    r_hat = _each(lambda x, am, kh_: x - _mm(am, kh_), rt, a_rb, k_hat)
    y_bar = _each(lambda x, am, u: x - _mm(am, u), arkv, a_rb, u_bar)
    e_bar = _each(_mm_tn, k_hat, bh)
    h_add = _each(lambda x, u, b_: x - _mm_tn(u, b_), vkh, u_bar, bh)
    s = [s_ref[h] for h in range(n_chains)]
    ys = _each(lambda rh, sh, yb: _mm_nt(rh, sh) + yb, r_hat, s, y_bar)
    s_new = _each(lambda sh, gh, eb, ha: sh * gh - _mm(sh, eb) + ha, s, g_h, e_bar, h_add)
    for h in range(n_chains):
        s_ref[h] = s_new[h]

    y = jnp.concatenate([jnp.concatenate(ys[bi * N_HEADS:(bi + 1) * N_HEADS], axis=-1) for bi in range(n_batch)],
                        axis=0)
    inv_n = 1.0 / HEAD_DIM
    mean = _split_dot(y, bd) * inv_n
    d = y - mean
    var = _split_dot(d * d, bd) * inv_n
    yn = d * lax.rsqrt(var + GN_EPS) * lnw_ref[...] + lnb_ref[...]
    bonus = _split_dot(r * k * rk_ref[...], bd) * vr
    out = (yn + bonus) * g
    for bi in range(n_batch):
        y_ref[bi] = out[bi * c:(bi + 1) * c].astype(y_ref.dtype)


def _rw_mix(rest3, v_first, mu_main, mu_lora, w0, a0, k_k, k_a, vres0, w_lora, bd, ln_w, ln_b, r_k):
    b, l, _ = rest3.shape
    has_vres = v_first is not None
    c = RW_CHUNK
    tok = pl.BlockSpec((b, c, C_MIX), lambda ci: (0, ci, 0))
    row = lambda width: pl.BlockSpec((1, width), lambda ci: (0, 0))
    in_specs = [pl.BlockSpec((b, c, 3 * C_MIX), lambda ci: (0, ci, COL_RW // (3 * C_MIX))),
                pl.BlockSpec((b, c, C_MIX), lambda ci: (0, ci, COL_LORA // C_MIX))]
    args = [rest3, rest3]
    if has_vres:
        in_specs.append(tok)
        args.append(v_first)
    in_specs += [row(3 * C_MIX)] + [row(C_MIX)] * 5
    args += [mu_main, mu_lora, w0, a0, k_k, k_a]
    if has_vres:
        in_specs.append(row(C_MIX))
        args.append(vres0)
    in_specs += [pl.BlockSpec((C_MIX, 4 * C_MIX), lambda ci: (0, 0)), pl.BlockSpec((C_MIX, C_MIX), lambda ci: (0, 0)),
                 row(C_MIX), row(C_MIX), row(C_MIX)]
    args += [w_lora, bd, ln_w, ln_b, r_k]
    out_shape = [jax.ShapeDtypeStruct((b, l, C_MIX), BF16)]
    out_specs = [tok]
    if not has_vres:
        out_shape.append(jax.ShapeDtypeStruct((b, l, C_MIX), F32))
        out_specs.append(tok)
    res = pl.pallas_call(
        functools.partial(_rw_mix_kernel, has_vres=has_vres),
        grid=(l // c,),
        in_specs=in_specs,
        out_specs=out_specs,
        out_shape=out_shape,
        scratch_shapes=[pltpu.VMEM((b * N_HEADS, HEAD_DIM, HEAD_DIM), F32),
                        pltpu.VMEM((b, 8, 3 * C_MIX), F32), pltpu.VMEM((b, 8, C_MIX), F32)],
        compiler_params=_params("arbitrary"),
        name="rw_mix",
    )(*args)
    return (res[0], None) if has_vres else (res[0], res[1])


def _merge_kernel(h_ref, gsb_ref, grw_ref, ysb_ref, yrw_ref, wsb_ref, wrw_ref, wout_ref, o_ref):
    o_sb = jnp.dot(ysb_ref[...], wsb_ref[...], preferred_element_type=F32)
    o_rw = jnp.dot(yrw_ref[...], wrw_ref[...], preferred_element_type=F32)
    merged = _sigmoid(gsb_ref[...]) * o_sb + _sigmoid(grw_ref[...]) * o_rw
    o_ref[...] = h_ref[...] + jnp.dot(merged.astype(BF16), wout_ref[...], preferred_element_type=F32)


def _merge(h2, rest2, y_sb, y_rw, w_sb, w_rw, w_out):
    m = h2.shape[0]
    tm = _pick(m, (512, 256, 128))
    rows = lambda width: pl.BlockSpec((tm, width), lambda i: (i, 0))
    full = lambda shape: pl.BlockSpec(shape, lambda i: (0, 0))
    return pl.pallas_call(
        _merge_kernel,
        grid=(m // tm,),
        in_specs=[rows(D_MODEL),
                  pl.BlockSpec((tm, D_MODEL), lambda i: (i, COL_GATES // D_MODEL)),
                  pl.BlockSpec((tm, D_MODEL), lambda i: (i, COL_GATES // D_MODEL + 1)),
                  rows(C_MIX), rows(C_MIX),
                  full((C_MIX, D_MODEL)), full((C_MIX, D_MODEL)), full((D_MODEL, D_MODEL))],
        out_specs=rows(D_MODEL),
        out_shape=jax.ShapeDtypeStruct((m, D_MODEL), F32),
        compiler_params=_params("arbitrary"),
        name="merge",
    )(h2, rest2, rest2, y_sb, y_rw, w_sb, w_rw, w_out)


def _ffn_kernel(h_ref, g_ref, wg_ref, wu_ref, wo_ref, o_ref):
    x = h_ref[...]
    hn = _rms_norm(x, g_ref[...]).astype(BF16)
    gate = jnp.dot(hn, wg_ref[...], preferred_element_type=F32)
    up = jnp.dot(hn, wu_ref[...], preferred_element_type=F32)
    act = gate * _sigmoid(gate) * up
    o_ref[...] = x + jnp.dot(act.astype(BF16), wo_ref[...], preferred_element_type=F32)


def _ffn(h2, gain, w_in, w_out):
    m = h2.shape[0]
    tm = _pick(m, (384, 256, 128))
    return pl.pallas_call(
        _ffn_kernel,
        grid=(m // tm,),
        in_specs=[pl.BlockSpec((tm, D_MODEL), lambda i: (i, 0)),
                  pl.BlockSpec((1, D_MODEL), lambda i: (0, 0)),
                  pl.BlockSpec((D_MODEL, FFN_HIDDEN), lambda i: (0, 0)),
                  pl.BlockSpec((D_MODEL, FFN_HIDDEN), lambda i: (0, 1)),
                  pl.BlockSpec((FFN_HIDDEN, D_MODEL), lambda i: (0, 0))],
        out_specs=pl.BlockSpec((tm, D_MODEL), lambda i: (i, 0)),
        out_shape=jax.ShapeDtypeStruct((m, D_MODEL), F32),
        compiler_params=_params("arbitrary"),
        name="ffn",
    )(h2, gain.reshape(1, D_MODEL), w_in, w_in, w_out)


def _final_norm_kernel(h_ref, g_ref, o_ref):
    o_ref[...] = _rms_norm(h_ref[...], g_ref[...])


def _final_norm(h3, gain, s):
    b = h3.shape[0]
    tm = _pick(s, (512, 256, 128, 16))
    return pl.pallas_call(
        _final_norm_kernel,
        grid=(b, s // tm),
        in_specs=[pl.BlockSpec((pl.Element(1), pl.Element(tm), pl.Element(D_MODEL)),
                               lambda bi, i: (bi, pl.multiple_of(i * tm + N_META, N_META), 0)),
                  pl.BlockSpec((1, 1, D_MODEL), lambda bi, i: (0, 0, 0))],
        out_specs=pl.BlockSpec((1, tm, D_MODEL), lambda bi, i: (bi, i, 0)),
        out_shape=jax.ShapeDtypeStruct((b, s, D_MODEL), F32),
        compiler_params=_params("arbitrary", "arbitrary"),
        name="final_norm",
    )(h3, gain.reshape(1, 1, D_MODEL))


def _head_block_diag():
    idx = jnp.arange(C_MIX) // HEAD_DIM
    return (idx[:, None] == idx[None, :]).astype(BF16)


def _lora_weight(w_up, a_up, g_up, vres_up):
    w = jnp.zeros((C_MIX, 4 * C_MIX), F32)
    o = 0
    for seg, (mat, width) in enumerate(((w_up, W_LORA), (a_up, A_LORA), (g_up, G_LORA), (vres_up, V_LORA))):
        if mat is not None:
            w = w.at[o:o + width, seg * C_MIX:(seg + 1) * C_MIX].set(mat)
        o += width
    return w.astype(BF16)


def kernel(x, meta_tokens, norm_mix, norm_ffn, norm_final, w_in, mu_rw, w0, w_up, a0, a_up, g_up, k_k, k_a, r_k, ln_x_w, ln_x_b, vres_down, vres_mu, vres_up, vres0, w_sb_out, w_rw_out, w_out, w_ffn_in, w_ffn_out):
    b, s, d = x.shape
    depth = w_in.shape[0]
    l_real = N_META + s
    l_pad = -(-l_real // SB_BLOCK) * SB_BLOCK
    meta = jnp.broadcast_to(meta_tokens.astype(x.dtype)[None], (b, N_META, d))
    h = jnp.concatenate([meta, x, jnp.zeros((b, l_pad - l_real, d), x.dtype)], axis=1)
    h2 = h.reshape(b * l_pad, d)
    bd = _head_block_diag()
    n_in = 3 * C_MIX + RW_COLS
    row = lambda vec: vec.reshape(1, -1)

    v_first = None
    for layer in range(depth):
        wl = w_in[layer]
        vdown = vres_down[layer - 1] if layer > 0 else jnp.zeros((d, V_LORA), F32)
        w_cat = jnp.concatenate([wl[:, :n_in], vdown, jnp.zeros((d, C_MIX - LORA_USED), F32), wl[:, n_in:]],
                                axis=1).astype(BF16)
        vmu = vres_mu[layer - 1] if layer > 0 else jnp.zeros((V_LORA,), F32)
        mu_main = row(mu_rw[layer, :3 * C_MIX])
        mu_lora = row(jnp.concatenate([mu_rw[layer, 3 * C_MIX:], vmu, jnp.zeros((C_MIX - LORA_USED,), F32)]))
        w_lora = _lora_weight(w_up[layer], a_up[layer], g_up[layer], vres_up[layer - 1] if layer > 0 else None)

        qkv2, rest2 = _norm_proj(h2, norm_mix[layer], w_cat)
        y_sb = _sb_attention(qkv2.reshape(b, l_pad, N_QKV))
        y_rw, v_out = _rw_mix(
            rest2.reshape(b, l_pad, N_REST), v_first, mu_main, mu_lora, row(w0[layer]), row(a0[layer]),
            row(k_k[layer]), row(k_a[layer]), row(vres0[layer - 1]) if layer > 0 else None, w_lora, bd,
            row(ln_x_w[layer]), row(ln_x_b[layer]), row(r_k[layer].reshape(-1)))
        if layer == 0:
            v_first = v_out
        flat = lambda t: t.reshape(b * l_pad, C_MIX)
        h2 = _merge(h2, rest2, flat(y_sb), flat(y_rw),
                    w_sb_out[layer].astype(BF16), w_rw_out[layer].astype(BF16), w_out[layer].astype(BF16))
        h2 = _ffn(h2, norm_ffn[layer], w_ffn_in[layer].astype(BF16), w_ffn_out[layer].astype(BF16))

    return _final_norm(h2.reshape(b, l_pad, d), norm_final, s)
```

```python
import functools

import jax
import jax.numpy as jnp
from jax import lax
from jax.experimental import pallas as pl
from jax.experimental.pallas import tpu as pltpu

D_MODEL = 1024
HEAD_DIM = 64
N_HEADS = 8
C_MIX = N_HEADS * HEAD_DIM
N_META = 16
SB_BLOCK = 128
W_LORA, A_LORA, V_LORA, G_LORA = 64, 64, 32, 160
RW_COLS = 3 * C_MIX + W_LORA + A_LORA + G_LORA
FFN_HIDDEN = 2816
RMS_EPS = 1e-6
GN_EPS = 64e-5
RW_CHUNK = 64
SB_UNDERFLOW = 104.0
SB_EAGER_BLOCKS = 2

N_QKV = 3 * C_MIX
COL_RW = 0
COL_LORA = 1536
COL_GATES = 2048
N_REST = 4096
LORA_USED = W_LORA + A_LORA + G_LORA + V_LORA

V7X_VMEM_LIMIT = 56 * 1024 * 1024

F32 = jnp.float32
BF16 = jnp.bfloat16


def _pick(n, cands):
    for c in cands:
        if n % c == 0:
            return c
    raise ValueError(f"no tile for {n} in {cands}")


def _params(*sem):
    return pltpu.CompilerParams(dimension_semantics=sem, vmem_limit_bytes=V7X_VMEM_LIMIT)


def _mm(a, b):
    return jnp.dot(a.astype(BF16), b.astype(BF16), preferred_element_type=F32)


def _mm_nt(a, b):
    return lax.dot_general(a.astype(BF16), b.astype(BF16), (((1,), (1,)), ((), ())),
                           preferred_element_type=F32)


def _mm_tn(a, b):
    return lax.dot_general(a.astype(BF16), b.astype(BF16), (((0,), (0,)), ((), ())),
                           preferred_element_type=F32)


def _each(fn, *lists):
    return [fn(*xs) for xs in zip(*lists)]


def _split_dot(x, w_bf16):
    hi = x.astype(BF16)
    lo = (x - hi.astype(F32)).astype(BF16)
    return (jnp.dot(hi, w_bf16, preferred_element_type=F32)
            + jnp.dot(lo, w_bf16, preferred_element_type=F32))


def _softplus(u):
    return jnp.maximum(u, 0.0) + jnp.log(1.0 + jnp.exp(-jnp.abs(u)))


def _sigmoid(u):
    return 1.0 / (1.0 + jnp.exp(-u))


def _rms_norm(x, gain):
    ms = jnp.mean(x * x, axis=-1, keepdims=True)
    return x * lax.rsqrt(ms + RMS_EPS) * gain


def _norm_proj_kernel(h_ref, g_ref, w_ref, qkv_ref, rest_ref):
    xn = _rms_norm(h_ref[...], g_ref[...]).astype(BF16)
    qkv_ref[...] = jnp.dot(xn, w_ref[:, :N_QKV], preferred_element_type=F32).astype(BF16)
    rest_ref[...] = jnp.dot(xn, w_ref[:, N_QKV:], preferred_element_type=F32)


def _norm_proj(h2, gain, w_cat):
    m = h2.shape[0]
    n = w_cat.shape[1]
    tm = _pick(m, (384, 256, 128))
    return pl.pallas_call(
        _norm_proj_kernel,
        grid=(m // tm,),
        in_specs=[pl.BlockSpec((tm, D_MODEL), lambda i: (i, 0)),
                  pl.BlockSpec((1, D_MODEL), lambda i: (0, 0)),
                  pl.BlockSpec((D_MODEL, n), lambda i: (0, 0))],
        out_specs=[pl.BlockSpec((tm, N_QKV), lambda i: (i, 0)),
                   pl.BlockSpec((tm, n - N_QKV), lambda i: (i, 0))],
        out_shape=[jax.ShapeDtypeStruct((m, N_QKV), BF16), jax.ShapeDtypeStruct((m, n - N_QKV), F32)],
        compiler_params=_params("arbitrary"),
        name="norm_proj",
    )(h2, gain.reshape(1, D_MODEL), w_cat)


def _sb_kernel(q_ref, k_ref, v_ref, lm_ref, o_ref, *, tq):
    qi = pl.program_id(1)
    n_pairs = N_HEADS // 2
    pw = 2 * HEAD_DIM
    lane = lax.broadcasted_iota(jnp.int32, (1, pw), 1)
    head_a = lane < HEAD_DIM
    zero_bf = jnp.zeros((), BF16)
    split_rows = lambda x: jnp.concatenate([jnp.where(head_a, x, zero_bf), jnp.where(head_a, zero_bf, x)], axis=0)
    q = q_ref[0] * jnp.asarray(HEAD_DIM ** -0.5, BF16)
    q2 = [split_rows(q[:, p * pw:(p + 1) * pw]) for p in range(n_pairs)]
    row = lax.broadcasted_iota(jnp.int32, (2 * tq, tq), 0)
    col = lax.broadcasted_iota(jnp.int32, (2 * tq, tq), 1)
    causal2 = col < jnp.where(row >= tq, row - tq, row)
    lm = lm_ref[...]

    def visit_many(blocks, r_run, acc):
        chains = [(bi, p) for bi in range(len(blocks)) for p in range(n_pairs)]
        starts = [pl.multiple_of(blk[0] * tq, tq) for blk in blocks]
        ks = [k_ref[0, pl.ds(st, tq), :] for st in starts]
        vs = [v_ref[0, pl.ds(st, tq), :] for st in starts]
        vs = [v if blk[2] is None else jnp.where(blk[2], v, zero_bf) for v, blk in zip(vs, blocks)]
        z = [lax.dot_general(q2[p], ks[bi][:, p * pw:(p + 1) * pw], (((1,), (1,)), ((), ())),
                             preferred_element_type=F32) for bi, p in chains]
        sp = [_softplus(zi) for zi in z]
        sp = [jnp.where(causal2, s, 0.0) if blocks[bi][1] else s for s, (bi, p) in zip(sp, chains)]
        hi = [s.astype(BF16) for s in sp]
        lo = [(s - h.astype(F32)).astype(BF16) for s, h in zip(sp, hi)]
        wm = [jnp.dot(jnp.concatenate([h, l_], axis=1), lm, preferred_element_type=F32) for h, l_ in zip(hi, lo)]
        r_run, acc = list(r_run), list(acc)
        for i, (bi, p) in enumerate(chains):
            a = jnp.exp(z[i] - sp[i] - wm[i][:, :tq] - r_run[p])
            if blocks[bi][1]:
                a = jnp.where(causal2, a, 0.0)
            a = a.astype(BF16)
            v2 = split_rows(vs[bi][:, p * pw:(p + 1) * pw])
            acc[p] = acc[p] + jnp.dot(jnp.concatenate([a[:tq], a[tq:]], axis=1), v2, preferred_element_type=F32)
            r_run[p] = r_run[p] + wm[i][:, tq:]
        return r_run, acc

    eager = [(qi, True, None)] + [(jnp.maximum(qi - back, 0), False, qi >= back)
                                  for back in range(1, SB_EAGER_BLOCKS + 1)]
    r_run, acc = visit_many(eager, [jnp.zeros((2 * tq, tq), F32)] * n_pairs, [jnp.zeros((tq, pw), F32)] * n_pairs)

    def more(c):
        j, rr, _ = c
        nearest = functools.reduce(jnp.minimum, [jnp.min(r[:, :1]) for r in rr])
        return jnp.logical_and(j >= 0, nearest < SB_UNDERFLOW)

    def far(c):
        j, rr, ac = c
        rr, ac = visit_many([(j, False, None)], rr, ac)
        return j - 1, rr, ac

    _, _, acc = lax.while_loop(more, far, (qi - 1 - SB_EAGER_BLOCKS, r_run, acc))
    o_ref[0] = jnp.concatenate(acc, axis=1).astype(o_ref.dtype)


def _later_and_ones(tq):
    s_from = jnp.arange(2 * tq)[:, None] % tq
    s_to = jnp.arange(2 * tq)[None, :]
    return jnp.where(s_to < tq, s_from > s_to, True).astype(BF16)


def _sb_attention(qkv3):
    b, l, _ = qkv3.shape
    tq = SB_BLOCK
    return pl.pallas_call(
        functools.partial(_sb_kernel, tq=tq),
        grid=(b, l // tq),
        in_specs=[pl.BlockSpec((1, tq, C_MIX), lambda bi, qi: (bi, qi, 0)),
                  pl.BlockSpec((1, l, C_MIX), lambda bi, qi: (bi, 0, 1)),
                  pl.BlockSpec((1, l, C_MIX), lambda bi, qi: (bi, 0, 2)),
                  pl.BlockSpec((2 * tq, 2 * tq), lambda bi, qi: (0, 0))],
        out_specs=pl.BlockSpec((1, tq, C_MIX), lambda bi, qi: (bi, qi, 0)),
        out_shape=jax.ShapeDtypeStruct((b, l, C_MIX), BF16),
        compiler_params=_params("arbitrary", "arbitrary"),
        name="sb_attention",
    )(qkv3, qkv3, qkv3, _later_and_ones(tq))


def _unit_lower_inverse(a_strict, rowi, coli):
    eye = (rowi == coli).astype(F32)
    same = lambda sh: (rowi >> sh) == (coli >> sh)
    m8, m16, m32 = same(3), same(4), same(5)
    n1 = [-jnp.where(m8, a, 0.0) for a in a_strict]
    n2 = _each(_mm, n1, n1)
    n4 = _each(_mm, n2, n2)
    t = [eye + n for n in n1]
    t = _each(lambda ti, ni: ti + _mm(ti, ni), t, n2)
    t = _each(lambda ti, ni: ti + _mm(ti, ni), t, n4)
    for inner, outer in ((m8, m16), (m16, m32), (m32, None)):
        off = jnp.logical_not(inner) if outer is None else jnp.logical_and(outer, jnp.logical_not(inner))
        ta = _each(lambda ti, a: _mm(ti, jnp.where(off, a, 0.0)), t, a_strict)
        t = _each(lambda ti, tai: ti - _mm(tai, ti), t, ta)
    return t


def _rw_mix_kernel(*refs, has_vres):
    if has_vres:
        (main_ref, lora_ref, vfirst_ref, mu_main_ref, mu_lora_ref, w0_ref, a0_ref, kk_ref, ka_ref, vres0_ref,
         wl_ref, bd_ref, lnw_ref, lnb_ref, rk_ref, y_ref, s_ref, pm_ref, pl_ref) = refs
        vout_ref = None
    else:
        (main_ref, lora_ref, mu_main_ref, mu_lora_ref, w0_ref, a0_ref, kk_ref, ka_ref,
         wl_ref, bd_ref, lnw_ref, lnb_ref, rk_ref, y_ref, vout_ref, s_ref, pm_ref, pl_ref) = refs
    c = RW_CHUNK
    n_batch = main_ref.shape[0]
    n_chains = n_batch * N_HEADS

    @pl.when(pl.program_id(0) == 0)
    def _():
        s_ref[...] = jnp.zeros_like(s_ref)
        pm_ref[...] = jnp.zeros_like(pm_ref)
        pl_ref[...] = jnp.zeros_like(pl_ref)

    first_row = lax.broadcasted_iota(jnp.int32, (c, 1), 0) == 0

    def shifted(x, prev8, mu):
        prev = jnp.where(first_row, prev8[7:8, :], pltpu.roll(x, shift=1, axis=0))
        return x + (prev - x) * mu

    mains = [main_ref[bi] for bi in range(n_batch)]
    loras = [lora_ref[bi] for bi in range(n_batch)]
    xs = jnp.concatenate([shifted(x, pm_ref[bi], mu_main_ref[...]) for bi, x in enumerate(mains)], axis=0)
    lo = jnp.concatenate([shifted(x, pl_ref[bi], mu_lora_ref[...]) for bi, x in enumerate(loras)], axis=0)
    for bi in range(n_batch):
        pm_ref[bi] = mains[bi][c - 8:c, :]
        pl_ref[bi] = loras[bi][c - 8:c, :]

    r = xs[:, 0:C_MIX]
    kr = xs[:, C_MIX:2 * C_MIX]
    vr = xs[:, 2 * C_MIX:3 * C_MIX]
    lane = lax.broadcasted_iota(jnp.int32, (1, C_MIX), 1)
    act = jnp.where(lane < W_LORA, jnp.tanh(lo),
                    jnp.where(jnp.logical_and(lane >= W_LORA + A_LORA, lane < W_LORA + A_LORA + G_LORA),
                              _sigmoid(lo), lo))
    pre = jnp.dot(act.astype(BF16), wl_ref[...], preferred_element_type=F32)
    w_log = -_softplus(-(w0_ref[...] + pre[:, 0:C_MIX])) - 0.5
    ld = -jnp.exp(w_log)
    a = _sigmoid(a0_ref[...] + pre[:, C_MIX:2 * C_MIX])
    g = pre[:, 2 * C_MIX:3 * C_MIX]
    if has_vres:
        v_first = jnp.concatenate([vfirst_ref[bi] for bi in range(n_batch)], axis=0)
        vr = vr + (v_first - vr) * _sigmoid(vres0_ref[...] + pre[:, 3 * C_MIX:4 * C_MIX])
    else:
        for bi in range(n_batch):
            vout_ref[bi] = vr[bi * c:(bi + 1) * c]
    bd = bd_ref[...]
    kk = kr * kk_ref[...]
    kk = kk / jnp.maximum(jnp.sqrt(_split_dot(kk * kk, bd)), 1e-12)
    k = kr * (1.0 + (a - 1.0) * ka_ref[...])
    bvec = kk * a

    rowi = lax.broadcasted_iota(jnp.int32, (c, c), 0)
    coli = lax.broadcasted_iota(jnp.int32, (c, c), 1)
    lower_incl = coli <= rowi
    lower_strict = coli < rowi
    tri = lower_incl.astype(F32)
    rows = lambda x, bi: x[bi * c:(bi + 1) * c]
    kt_all, bt_all, kn_all, rt_all, bh_all, kh_all, g_end = [], [], [], [], [], [], []
    for bi in range(n_batch):
        ld_b = rows(ld, bi)
        cum = jnp.dot(tri, ld_b, precision=lax.Precision.HIGHEST, preferred_element_type=F32)
        cum_end = cum[c - 1:c, :]
        e_neg = jnp.exp(-cum)
        to_end = jnp.exp(cum_end - cum)
        kt_all.append(rows(kk, bi) * jnp.exp(cum - ld_b))
        bt_all.append(rows(bvec, bi) * e_neg)
        kn_all.append(rows(k, bi) * e_neg)
        rt_all.append(rows(r, bi) * jnp.exp(cum))
        bh_all.append(rows(bvec, bi) * to_end)
        kh_all.append(rows(k, bi) * to_end)
        g_end.append(jnp.exp(cum_end))
    v_all = [rows(vr, bi) for bi in range(n_batch)]

    heads = lambda xb: [x[:, h * HEAD_DIM:(h + 1) * HEAD_DIM] for x in xb for h in range(N_HEADS)]
    kt, bt, kn, rt, bh, kh, v, g_h = (heads(x) for x in (kt_all, bt_all, kn_all, rt_all, bh_all, kh_all, v_all,
                                                        g_end))
    a_b = _each(lambda x, y: jnp.where(lower_strict, _mm_nt(x, y), 0.0), kt, bt)
    a_k = _each(lambda x, y: jnp.where(lower_strict, _mm_nt(x, y), 0.0), kt, kn)
    a_rb = _each(lambda x, y: jnp.where(lower_incl, _mm_nt(x, y), 0.0), rt, bt)
    a_rk = _each(lambda x, y: jnp.where(lower_incl, _mm_nt(x, y), 0.0), rt, kn)
    akv = _each(_mm, a_k, v)
    arkv = _each(_mm, a_rk, v)
    vkh = _each(_mm_tn, v, kh)
    t_inv = _unit_lower_inverse(a_b, rowi, coli)
    k_hat = _each(_mm, t_inv, kt)
    u_bar = _each(_mm, t_inv, akv)
    r_hat = _each(lambda x, am, kh_: x - _mm(am, kh_), rt, a_rb, k_hat)
    y_bar = _each(lambda x, am, u: x - _mm(am, u), arkv, a_rb, u_bar)
    e_bar = _each(_mm_tn, k_hat, bh)
    h_add = _each(lambda x, u, b_: x - _mm_tn(u, b_), vkh, u_bar, bh)
    s = [s_ref[h] for h in range(n_chains)]
    ys = _each(lambda rh, sh, yb: _mm_nt(rh, sh) + yb, r_hat, s, y_bar)
    s_new = _each(lambda sh, gh, eb, ha: sh * gh - _mm(sh, eb) + ha, s, g_h, e_bar, h_add)
    for h in range(n_chains):
        s_ref[h] = s_new[h]

    y = jnp.concatenate([jnp.concatenate(ys[bi * N_HEADS:(bi + 1) * N_HEADS], axis=-1) for bi in range(n_batch)],
                        axis=0)
    inv_n = 1.0 / HEAD_DIM
    mean = _split_dot(y, bd) * inv_n
    d = y - mean
    var = _split_dot(d * d, bd) * inv_n
    yn = d * lax.rsqrt(var + GN_EPS) * lnw_ref[...] + lnb_ref[...]
    bonus = _split_dot(r * k * rk_ref[...], bd) * vr
    out = (yn + bonus) * g
    for bi in range(n_batch):
        y_ref[bi] = out[bi * c:(bi + 1) * c].astype(y_ref.dtype)


def _rw_mix(rest3, v_first, mu_main, mu_lora, w0, a0, k_k, k_a, vres0, w_lora, bd, ln_w, ln_b, r_k):
    b, l, _ = rest3.shape
    has_vres = v_first is not None
    c = RW_CHUNK
    tok = pl.BlockSpec((b, c, C_MIX), lambda ci: (0, ci, 0))
    row = lambda width: pl.BlockSpec((1, width), lambda ci: (0, 0))
    in_specs = [pl.BlockSpec((b, c, 3 * C_MIX), lambda ci: (0, ci, COL_RW // (3 * C_MIX))),
                pl.BlockSpec((b, c, C_MIX), lambda ci: (0, ci, COL_LORA // C_MIX))]
    args = [rest3, rest3]
    if has_vres:
        in_specs.append(tok)
        args.append(v_first)
    in_specs += [row(3 * C_MIX)] + [row(C_MIX)] * 5
    args += [mu_main, mu_lora, w0, a0, k_k, k_a]
    if has_vres:
        in_specs.append(row(C_MIX))
        args.append(vres0)
    in_specs += [pl.BlockSpec((C_MIX, 4 * C_MIX), lambda ci: (0, 0)), pl.BlockSpec((C_MIX, C_MIX), lambda ci: (0, 0)),
                 row(C_MIX), row(C_MIX), row(C_MIX)]
    args += [w_lora, bd, ln_w, ln_b, r_k]
    out_shape = [jax.ShapeDtypeStruct((b, l, C_MIX), BF16)]
    out_specs = [tok]
    if not has_vres:
        out_shape.append(jax.ShapeDtypeStruct((b, l, C_MIX), F32))
        out_specs.append(tok)
    res = pl.pallas_call(
        functools.partial(_rw_mix_kernel, has_vres=has_vres),
        grid=(l // c,),
        in_specs=in_specs,
        out_specs=out_specs,
        out_shape=out_shape,
        scratch_shapes=[pltpu.VMEM((b * N_HEADS, HEAD_DIM, HEAD_DIM), F32),
                        pltpu.VMEM((b, 8, 3 * C_MIX), F32), pltpu.VMEM((b, 8, C_MIX), F32)],
        compiler_params=_params("arbitrary"),
        name="rw_mix",
    )(*args)
    return (res[0], None) if has_vres else (res[0], res[1])


def _merge_kernel(h_ref, gsb_ref, grw_ref, ysb_ref, yrw_ref, wsb_ref, wrw_ref, wout_ref, o_ref):
    o_sb = jnp.dot(ysb_ref[...], wsb_ref[...], preferred_element_type=F32)
    o_rw = jnp.dot(yrw_ref[...], wrw_ref[...], preferred_element_type=F32)
    merged = _sigmoid(gsb_ref[...]) * o_sb + _sigmoid(grw_ref[...]) * o_rw
    o_ref[...] = h_ref[...] + jnp.dot(merged.astype(BF16), wout_ref[...], preferred_element_type=F32)


def _merge(h2, rest2, y_sb, y_rw, w_sb, w_rw, w_out):
    m = h2.shape[0]
    tm = _pick(m, (512, 256, 128))
    rows = lambda width: pl.BlockSpec((tm, width), lambda i: (i, 0))
    full = lambda shape: pl.BlockSpec(shape, lambda i: (0, 0))
    return pl.pallas_call(
        _merge_kernel,
        grid=(m // tm,),
        in_specs=[rows(D_MODEL),
                  pl.BlockSpec((tm, D_MODEL), lambda i: (i, COL_GATES // D_MODEL)),
                  pl.BlockSpec((tm, D_MODEL), lambda i: (i, COL_GATES // D_MODEL + 1)),
                  rows(C_MIX), rows(C_MIX),
                  full((C_MIX, D_MODEL)), full((C_MIX, D_MODEL)), full((D_MODEL, D_MODEL))],
        out_specs=rows(D_MODEL),
        out_shape=jax.ShapeDtypeStruct((m, D_MODEL), F32),
        compiler_params=_params("arbitrary"),
        name="merge",
    )(h2, rest2, rest2, y_sb, y_rw, w_sb, w_rw, w_out)


def _ffn_kernel(h_ref, g_ref, wg_ref, wu_ref, wo_ref, o_ref):
    x = h_ref[...]
    hn = _rms_norm(x, g_ref[...]).astype(BF16)
    gate = jnp.dot(hn, wg_ref[...], preferred_element_type=F32)
    up = jnp.dot(hn, wu_ref[...], preferred_element_type=F32)
    act = gate * _sigmoid(gate) * up
    o_ref[...] = x + jnp.dot(act.astype(BF16), wo_ref[...], preferred_element_type=F32)


def _ffn(h2, gain, w_in, w_out):
    m = h2.shape[0]
    tm = _pick(m, (384, 256, 128))
    return pl.pallas_call(
        _ffn_kernel,
        grid=(m // tm,),
        in_specs=[pl.BlockSpec((tm, D_MODEL), lambda i: (i, 0)),
                  pl.BlockSpec((1, D_MODEL), lambda i: (0, 0)),
                  pl.BlockSpec((D_MODEL, FFN_HIDDEN), lambda i: (0, 0)),
                  pl.BlockSpec((D_MODEL, FFN_HIDDEN), lambda i: (0, 1)),
                  pl.BlockSpec((FFN_HIDDEN, D_MODEL), lambda i: (0, 0))],
        out_specs=pl.BlockSpec((tm, D_MODEL), lambda i: (i, 0)),
        out_shape=jax.ShapeDtypeStruct((m, D_MODEL), F32),
        compiler_params=_params("arbitrary"),
        name="ffn",
    )(h2, gain.reshape(1, D_MODEL), w_in, w_in, w_out)


def _final_norm_kernel(h_ref, g_ref, o_ref):
    o_ref[...] = _rms_norm(h_ref[...], g_ref[...])


def _final_norm(h3, gain, s):
    b = h3.shape[0]
    tm = _pick(s, (512, 256, 128, 16))
    return pl.pallas_call(
        _final_norm_kernel,
        grid=(b, s // tm),
        in_specs=[pl.BlockSpec((pl.Element(1), pl.Element(tm), pl.Element(D_MODEL)),
                               lambda bi, i: (bi, pl.multiple_of(i * tm + N_META, N_META), 0)),
                  pl.BlockSpec((1, 1, D_MODEL), lambda bi, i: (0, 0, 0))],
        out_specs=pl.BlockSpec((1, tm, D_MODEL), lambda bi, i: (bi, i, 0)),
        out_shape=jax.ShapeDtypeStruct((b, s, D_MODEL), F32),
        compiler_params=_params("arbitrary", "arbitrary"),
        name="final_norm",
    )(h3, gain.reshape(1, 1, D_MODEL))


def _head_block_diag():
    idx = jnp.arange(C_MIX) // HEAD_DIM
    return (idx[:, None] == idx[None, :]).astype(BF16)


def _lora_weight(w_up, a_up, g_up, vres_up):
    w = jnp.zeros((C_MIX, 4 * C_MIX), F32)
    o = 0
    for seg, (mat, width) in enumerate(((w_up, W_LORA), (a_up, A_LORA), (g_up, G_LORA), (vres_up, V_LORA))):
        if mat is not None:
            w = w.at[o:o + width, seg * C_MIX:(seg + 1) * C_MIX].set(mat)
        o += width
    return w.astype(BF16)


def kernel(x, meta_tokens, norm_mix, norm_ffn, norm_final, w_in, mu_rw, w0, w_up, a0, a_up, g_up, k_k, k_a, r_k, ln_x_w, ln_x_b, vres_down, vres_mu, vres_up, vres0, w_sb_out, w_rw_out, w_out, w_ffn_in, w_ffn_out):
    b, s, d = x.shape
    depth = w_in.shape[0]
    l_real = N_META + s
    l_pad = -(-l_real // SB_BLOCK) * SB_BLOCK
    meta = jnp.broadcast_to(meta_tokens.astype(x.dtype)[None], (b, N_META, d))
    h = jnp.concatenate([meta, x, jnp.zeros((b, l_pad - l_real, d), x.dtype)], axis=1)
    h2 = h.reshape(b * l_pad, d)
    bd = _head_block_diag()
    n_in = 3 * C_MIX + RW_COLS
    row = lambda vec: vec.reshape(1, -1)

    v_first = None
    for layer in range(depth):
        wl = w_in[layer]
        vdown = vres_down[layer - 1] if layer > 0 else jnp.zeros((d, V_LORA), F32)
        w_cat = jnp.concatenate([wl[:, :n_in], vdown, jnp.zeros((d, C_MIX - LORA_USED), F32), wl[:, n_in:]],
                                axis=1).astype(BF16)
        vmu = vres_mu[layer - 1] if layer > 0 else jnp.zeros((V_LORA,), F32)
        mu_main = row(mu_rw[layer, :3 * C_MIX])
        mu_lora = row(jnp.concatenate([mu_rw[layer, 3 * C_MIX:], vmu, jnp.zeros((C_MIX - LORA_USED,), F32)]))
        w_lora = _lora_weight(w_up[layer], a_up[layer], g_up[layer], vres_up[layer - 1] if layer > 0 else None)

        qkv2, rest2 = _norm_proj(h2, norm_mix[layer], w_cat)
        y_sb = _sb_attention(qkv2.reshape(b, l_pad, N_QKV))
        y_rw, v_out = _rw_mix(
            rest2.reshape(b, l_pad, N_REST), v_first, mu_main, mu_lora, row(w0[layer]), row(a0[layer]),
            row(k_k[layer]), row(k_a[layer]), row(vres0[layer - 1]) if layer > 0 else None, w_lora, bd,
            row(ln_x_w[layer]), row(ln_x_b[layer]), row(r_k[layer].reshape(-1)))
        if layer == 0:
            v_first = v_out
        flat = lambda t: t.reshape(b * l_pad, C_MIX)
        h2 = _merge(h2, rest2, flat(y_sb), flat(y_rw),
                    w_sb_out[layer].astype(BF16), w_rw_out[layer].astype(BF16), w_out[layer].astype(BF16))
        h2 = _ffn(h2, norm_ffn[layer], w_ffn_in[layer].astype(BF16), w_ffn_out[layer].astype(BF16))

    return _final_norm(h2.reshape(b, l_pad, d), norm_final, s)
```

```python
import functools

import jax
import jax.numpy as jnp
from jax import lax
from jax.experimental import pallas as pl
from jax.experimental.pallas import tpu as pltpu

D_MODEL = 1024
HEAD_DIM = 64
N_HEADS = 8
C_MIX = N_HEADS * HEAD_DIM
N_META = 16
SB_BLOCK = 128
W_LORA, A_LORA, V_LORA, G_LORA = 64, 64, 32, 160
RW_COLS = 3 * C_MIX + W_LORA + A_LORA + G_LORA
FFN_HIDDEN = 2816
RMS_EPS = 1e-6
GN_EPS = 64e-5
RW_CHUNK = 64
SB_UNDERFLOW = 104.0
SB_EAGER_BLOCKS = 2

N_QKV = 3 * C_MIX
COL_RW = 0
COL_LORA = 1536
COL_GATES = 2048
N_REST = 4096
LORA_USED = W_LORA + A_LORA + G_LORA + V_LORA

V7X_VMEM_LIMIT = 56 * 1024 * 1024

F32 = jnp.float32
BF16 = jnp.bfloat16


def _pick(n, cands):
    for c in cands:
        if n % c == 0:
            return c
    raise ValueError(f"no tile for {n} in {cands}")


def _params(*sem):
    return pltpu.CompilerParams(dimension_semantics=sem, vmem_limit_bytes=V7X_VMEM_LIMIT)


def _mm(a, b):
    return jnp.dot(a.astype(BF16), b.astype(BF16), preferred_element_type=F32)


def _mm_nt(a, b):
    return lax.dot_general(a.astype(BF16), b.astype(BF16), (((1,), (1,)), ((), ())),
                           preferred_element_type=F32)


def _mm_tn(a, b):
    return lax.dot_general(a.astype(BF16), b.astype(BF16), (((0,), (0,)), ((), ())),
                           preferred_element_type=F32)


def _each(fn, *lists):
    return [fn(*xs) for xs in zip(*lists)]


def _split_dot(x, w_bf16):
    hi = x.astype(BF16)
    lo = (x - hi.astype(F32)).astype(BF16)
    return (jnp.dot(hi, w_bf16, preferred_element_type=F32)
            + jnp.dot(lo, w_bf16, preferred_element_type=F32))


def _softplus(u):
    return jnp.maximum(u, 0.0) + jnp.log(1.0 + jnp.exp(-jnp.abs(u)))


def _sigmoid(u):
    return 1.0 / (1.0 + jnp.exp(-u))


def _rms_norm(x, gain):
    ms = jnp.mean(x * x, axis=-1, keepdims=True)
    return x * lax.rsqrt(ms + RMS_EPS) * gain


def _norm_proj_kernel(h_ref, g_ref, w_ref, qkv_ref, rest_ref):
    xn = _rms_norm(h_ref[...], g_ref[...]).astype(BF16)
    qkv_ref[...] = jnp.dot(xn, w_ref[:, :N_QKV], preferred_element_type=F32).astype(BF16)
    rest_ref[...] = jnp.dot(xn, w_ref[:, N_QKV:], preferred_element_type=F32)


def _norm_proj(h2, gain, w_cat):
    m = h2.shape[0]
    n = w_cat.shape[1]
    tm = _pick(m, (384, 256, 128))
    return pl.pallas_call(
        _norm_proj_kernel,
        grid=(m // tm,),
        in_specs=[pl.BlockSpec((tm, D_MODEL), lambda i: (i, 0)),
                  pl.BlockSpec((1, D_MODEL), lambda i: (0, 0)),
                  pl.BlockSpec((D_MODEL, n), lambda i: (0, 0))],
        out_specs=[pl.BlockSpec((tm, N_QKV), lambda i: (i, 0)),
                   pl.BlockSpec((tm, n - N_QKV), lambda i: (i, 0))],
        out_shape=[jax.ShapeDtypeStruct((m, N_QKV), BF16), jax.ShapeDtypeStruct((m, n - N_QKV), F32)],
        compiler_params=_params("arbitrary"),
        name="norm_proj",
    )(h2, gain.reshape(1, D_MODEL), w_cat)


def _sb_kernel(q_ref, k_ref, v_ref, lm_ref, o_ref, *, tq):
    qi = pl.program_id(1)
    n_pairs = N_HEADS // 2
    pw = 2 * HEAD_DIM
    lane = lax.broadcasted_iota(jnp.int32, (1, pw), 1)
    head_a = lane < HEAD_DIM
    zero_bf = jnp.zeros((), BF16)
    split_rows = lambda x: jnp.concatenate([jnp.where(head_a, x, zero_bf), jnp.where(head_a, zero_bf, x)], axis=0)
    q = q_ref[0] * jnp.asarray(HEAD_DIM ** -0.5, BF16)
    q2 = [split_rows(q[:, p * pw:(p + 1) * pw]) for p in range(n_pairs)]
    row = lax.broadcasted_iota(jnp.int32, (2 * tq, tq), 0)
    col = lax.broadcasted_iota(jnp.int32, (2 * tq, tq), 1)
    causal2 = col < jnp.where(row >= tq, row - tq, row)
    lm = lm_ref[...]

    def visit_many(blocks, r_run, acc):
        chains = [(bi, p) for bi in range(len(blocks)) for p in range(n_pairs)]
        starts = [pl.multiple_of(blk[0] * tq, tq) for blk in blocks]
        ks = [k_ref[0, pl.ds(st, tq), :] for st in starts]
        vs = [v_ref[0, pl.ds(st, tq), :] for st in starts]
        vs = [v if blk[2] is None else jnp.where(blk[2], v, zero_bf) for v, blk in zip(vs, blocks)]
        z = [lax.dot_general(q2[p], ks[bi][:, p * pw:(p + 1) * pw], (((1,), (1,)), ((), ())),
                             preferred_element_type=F32) for bi, p in chains]
        sp = [_softplus(zi) for zi in z]
        sp = [jnp.where(causal2, s, 0.0) if blocks[bi][1] else s for s, (bi, p) in zip(sp, chains)]
        hi = [s.astype(BF16) for s in sp]
        lo = [(s - h.astype(F32)).astype(BF16) for s, h in zip(sp, hi)]
        wm = [jnp.dot(jnp.concatenate([h, l_], axis=1), lm, preferred_element_type=F32) for h, l_ in zip(hi, lo)]
        r_run, acc = list(r_run), list(acc)
        for i, (bi, p) in enumerate(chains):
            a = jnp.exp(z[i] - sp[i] - wm[i][:, :tq] - r_run[p])
            if blocks[bi][1]:
                a = jnp.where(causal2, a, 0.0)
            a = a.astype(BF16)
            v2 = split_rows(vs[bi][:, p * pw:(p + 1) * pw])
            acc[p] = acc[p] + jnp.dot(jnp.concatenate([a[:tq], a[tq:]], axis=1), v2, preferred_element_type=F32)
            r_run[p] = r_run[p] + wm[i][:, tq:]
        return r_run, acc

    eager = [(qi, True, None)] + [(jnp.maximum(qi - back, 0), False, qi >= back)
                                  for back in range(1, SB_EAGER_BLOCKS + 1)]
    r_run, acc = visit_many(eager, [jnp.zeros((2 * tq, tq), F32)] * n_pairs, [jnp.zeros((tq, pw), F32)] * n_pairs)

    def more(c):
        j, rr, _ = c
        nearest = functools.reduce(jnp.minimum, [jnp.min(r[:, :1]) for r in rr])
        return jnp.logical_and(j >= 0, nearest < SB_UNDERFLOW)

    def far(c):
        j, rr, ac = c
        rr, ac = visit_many([(j, False, None)], rr, ac)
        return j - 1, rr, ac

    _, _, acc = lax.while_loop(more, far, (qi - 1 - SB_EAGER_BLOCKS, r_run, acc))
    o_ref[0] = jnp.concatenate(acc, axis=1).astype(o_ref.dtype)


def _later_and_ones(tq):
    s_from = jnp.arange(2 * tq)[:, None] % tq
    s_to = jnp.arange(2 * tq)[None, :]
    return jnp.where(s_to < tq, s_from > s_to, True).astype(BF16)


def _sb_attention(qkv3):
    b, l, _ = qkv3.shape
    tq = SB_BLOCK
    return pl.pallas_call(
        functools.partial(_sb_kernel, tq=tq),
        grid=(b, l // tq),
        in_specs=[pl.BlockSpec((1, tq, C_MIX), lambda bi, qi: (bi, qi, 0)),
                  pl.BlockSpec((1, l, C_MIX), lambda bi, qi: (bi, 0, 1)),
                  pl.BlockSpec((1, l, C_MIX), lambda bi, qi: (bi, 0, 2)),
                  pl.BlockSpec((2 * tq, 2 * tq), lambda bi, qi: (0, 0))],
        out_specs=pl.BlockSpec((1, tq, C_MIX), lambda bi, qi: (bi, qi, 0)),
        out_shape=jax.ShapeDtypeStruct((b, l, C_MIX), BF16),
        compiler_params=_params("arbitrary", "arbitrary"),
        name="sb_attention",
    )(qkv3, qkv3, qkv3, _later_and_ones(tq))


def _unit_lower_inverse(a_strict, rowi, coli):
    eye = (rowi == coli).astype(F32)
    same = lambda sh: (rowi >> sh) == (coli >> sh)
    size = a_strict[0].shape[0]
    levels = [same(sh) for sh in range(3, size.bit_length() - 1)] + [None]
    m8 = levels[0]
    n1 = [-jnp.where(m8, a, 0.0) for a in a_strict]
    n2 = _each(_mm, n1, n1)
    n4 = _each(_mm, n2, n2)
    t = [eye + n for n in n1]
    t = _each(lambda ti, ni: ti + _mm(ti, ni), t, n2)
    t = _each(lambda ti, ni: ti + _mm(ti, ni), t, n4)
    for inner, outer in zip(levels[:-1], levels[1:]):
        off = jnp.logical_not(inner) if outer is None else jnp.logical_and(outer, jnp.logical_not(inner))
        ta = _each(lambda ti, a: _mm(ti, jnp.where(off, a, 0.0)), t, a_strict)
        t = _each(lambda ti, tai: ti - _mm(tai, ti), t, ta)
    return t


def _rw_mix_kernel(*refs, has_vres):
    if has_vres:
        (main_ref, lora_ref, vfirst_ref, mu_main_ref, mu_lora_ref, w0_ref, a0_ref, kk_ref, ka_ref, vres0_ref,
         wl_ref, bd_ref, lnw_ref, lnb_ref, rk_ref, y_ref, s_ref, pm_ref, pl_ref) = refs
        vout_ref = None
    else:
        (main_ref, lora_ref, mu_main_ref, mu_lora_ref, w0_ref, a0_ref, kk_ref, ka_ref,
         wl_ref, bd_ref, lnw_ref, lnb_ref, rk_ref, y_ref, vout_ref, s_ref, pm_ref, pl_ref) = refs
    c = RW_CHUNK
    n_batch = main_ref.shape[0]
    n_chains = n_batch * N_HEADS

    @pl.when(pl.program_id(0) == 0)
    def _():
        s_ref[...] = jnp.zeros_like(s_ref)
        pm_ref[...] = jnp.zeros_like(pm_ref)
        pl_ref[...] = jnp.zeros_like(pl_ref)

    first_row = lax.broadcasted_iota(jnp.int32, (c, 1), 0) == 0

    def shifted(x, prev8, mu):
        prev = jnp.where(first_row, prev8[7:8, :], pltpu.roll(x, shift=1, axis=0))
        return x + (prev - x) * mu

    mains = [main_ref[bi] for bi in range(n_batch)]
    loras = [lora_ref[bi] for bi in range(n_batch)]
    xs = jnp.concatenate([shifted(x, pm_ref[bi], mu_main_ref[...]) for bi, x in enumerate(mains)], axis=0)
    lo = jnp.concatenate([shifted(x, pl_ref[bi], mu_lora_ref[...]) for bi, x in enumerate(loras)], axis=0)
    for bi in range(n_batch):
        pm_ref[bi] = mains[bi][c - 8:c, :]
        pl_ref[bi] = loras[bi][c - 8:c, :]

    r = xs[:, 0:C_MIX]
    kr = xs[:, C_MIX:2 * C_MIX]
    vr = xs[:, 2 * C_MIX:3 * C_MIX]
    lane = lax.broadcasted_iota(jnp.int32, (1, C_MIX), 1)
    act = jnp.where(lane < W_LORA, jnp.tanh(lo),
                    jnp.where(jnp.logical_and(lane >= W_LORA + A_LORA, lane < W_LORA + A_LORA + G_LORA),
                              _sigmoid(lo), lo))
    pre = jnp.dot(act.astype(BF16), wl_ref[...], preferred_element_type=F32)
    w_log = -_softplus(-(w0_ref[...] + pre[:, 0:C_MIX])) - 0.5
    ld = -jnp.exp(w_log)
    a = _sigmoid(a0_ref[...] + pre[:, C_MIX:2 * C_MIX])
    g = pre[:, 2 * C_MIX:3 * C_MIX]
    if has_vres:
        v_first = jnp.concatenate([vfirst_ref[bi] for bi in range(n_batch)], axis=0)
        vr = vr + (v_first - vr) * _sigmoid(vres0_ref[...] + pre[:, 3 * C_MIX:4 * C_MIX])
    else:
        for bi in range(n_batch):
            vout_ref[bi] = vr[bi * c:(bi + 1) * c]
    bd = bd_ref[...]
    kk = kr * kk_ref[...]
    kk = kk / jnp.maximum(jnp.sqrt(_split_dot(kk * kk, bd)), 1e-12)
    k = kr * (1.0 + (a - 1.0) * ka_ref[...])
    bvec = kk * a

    rowi = lax.broadcasted_iota(jnp.int32, (c, c), 0)
    coli = lax.broadcasted_iota(jnp.int32, (c, c), 1)
    lower_incl = coli <= rowi
    lower_strict = coli < rowi
    tri = lower_incl.astype(F32)
    rows = lambda x, bi: x[bi * c:(bi + 1) * c]
    kt_all, bt_all, kn_all, rt_all, bh_all, kh_all, g_end = [], [], [], [], [], [], []
    for bi in range(n_batch):
        ld_b = rows(ld, bi)
        cum = jnp.dot(tri, ld_b, precision=lax.Precision.HIGHEST, preferred_element_type=F32)
        cum_end = cum[c - 1:c, :]
        e_neg = jnp.exp(-cum)
        to_end = jnp.exp(cum_end - cum)
        kt_all.append(rows(kk, bi) * jnp.exp(cum - ld_b))
        bt_all.append(rows(bvec, bi) * e_neg)
        kn_all.append(rows(k, bi) * e_neg)
        rt_all.append(rows(r, bi) * jnp.exp(cum))
        bh_all.append(rows(bvec, bi) * to_end)
        kh_all.append(rows(k, bi) * to_end)
        g_end.append(jnp.exp(cum_end))
    v_all = [rows(vr, bi) for bi in range(n_batch)]

    heads = lambda xb: [x[:, h * HEAD_DIM:(h + 1) * HEAD_DIM] for x in xb for h in range(N_HEADS)]
    kt, bt, kn, rt, bh, kh, v, g_h = (heads(x) for x in (kt_all, bt_all, kn_all, rt_all, bh_all, kh_all, v_all,
                                                        g_end))
    hd = HEAD_DIM
    kt_rt = _each(lambda x, y: jnp.concatenate([x, y], axis=0), kt, rt)
    on_b = _each(_mm_nt, kt_rt, bt)
    on_k = _each(_mm_nt, kt_rt, kn)
    a_b = [jnp.where(lower_strict, x[:c], 0.0) for x in on_b]
    a_rb = [jnp.where(lower_incl, x[c:], 0.0) for x in on_b]
    a_k = [jnp.where(lower_strict, x[:c], 0.0) for x in on_k]
    a_rk = [jnp.where(lower_incl, x[c:], 0.0) for x in on_k]
    av = _each(lambda x, y, vv: _mm(jnp.concatenate([x, y], axis=0), vv), a_k, a_rk, v)
    akv = [x[:c] for x in av]
    arkv = [x[c:] for x in av]
    vkh = _each(_mm_tn, v, kh)
    t_inv = _unit_lower_inverse(a_b, rowi, coli)
    ku = _each(lambda t_, x, y: _mm(t_, jnp.concatenate([x, y], axis=1)), t_inv, kt, akv)
    ry = _each(_mm, a_rb, ku)
    r_hat = _each(lambda x, z: x - z[:, :hd], rt, ry)
    y_bar = _each(lambda x, z: x - z[:, hd:], arkv, ry)
    eh = _each(_mm_tn, ku, bh)
    e_bar = [x[:hd] for x in eh]
    h_add = _each(lambda x, z: x - z[hd:], vkh, eh)
    s = [s_ref[h] for h in range(n_chains)]
    ys = _each(lambda rh, sh, yb: _mm_nt(rh, sh) + yb, r_hat, s, y_bar)
    s_new = _each(lambda sh, gh, eb, ha: sh * gh - _mm(sh, eb) + ha, s, g_h, e_bar, h_add)
    for h in range(n_chains):
        s_ref[h] = s_new[h]

    y = jnp.concatenate([jnp.concatenate(ys[bi * N_HEADS:(bi + 1) * N_HEADS], axis=-1) for bi in range(n_batch)],
                        axis=0)
    inv_n = 1.0 / HEAD_DIM
    mean = _split_dot(y, bd) * inv_n
    d = y - mean
    var = _split_dot(d * d, bd) * inv_n
    yn = d * lax.rsqrt(var + GN_EPS) * lnw_ref[...] + lnb_ref[...]
    bonus = _split_dot(r * k * rk_ref[...], bd) * vr
    out = (yn + bonus) * g
    for bi in range(n_batch):
        y_ref[bi] = out[bi * c:(bi + 1) * c].astype(y_ref.dtype)


def _rw_mix(rest3, v_first, mu_main, mu_lora, w0, a0, k_k, k_a, vres0, w_lora, bd, ln_w, ln_b, r_k):
    b, l, _ = rest3.shape
    has_vres = v_first is not None
    c = RW_CHUNK
    tok = pl.BlockSpec((b, c, C_MIX), lambda ci: (0, ci, 0))
    row = lambda width: pl.BlockSpec((1, width), lambda ci: (0, 0))
    in_specs = [pl.BlockSpec((b, c, 3 * C_MIX), lambda ci: (0, ci, COL_RW // (3 * C_MIX))),
                pl.BlockSpec((b, c, C_MIX), lambda ci: (0, ci, COL_LORA // C_MIX))]
    args = [rest3, rest3]
    if has_vres:
        in_specs.append(tok)
        args.append(v_first)
    in_specs += [row(3 * C_MIX)] + [row(C_MIX)] * 5
    args += [mu_main, mu_lora, w0, a0, k_k, k_a]
    if has_vres:
        in_specs.append(row(C_MIX))
        args.append(vres0)
    in_specs += [pl.BlockSpec((C_MIX, 4 * C_MIX), lambda ci: (0, 0)), pl.BlockSpec((C_MIX, C_MIX), lambda ci: (0, 0)),
                 row(C_MIX), row(C_MIX), row(C_MIX)]
    args += [w_lora, bd, ln_w, ln_b, r_k]
    out_shape = [jax.ShapeDtypeStruct((b, l, C_MIX), BF16)]
    out_specs = [tok]
    if not has_vres:
        out_shape.append(jax.ShapeDtypeStruct((b, l, C_MIX), F32))
        out_specs.append(tok)
    res = pl.pallas_call(
        functools.partial(_rw_mix_kernel, has_vres=has_vres),
        grid=(l // c,),
        in_specs=in_specs,
        out_specs=out_specs,
        out_shape=out_shape,
        scratch_shapes=[pltpu.VMEM((b * N_HEADS, HEAD_DIM, HEAD_DIM), F32),
                        pltpu.VMEM((b, 8, 3 * C_MIX), F32), pltpu.VMEM((b, 8, C_MIX), F32)],
        compiler_params=_params("arbitrary"),
        name="rw_mix",
    )(*args)
    return (res[0], None) if has_vres else (res[0], res[1])


def _merge_kernel(h_ref, gsb_ref, grw_ref, ysb_ref, yrw_ref, wsb_ref, wrw_ref, wout_ref, o_ref):
    o_sb = jnp.dot(ysb_ref[...], wsb_ref[...], preferred_element_type=F32)
    o_rw = jnp.dot(yrw_ref[...], wrw_ref[...], preferred_element_type=F32)
    merged = _sigmoid(gsb_ref[...]) * o_sb + _sigmoid(grw_ref[...]) * o_rw
    o_ref[...] = h_ref[...] + jnp.dot(merged.astype(BF16), wout_ref[...], preferred_element_type=F32)


def _merge(h2, rest2, y_sb, y_rw, w_sb, w_rw, w_out):
    m = h2.shape[0]
    tm = _pick(m, (512, 256, 128))
    rows = lambda width: pl.BlockSpec((tm, width), lambda i: (i, 0))
    full = lambda shape: pl.BlockSpec(shape, lambda i: (0, 0))
    return pl.pallas_call(
        _merge_kernel,
        grid=(m // tm,),
        in_specs=[rows(D_MODEL),
                  pl.BlockSpec((tm, D_MODEL), lambda i: (i, COL_GATES // D_MODEL)),
                  pl.BlockSpec((tm, D_MODEL), lambda i: (i, COL_GATES // D_MODEL + 1)),
                  rows(C_MIX), rows(C_MIX),
                  full((C_MIX, D_MODEL)), full((C_MIX, D_MODEL)), full((D_MODEL, D_MODEL))],
        out_specs=rows(D_MODEL),
        out_shape=jax.ShapeDtypeStruct((m, D_MODEL), F32),
        compiler_params=_params("arbitrary"),
        name="merge",
    )(h2, rest2, rest2, y_sb, y_rw, w_sb, w_rw, w_out)


def _ffn_kernel(h_ref, g_ref, wg_ref, wu_ref, wo_ref, o_ref):
    x = h_ref[...]
    hn = _rms_norm(x, g_ref[...]).astype(BF16)
    gate = jnp.dot(hn, wg_ref[...], preferred_element_type=F32)
    up = jnp.dot(hn, wu_ref[...], preferred_element_type=F32)
    act = gate * _sigmoid(gate) * up
    o_ref[...] = x + jnp.dot(act.astype(BF16), wo_ref[...], preferred_element_type=F32)


def _ffn(h2, gain, w_in, w_out):
    m = h2.shape[0]
    tm = _pick(m, (384, 256, 128))
    return pl.pallas_call(
        _ffn_kernel,
        grid=(m // tm,),
        in_specs=[pl.BlockSpec((tm, D_MODEL), lambda i: (i, 0)),
                  pl.BlockSpec((1, D_MODEL), lambda i: (0, 0)),
                  pl.BlockSpec((D_MODEL, FFN_HIDDEN), lambda i: (0, 0)),
                  pl.BlockSpec((D_MODEL, FFN_HIDDEN), lambda i: (0, 1)),
                  pl.BlockSpec((FFN_HIDDEN, D_MODEL), lambda i: (0, 0))],
        out_specs=pl.BlockSpec((tm, D_MODEL), lambda i: (i, 0)),
        out_shape=jax.ShapeDtypeStruct((m, D_MODEL), F32),
        compiler_params=_params("arbitrary"),
        name="ffn",
    )(h2, gain.reshape(1, D_MODEL), w_in, w_in, w_out)


def _final_norm_kernel(h_ref, g_ref, o_ref):
    o_ref[...] = _rms_norm(h_ref[...], g_ref[...])


def _final_norm(h3, gain, s):
    b = h3.shape[0]
    tm = _pick(s, (512, 256, 128, 16))
    return pl.pallas_call(
        _final_norm_kernel,
        grid=(b, s // tm),
        in_specs=[pl.BlockSpec((pl.Element(1), pl.Element(tm), pl.Element(D_MODEL)),
                               lambda bi, i: (bi, pl.multiple_of(i * tm + N_META, N_META), 0)),
                  pl.BlockSpec((1, 1, D_MODEL), lambda bi, i: (0, 0, 0))],
        out_specs=pl.BlockSpec((1, tm, D_MODEL), lambda bi, i: (bi, i, 0)),
        out_shape=jax.ShapeDtypeStruct((b, s, D_MODEL), F32),
        compiler_params=_params("arbitrary", "arbitrary"),
        name="final_norm",
    )(h3, gain.reshape(1, 1, D_MODEL))


def _head_block_diag():
    idx = jnp.arange(C_MIX) // HEAD_DIM
    return (idx[:, None] == idx[None, :]).astype(BF16)


def _lora_weight(w_up, a_up, g_up, vres_up):
    w = jnp.zeros((C_MIX, 4 * C_MIX), F32)
    o = 0
    for seg, (mat, width) in enumerate(((w_up, W_LORA), (a_up, A_LORA), (g_up, G_LORA), (vres_up, V_LORA))):
        if mat is not None:
            w = w.at[o:o + width, seg * C_MIX:(seg + 1) * C_MIX].set(mat)
        o += width
    return w.astype(BF16)


def kernel(x, meta_tokens, norm_mix, norm_ffn, norm_final, w_in, mu_rw, w0, w_up, a0, a_up, g_up, k_k, k_a, r_k, ln_x_w, ln_x_b, vres_down, vres_mu, vres_up, vres0, w_sb_out, w_rw_out, w_out, w_ffn_in, w_ffn_out):
    b, s, d = x.shape
    depth = w_in.shape[0]
    l_real = N_META + s
    l_pad = -(-l_real // SB_BLOCK) * SB_BLOCK
    meta = jnp.broadcast_to(meta_tokens.astype(x.dtype)[None], (b, N_META, d))
    h = jnp.concatenate([meta, x, jnp.zeros((b, l_pad - l_real, d), x.dtype)], axis=1)
    h2 = h.reshape(b * l_pad, d)
    bd = _head_block_diag()
    n_in = 3 * C_MIX + RW_COLS
    row = lambda vec: vec.reshape(1, -1)

    v_first = None
    for layer in range(depth):
        wl = w_in[layer]
        vdown = vres_down[layer - 1] if layer > 0 else jnp.zeros((d, V_LORA), F32)
        w_cat = jnp.concatenate([wl[:, :n_in], vdown, jnp.zeros((d, C_MIX - LORA_USED), F32), wl[:, n_in:]],
                                axis=1).astype(BF16)
        vmu = vres_mu[layer - 1] if layer > 0 else jnp.zeros((V_LORA,), F32)
        mu_main = row(mu_rw[layer, :3 * C_MIX])
        mu_lora = row(jnp.concatenate([mu_rw[layer, 3 * C_MIX:], vmu, jnp.zeros((C_MIX - LORA_USED,), F32)]))
        w_lora = _lora_weight(w_up[layer], a_up[layer], g_up[layer], vres_up[layer - 1] if layer > 0 else None)

        qkv2, rest2 = _norm_proj(h2, norm_mix[layer], w_cat)
        y_sb = _sb_attention(qkv2.reshape(b, l_pad, N_QKV))
        y_rw, v_out = _rw_mix(
            rest2.reshape(b, l_pad, N_REST), v_first, mu_main, mu_lora, row(w0[layer]), row(a0[layer]),
            row(k_k[layer]), row(k_a[layer]), row(vres0[layer - 1]) if layer > 0 else None, w_lora, bd,
            row(ln_x_w[layer]), row(ln_x_b[layer]), row(r_k[layer].reshape(-1)))
        if layer == 0:
            v_first = v_out
        flat = lambda t: t.reshape(b * l_pad, C_MIX)
        h2 = _merge(h2, rest2, flat(y_sb), flat(y_rw),
                    w_sb_out[layer].astype(BF16), w_rw_out[layer].astype(BF16), w_out[layer].astype(BF16))
        h2 = _ffn(h2, norm_ffn[layer], w_ffn_in[layer].astype(BF16), w_ffn_out[layer].astype(BF16))

    return _final_norm(h2.reshape(b, l_pad, d), norm_final, s)
```

```python
import functools

import jax
import jax.numpy as jnp
from jax import lax
from jax.experimental import pallas as pl
from jax.experimental.pallas import tpu as pltpu

D_MODEL = 1024
HEAD_DIM = 64
N_HEADS = 8
C_MIX = N_HEADS * HEAD_DIM
N_META = 16
SB_BLOCK = 128
W_LORA, A_LORA, V_LORA, G_LORA = 64, 64, 32, 160
RW_COLS = 3 * C_MIX + W_LORA + A_LORA + G_LORA
FFN_HIDDEN = 2816
RMS_EPS = 1e-6
GN_EPS = 64e-5
RW_CHUNK = 64
SB_UNDERFLOW = 104.0
SB_EAGER_BLOCKS = 2

N_QKV = 3 * C_MIX
COL_RW = 0
COL_LORA = 1536
COL_GATES = 2048
N_REST = 4096
LORA_USED = W_LORA + A_LORA + G_LORA + V_LORA

V7X_VMEM_LIMIT = 56 * 1024 * 1024

F32 = jnp.float32
BF16 = jnp.bfloat16


def _pick(n, cands):
    for c in cands:
        if n % c == 0:
            return c
    raise ValueError(f"no tile for {n} in {cands}")


def _params(*sem):
    return pltpu.CompilerParams(dimension_semantics=sem, vmem_limit_bytes=V7X_VMEM_LIMIT)


def _mm(a, b):
    return jnp.dot(a.astype(BF16), b.astype(BF16), preferred_element_type=F32)


def _mm_nt(a, b):
    return lax.dot_general(a.astype(BF16), b.astype(BF16), (((1,), (1,)), ((), ())),
                           preferred_element_type=F32)


def _mm_tn(a, b):
    return lax.dot_general(a.astype(BF16), b.astype(BF16), (((0,), (0,)), ((), ())),
                           preferred_element_type=F32)


def _each(fn, *lists):
    return [fn(*xs) for xs in zip(*lists)]


def _split_dot(x, w_bf16):
    hi = x.astype(BF16)
    lo = (x - hi.astype(F32)).astype(BF16)
    return (jnp.dot(hi, w_bf16, preferred_element_type=F32)
            + jnp.dot(lo, w_bf16, preferred_element_type=F32))


def _softplus(u):
    return jnp.maximum(u, 0.0) + jnp.log(1.0 + jnp.exp(-jnp.abs(u)))


def _sigmoid(u):
    return 1.0 / (1.0 + jnp.exp(-u))


def _rms_norm(x, gain):
    ms = jnp.mean(x * x, axis=-1, keepdims=True)
    return x * lax.rsqrt(ms + RMS_EPS) * gain


def _norm_proj_kernel(h_ref, g_ref, w_ref, qkv_ref, rest_ref):
    xn = _rms_norm(h_ref[...], g_ref[...]).astype(BF16)
    qkv_ref[...] = jnp.dot(xn, w_ref[:, :N_QKV], preferred_element_type=F32).astype(BF16)
    rest_ref[...] = jnp.dot(xn, w_ref[:, N_QKV:], preferred_element_type=F32)


def _norm_proj(h2, gain, w_cat):
    m = h2.shape[0]
    n = w_cat.shape[1]
    tm = _pick(m, (384, 256, 128))
    return pl.pallas_call(
        _norm_proj_kernel,
        grid=(m // tm,),
        in_specs=[pl.BlockSpec((tm, D_MODEL), lambda i: (i, 0)),
                  pl.BlockSpec((1, D_MODEL), lambda i: (0, 0)),
                  pl.BlockSpec((D_MODEL, n), lambda i: (0, 0))],
        out_specs=[pl.BlockSpec((tm, N_QKV), lambda i: (i, 0)),
                   pl.BlockSpec((tm, n - N_QKV), lambda i: (i, 0))],
        out_shape=[jax.ShapeDtypeStruct((m, N_QKV), BF16), jax.ShapeDtypeStruct((m, n - N_QKV), F32)],
        compiler_params=_params("arbitrary"),
        name="norm_proj",
    )(h2, gain.reshape(1, D_MODEL), w_cat)


def _sb_unit(q_ref, k_ref, v_ref, lm, o_ref, qi, tq):
    n_pairs = N_HEADS // 2
    pw = 2 * HEAD_DIM
    lane = lax.broadcasted_iota(jnp.int32, (1, pw), 1)
    head_a = lane < HEAD_DIM
    zero_bf = jnp.zeros((), BF16)
    split_rows = lambda x: jnp.concatenate([jnp.where(head_a, x, zero_bf), jnp.where(head_a, zero_bf, x)], axis=0)
    q = q_ref[0] * jnp.asarray(HEAD_DIM ** -0.5, BF16)
    q2 = [split_rows(q[:, p * pw:(p + 1) * pw]) for p in range(n_pairs)]
    row = lax.broadcasted_iota(jnp.int32, (2 * tq, tq), 0)
    col = lax.broadcasted_iota(jnp.int32, (2 * tq, tq), 1)
    causal2 = col < jnp.where(row >= tq, row - tq, row)
    state = {}

    def visit_steps(blocks, r_run, acc):
        chains = [(bi, p) for bi in range(len(blocks)) for p in range(n_pairs)]
        starts = [pl.multiple_of(blk[0] * tq, tq) for blk in blocks]
        ks = [k_ref[0, pl.ds(st, tq), :] for st in starts]
        vs = [v_ref[0, pl.ds(st, tq), :] for st in starts]
        vs = [v if blk[2] is None else jnp.where(blk[2], v, zero_bf) for v, blk in zip(vs, blocks)]
        z = [lax.dot_general(q2[p], ks[bi][:, p * pw:(p + 1) * pw], (((1,), (1,)), ((), ())),
                             preferred_element_type=F32) for bi, p in chains]
        yield
        sp = [_softplus(zi) for zi in z]
        sp = [jnp.where(causal2, s, 0.0) if blocks[bi][1] else s for s, (bi, p) in zip(sp, chains)]
        hi = [s.astype(BF16) for s in sp]
        lo = [(s - h.astype(F32)).astype(BF16) for s, h in zip(sp, hi)]
        wm = [jnp.dot(jnp.concatenate([h, l_], axis=1), lm, preferred_element_type=F32) for h, l_ in zip(hi, lo)]
        yield
        r_run, acc = list(r_run), list(acc)
        for i, (bi, p) in enumerate(chains):
            a = jnp.exp(z[i] - sp[i] - wm[i][:, :tq] - r_run[p])
            if blocks[bi][1]:
                a = jnp.where(causal2, a, 0.0)
            a = a.astype(BF16)
            v2 = split_rows(vs[bi][:, p * pw:(p + 1) * pw])
            acc[p] = acc[p] + jnp.dot(jnp.concatenate([a[:tq], a[tq:]], axis=1), v2, preferred_element_type=F32)
            r_run[p] = r_run[p] + wm[i][:, tq:]
        state["r_run"], state["acc"] = r_run, acc

    eager_blocks = [(qi, True, None)] + [(jnp.maximum(qi - back, 0), False, qi >= back)
                                         for back in range(1, SB_EAGER_BLOCKS + 1)]
    eager = visit_steps(eager_blocks, [jnp.zeros((2 * tq, tq), F32)] * n_pairs,
                        [jnp.zeros((tq, pw), F32)] * n_pairs)

    def finish():
        def more(c):
            j, rr, _ = c
            nearest = functools.reduce(jnp.minimum, [jnp.min(r[:, :1]) for r in rr])
            return jnp.logical_and(j >= 0, nearest < SB_UNDERFLOW)

        def far(c):
            j, rr, ac = c
            for _ in visit_steps([(j, False, None)], rr, ac):
                pass
            return j - 1, state["r_run"], state["acc"]

        _, _, acc = lax.while_loop(more, far, (qi - 1 - SB_EAGER_BLOCKS, state["r_run"], state["acc"]))
        o_ref[0] = jnp.concatenate(acc, axis=1).astype(o_ref.dtype)

    return eager, finish


def _later_and_ones(tq):
    s_from = jnp.arange(2 * tq)[:, None] % tq
    s_to = jnp.arange(2 * tq)[None, :]
    return jnp.where(s_to < tq, s_from > s_to, True).astype(BF16)


def _unit_lower_inverse(a_strict, rowi, coli):
    eye = (rowi == coli).astype(F32)
    same = lambda sh: (rowi >> sh) == (coli >> sh)
    size = a_strict[0].shape[0]
    levels = [same(sh) for sh in range(3, size.bit_length() - 1)] + [None]
    m8 = levels[0]
    n1 = [-jnp.where(m8, a, 0.0) for a in a_strict]
    n2 = _each(_mm, n1, n1)
    n4 = _each(_mm, n2, n2)
    t = [eye + n for n in n1]
    t = _each(lambda ti, ni: ti + _mm(ti, ni), t, n2)
    t = _each(lambda ti, ni: ti + _mm(ti, ni), t, n4)
    for inner, outer in zip(levels[:-1], levels[1:]):
        off = jnp.logical_not(inner) if outer is None else jnp.logical_and(outer, jnp.logical_not(inner))
        ta = _each(lambda ti, a: _mm(ti, jnp.where(off, a, 0.0)), t, a_strict)
        t = _each(lambda ti, tai: ti - _mm(tai, ti), t, ta)
    return t


def _mixers_kernel(*refs, has_vres, n_units, n_qblocks):
    refs = list(refs)
    n_rw_in = 15 if has_vres else 13
    rw_in, refs = refs[:n_rw_in], refs[n_rw_in:]
    sb_in, refs = refs[:3 * n_units + 1], refs[3 * n_units + 1:]
    if has_vres:
        (main_ref, lora_ref, vfirst_ref, mu_main_ref, mu_lora_ref, w0_ref, a0_ref, kk_ref, ka_ref, vres0_ref,
         wl_ref, bd_ref, lnw_ref, lnb_ref, rk_ref) = rw_in
        y_ref, vout_ref, refs = refs[0], None, refs[1:]
    else:
        (main_ref, lora_ref, mu_main_ref, mu_lora_ref, w0_ref, a0_ref, kk_ref, ka_ref,
         wl_ref, bd_ref, lnw_ref, lnb_ref, rk_ref) = rw_in
        y_ref, vout_ref, refs = refs[0], refs[1], refs[2:]
    ysb_refs, (s_ref, pm_ref, pl_ref) = refs[:n_units], refs[n_units:]
    c = RW_CHUNK
    n_batch = main_ref.shape[0]
    n_chains = n_batch * N_HEADS

    @pl.when(pl.program_id(0) == 0)
    def _():
        s_ref[...] = jnp.zeros_like(s_ref)
        pm_ref[...] = jnp.zeros_like(pm_ref)
        pl_ref[...] = jnp.zeros_like(pl_ref)

    lm = sb_in[-1][...]
    q_block = pl.program_id(0) % n_qblocks
    units = [_sb_unit(sb_in[3 * j], sb_in[3 * j + 1], sb_in[3 * j + 2], lm, ysb_refs[j], q_block, SB_BLOCK)
             for j in range(n_units)]

    def attention_step():
        for eager, _ in units:
            next(eager, None)

    first_row = lax.broadcasted_iota(jnp.int32, (c, 1), 0) == 0

    def shifted(x, prev8, mu):
        prev = jnp.where(first_row, prev8[7:8, :], pltpu.roll(x, shift=1, axis=0))
        return x + (prev - x) * mu

    mains = [main_ref[bi] for bi in range(n_batch)]
    loras = [lora_ref[bi] for bi in range(n_batch)]
    xs = jnp.concatenate([shifted(x, pm_ref[bi], mu_main_ref[...]) for bi, x in enumerate(mains)], axis=0)
    lo = jnp.concatenate([shifted(x, pl_ref[bi], mu_lora_ref[...]) for bi, x in enumerate(loras)], axis=0)
    for bi in range(n_batch):
        pm_ref[bi] = mains[bi][c - 8:c, :]
        pl_ref[bi] = loras[bi][c - 8:c, :]

    r = xs[:, 0:C_MIX]
    kr = xs[:, C_MIX:2 * C_MIX]
    vr = xs[:, 2 * C_MIX:3 * C_MIX]
    lane = lax.broadcasted_iota(jnp.int32, (1, C_MIX), 1)
    act = jnp.where(lane < W_LORA, jnp.tanh(lo),
                    jnp.where(jnp.logical_and(lane >= W_LORA + A_LORA, lane < W_LORA + A_LORA + G_LORA),
                              _sigmoid(lo), lo))
    pre = jnp.dot(act.astype(BF16), wl_ref[...], preferred_element_type=F32)
    w_log = -_softplus(-(w0_ref[...] + pre[:, 0:C_MIX])) - 0.5
    ld = -jnp.exp(w_log)
    a = _sigmoid(a0_ref[...] + pre[:, C_MIX:2 * C_MIX])
    g = pre[:, 2 * C_MIX:3 * C_MIX]
    if has_vres:
        v_first = jnp.concatenate([vfirst_ref[bi] for bi in range(n_batch)], axis=0)
        vr = vr + (v_first - vr) * _sigmoid(vres0_ref[...] + pre[:, 3 * C_MIX:4 * C_MIX])
    else:
        for bi in range(n_batch):
            vout_ref[bi] = vr[bi * c:(bi + 1) * c]
    bd = bd_ref[...]
    kk = kr * kk_ref[...]
    kk = kk / jnp.maximum(jnp.sqrt(_split_dot(kk * kk, bd)), 1e-12)
    k = kr * (1.0 + (a - 1.0) * ka_ref[...])
    bvec = kk * a

    rowi = lax.broadcasted_iota(jnp.int32, (c, c), 0)
    coli = lax.broadcasted_iota(jnp.int32, (c, c), 1)
    lower_incl = coli <= rowi
    lower_strict = coli < rowi
    tri = lower_incl.astype(F32)
    rows = lambda x, bi: x[bi * c:(bi + 1) * c]
    kt_all, bt_all, kn_all, rt_all, bh_all, kh_all, g_end = [], [], [], [], [], [], []
    for bi in range(n_batch):
        ld_b = rows(ld, bi)
        cum = jnp.dot(tri, ld_b, precision=lax.Precision.HIGHEST, preferred_element_type=F32)
        cum_end = cum[c - 1:c, :]
        e_neg = jnp.exp(-cum)
        to_end = jnp.exp(cum_end - cum)
        kt_all.append(rows(kk, bi) * jnp.exp(cum - ld_b))
        bt_all.append(rows(bvec, bi) * e_neg)
        kn_all.append(rows(k, bi) * e_neg)
        rt_all.append(rows(r, bi) * jnp.exp(cum))
        bh_all.append(rows(bvec, bi) * to_end)
        kh_all.append(rows(k, bi) * to_end)
        g_end.append(jnp.exp(cum_end))
    v_all = [rows(vr, bi) for bi in range(n_batch)]

    heads = lambda xb: [x[:, h * HEAD_DIM:(h + 1) * HEAD_DIM] for x in xb for h in range(N_HEADS)]
    kt, bt, kn, rt, bh, kh, v, g_h = (heads(x) for x in (kt_all, bt_all, kn_all, rt_all, bh_all, kh_all, v_all,
                                                        g_end))
    hd = HEAD_DIM
    kt_rt = _each(lambda x, y: jnp.concatenate([x, y], axis=0), kt, rt)
    on_b = _each(_mm_nt, kt_rt, bt)
    on_k = _each(_mm_nt, kt_rt, kn)
    a_b = [jnp.where(lower_strict, x[:c], 0.0) for x in on_b]
    a_rb = [jnp.where(lower_incl, x[c:], 0.0) for x in on_b]
    a_k = [jnp.where(lower_strict, x[:c], 0.0) for x in on_k]
    a_rk = [jnp.where(lower_incl, x[c:], 0.0) for x in on_k]
    av = _each(lambda x, y, vv: _mm(jnp.concatenate([x, y], axis=0), vv), a_k, a_rk, v)
    akv = [x[:c] for x in av]
    arkv = [x[c:] for x in av]
    vkh = _each(_mm_tn, v, kh)
    attention_step()
    t_inv = _unit_lower_inverse(a_b, rowi, coli)
    attention_step()
    ku = _each(lambda t_, x, y: _mm(t_, jnp.concatenate([x, y], axis=1)), t_inv, kt, akv)
    ry = _each(_mm, a_rb, ku)
    r_hat = _each(lambda x, z: x - z[:, :hd], rt, ry)
    y_bar = _each(lambda x, z: x - z[:, hd:], arkv, ry)
    eh = _each(_mm_tn, ku, bh)
    attention_step()
    e_bar = [x[:hd] for x in eh]
    h_add = _each(lambda x, z: x - z[hd:], vkh, eh)
    s = [s_ref[h] for h in range(n_chains)]
    ys = _each(lambda rh, sh, yb: _mm_nt(rh, sh) + yb, r_hat, s, y_bar)
    s_new = _each(lambda sh, gh, eb, ha: sh * gh - _mm(sh, eb) + ha, s, g_h, e_bar, h_add)
    for h in range(n_chains):
        s_ref[h] = s_new[h]

    y = jnp.concatenate([jnp.concatenate(ys[bi * N_HEADS:(bi + 1) * N_HEADS], axis=-1) for bi in range(n_batch)],
                        axis=0)
    inv_n = 1.0 / HEAD_DIM
    mean = _split_dot(y, bd) * inv_n
    d = y - mean
    var = _split_dot(d * d, bd) * inv_n
    yn = d * lax.rsqrt(var + GN_EPS) * lnw_ref[...] + lnb_ref[...]
    bonus = _split_dot(r * k * rk_ref[...], bd) * vr
    out = (yn + bonus) * g
    for bi in range(n_batch):
        y_ref[bi] = out[bi * c:(bi + 1) * c].astype(y_ref.dtype)

    for _, finish in units:
        finish()


def _mixers(qkv3, rest3, v_first, mu_main, mu_lora, w0, a0, k_k, k_a, vres0, w_lora, bd, ln_w, ln_b, r_k):
    b, l, _ = rest3.shape
    has_vres = v_first is not None
    c = RW_CHUNK
    tq = SB_BLOCK
    n_q = l // tq
    n_units = b // 2
    assert b % 2 == 0 and l // c == 2 * n_q, "one query block of two batch elements per unit and chunk pair"
    tok = pl.BlockSpec((b, c, C_MIX), lambda ci: (0, ci, 0))
    row = lambda width: pl.BlockSpec((1, width), lambda ci: (0, 0))
    in_specs = [pl.BlockSpec((b, c, 3 * C_MIX), lambda ci: (0, ci, COL_RW // (3 * C_MIX))),
                pl.BlockSpec((b, c, C_MIX), lambda ci: (0, ci, COL_LORA // C_MIX))]
    args = [rest3, rest3]
    if has_vres:
        in_specs.append(tok)
        args.append(v_first)
    in_specs += [row(3 * C_MIX)] + [row(C_MIX)] * 5
    args += [mu_main, mu_lora, w0, a0, k_k, k_a]
    if has_vres:
        in_specs.append(row(C_MIX))
        args.append(vres0)
    in_specs += [pl.BlockSpec((C_MIX, 4 * C_MIX), lambda ci: (0, 0)), pl.BlockSpec((C_MIX, C_MIX), lambda ci: (0, 0)),
                 row(C_MIX), row(C_MIX), row(C_MIX)]
    args += [w_lora, bd, ln_w, ln_b, r_k]
    for j in range(n_units):
        which = lambda ci, j=j: 2 * j + ci // n_q
        in_specs += [pl.BlockSpec((1, tq, C_MIX), lambda ci, w=which: (w(ci), ci % n_q, 0)),
                     pl.BlockSpec((1, l, C_MIX), lambda ci, w=which: (w(ci), 0, 1), pipeline_mode=pl.Buffered(1)),
                     pl.BlockSpec((1, l, C_MIX), lambda ci, w=which: (w(ci), 0, 2), pipeline_mode=pl.Buffered(1))]
        args += [qkv3, qkv3, qkv3]
    in_specs.append(pl.BlockSpec((2 * tq, 2 * tq), lambda ci: (0, 0)))
    args.append(_later_and_ones(tq))
    out_shape = [jax.ShapeDtypeStruct((b, l, C_MIX), BF16)]
    out_specs = [tok]
    if not has_vres:
        out_shape.append(jax.ShapeDtypeStruct((b, l, C_MIX), F32))
        out_specs.append(tok)
    out_shape += [jax.ShapeDtypeStruct((2, l, C_MIX), BF16)] * n_units
    out_specs += [pl.BlockSpec((1, tq, C_MIX), lambda ci: (ci // n_q, ci % n_q, 0))] * n_units
    res = pl.pallas_call(
        functools.partial(_mixers_kernel, has_vres=has_vres, n_units=n_units, n_qblocks=n_q),
        grid=(l // c,),
        in_specs=in_specs,
        out_specs=out_specs,
        out_shape=out_shape,
        scratch_shapes=[pltpu.VMEM((b * N_HEADS, HEAD_DIM, HEAD_DIM), F32),
                        pltpu.VMEM((b, 8, 3 * C_MIX), F32), pltpu.VMEM((b, 8, C_MIX), F32)],
        compiler_params=_params("arbitrary"),
        name="mixers",
    )(*args)
    y_sb = jnp.concatenate(res[-n_units:], axis=0)
    return (y_sb, res[0], None) if has_vres else (y_sb, res[0], res[1])


def _merge_kernel(h_ref, gsb_ref, grw_ref, ysb_ref, yrw_ref, wsb_ref, wrw_ref, wout_ref, o_ref):
    o_sb = jnp.dot(ysb_ref[...], wsb_ref[...], preferred_element_type=F32)
    o_rw = jnp.dot(yrw_ref[...], wrw_ref[...], preferred_element_type=F32)
    merged = _sigmoid(gsb_ref[...]) * o_sb + _sigmoid(grw_ref[...]) * o_rw
    o_ref[...] = h_ref[...] + jnp.dot(merged.astype(BF16), wout_ref[...], preferred_element_type=F32)


def _merge(h2, rest2, y_sb, y_rw, w_sb, w_rw, w_out):
    m = h2.shape[0]
    tm = _pick(m, (512, 256, 128))
    rows = lambda width: pl.BlockSpec((tm, width), lambda i: (i, 0))
    full = lambda shape: pl.BlockSpec(shape, lambda i: (0, 0))
    return pl.pallas_call(
        _merge_kernel,
        grid=(m // tm,),
        in_specs=[rows(D_MODEL),
                  pl.BlockSpec((tm, D_MODEL), lambda i: (i, COL_GATES // D_MODEL)),
                  pl.BlockSpec((tm, D_MODEL), lambda i: (i, COL_GATES // D_MODEL + 1)),
                  rows(C_MIX), rows(C_MIX),
                  full((C_MIX, D_MODEL)), full((C_MIX, D_MODEL)), full((D_MODEL, D_MODEL))],
        out_specs=rows(D_MODEL),
        out_shape=jax.ShapeDtypeStruct((m, D_MODEL), F32),
        compiler_params=_params("arbitrary"),
        name="merge",
    )(h2, rest2, rest2, y_sb, y_rw, w_sb, w_rw, w_out)


def _ffn_kernel(h_ref, g_ref, wg_ref, wu_ref, wo_ref, o_ref):
    x = h_ref[...]
    hn = _rms_norm(x, g_ref[...]).astype(BF16)
    gate = jnp.dot(hn, wg_ref[...], preferred_element_type=F32)
    up = jnp.dot(hn, wu_ref[...], preferred_element_type=F32)
    act = gate * _sigmoid(gate) * up
    o_ref[...] = x + jnp.dot(act.astype(BF16), wo_ref[...], preferred_element_type=F32)


def _ffn(h2, gain, w_in, w_out):
    m = h2.shape[0]
    tm = _pick(m, (384, 256, 128))
    return pl.pallas_call(
        _ffn_kernel,
        grid=(m // tm,),
        in_specs=[pl.BlockSpec((tm, D_MODEL), lambda i: (i, 0)),
                  pl.BlockSpec((1, D_MODEL), lambda i: (0, 0)),
                  pl.BlockSpec((D_MODEL, FFN_HIDDEN), lambda i: (0, 0)),
                  pl.BlockSpec((D_MODEL, FFN_HIDDEN), lambda i: (0, 1)),
                  pl.BlockSpec((FFN_HIDDEN, D_MODEL), lambda i: (0, 0))],
        out_specs=pl.BlockSpec((tm, D_MODEL), lambda i: (i, 0)),
        out_shape=jax.ShapeDtypeStruct((m, D_MODEL), F32),
        compiler_params=_params("arbitrary"),
        name="ffn",
    )(h2, gain.reshape(1, D_MODEL), w_in, w_in, w_out)


def _final_norm_kernel(h_ref, g_ref, o_ref):
    o_ref[...] = _rms_norm(h_ref[...], g_ref[...])


def _final_norm(h3, gain, s):
    b = h3.shape[0]
    tm = _pick(s, (512, 256, 128, 16))
    return pl.pallas_call(
        _final_norm_kernel,
        grid=(b, s // tm),
        in_specs=[pl.BlockSpec((pl.Element(1), pl.Element(tm), pl.Element(D_MODEL)),
                               lambda bi, i: (bi, pl.multiple_of(i * tm + N_META, N_META), 0)),
                  pl.BlockSpec((1, 1, D_MODEL), lambda bi, i: (0, 0, 0))],
        out_specs=pl.BlockSpec((1, tm, D_MODEL), lambda bi, i: (bi, i, 0)),
        out_shape=jax.ShapeDtypeStruct((b, s, D_MODEL), F32),
        compiler_params=_params("arbitrary", "arbitrary"),
        name="final_norm",
    )(h3, gain.reshape(1, 1, D_MODEL))


def _head_block_diag():
    idx = jnp.arange(C_MIX) // HEAD_DIM
    return (idx[:, None] == idx[None, :]).astype(BF16)


def _lora_weight(w_up, a_up, g_up, vres_up):
    w = jnp.zeros((C_MIX, 4 * C_MIX), F32)
    o = 0
    for seg, (mat, width) in enumerate(((w_up, W_LORA), (a_up, A_LORA), (g_up, G_LORA), (vres_up, V_LORA))):
        if mat is not None:
            w = w.at[o:o + width, seg * C_MIX:(seg + 1) * C_MIX].set(mat)
        o += width
    return w.astype(BF16)


def kernel(x, meta_tokens, norm_mix, norm_ffn, norm_final, w_in, mu_rw, w0, w_up, a0, a_up, g_up, k_k, k_a, r_k, ln_x_w, ln_x_b, vres_down, vres_mu, vres_up, vres0, w_sb_out, w_rw_out, w_out, w_ffn_in, w_ffn_out):
    b, s, d = x.shape
    depth = w_in.shape[0]
    l_real = N_META + s
    l_pad = -(-l_real // SB_BLOCK) * SB_BLOCK
    meta = jnp.broadcast_to(meta_tokens.astype(x.dtype)[None], (b, N_META, d))
    h = jnp.concatenate([meta, x, jnp.zeros((b, l_pad - l_real, d), x.dtype)], axis=1)
    h2 = h.reshape(b * l_pad, d)
    bd = _head_block_diag()
    n_in = 3 * C_MIX + RW_COLS
    row = lambda vec: vec.reshape(1, -1)

    v_first = None
    for layer in range(depth):
        wl = w_in[layer]
        vdown = vres_down[layer - 1] if layer > 0 else jnp.zeros((d, V_LORA), F32)
        w_cat = jnp.concatenate([wl[:, :n_in], vdown, jnp.zeros((d, C_MIX - LORA_USED), F32), wl[:, n_in:]],
                                axis=1).astype(BF16)
        vmu = vres_mu[layer - 1] if layer > 0 else jnp.zeros((V_LORA,), F32)
        mu_main = row(mu_rw[layer, :3 * C_MIX])
        mu_lora = row(jnp.concatenate([mu_rw[layer, 3 * C_MIX:], vmu, jnp.zeros((C_MIX - LORA_USED,), F32)]))
        w_lora = _lora_weight(w_up[layer], a_up[layer], g_up[layer], vres_up[layer - 1] if layer > 0 else None)

        qkv2, rest2 = _norm_proj(h2, norm_mix[layer], w_cat)
        y_sb, y_rw, v_out = _mixers(
            qkv2.reshape(b, l_pad, N_QKV), rest2.reshape(b, l_pad, N_REST), v_first, mu_main, mu_lora,
            row(w0[layer]), row(a0[layer]),
            row(k_k[layer]), row(k_a[layer]), row(vres0[layer - 1]) if layer > 0 else None, w_lora, bd,
            row(ln_x_w[layer]), row(ln_x_b[layer]), row(r_k[layer].reshape(-1)))
        if layer == 0:
            v_first = v_out
        flat = lambda t: t.reshape(b * l_pad, C_MIX)
        h2 = _merge(h2, rest2, flat(y_sb), flat(y_rw),
                    w_sb_out[layer].astype(BF16), w_rw_out[layer].astype(BF16), w_out[layer].astype(BF16))
        h2 = _ffn(h2, norm_ffn[layer], w_ffn_in[layer].astype(BF16), w_ffn_out[layer].astype(BF16))

    return _final_norm(h2.reshape(b, l_pad, d), norm_final, s)
```

```python
import functools

import jax
import jax.numpy as jnp
from jax import lax
from jax.experimental import pallas as pl
from jax.experimental.pallas import tpu as pltpu

D_MODEL = 1024
HEAD_DIM = 64
N_HEADS = 8
C_MIX = N_HEADS * HEAD_DIM
N_META = 16
SB_BLOCK = 128
W_LORA, A_LORA, V_LORA, G_LORA = 64, 64, 32, 160
RW_COLS = 3 * C_MIX + W_LORA + A_LORA + G_LORA
FFN_HIDDEN = 2816
RMS_EPS = 1e-6
GN_EPS = 64e-5
RW_CHUNK = 64
SB_UNDERFLOW = 104.0
SB_EAGER_BLOCKS = 2

N_QKV = 3 * C_MIX
COL_RW = 0
COL_LORA = 1536
COL_GATES = 2048
N_REST = 4096
LORA_USED = W_LORA + A_LORA + G_LORA + V_LORA

V7X_VMEM_LIMIT = 56 * 1024 * 1024

F32 = jnp.float32
BF16 = jnp.bfloat16


def _pick(n, cands):
    for c in cands:
        if n % c == 0:
            return c
    raise ValueError(f"no tile for {n} in {cands}")


def _params(*sem):
    return pltpu.CompilerParams(dimension_semantics=sem, vmem_limit_bytes=V7X_VMEM_LIMIT)


def _mm(a, b):
    return jnp.dot(a.astype(BF16), b.astype(BF16), preferred_element_type=F32)


def _mm_nt(a, b):
    return lax.dot_general(a.astype(BF16), b.astype(BF16), (((1,), (1,)), ((), ())),
                           preferred_element_type=F32)


def _mm_tn(a, b):
    return lax.dot_general(a.astype(BF16), b.astype(BF16), (((0,), (0,)), ((), ())),
                           preferred_element_type=F32)


def _each(fn, *lists):
    return [fn(*xs) for xs in zip(*lists)]


def _split_dot(x, w_bf16):
    hi = x.astype(BF16)
    lo = (x - hi.astype(F32)).astype(BF16)
    return (jnp.dot(hi, w_bf16, preferred_element_type=F32)
            + jnp.dot(lo, w_bf16, preferred_element_type=F32))


def _softplus(u):
    return jnp.maximum(u, 0.0) + jnp.log(1.0 + jnp.exp(-jnp.abs(u)))


def _sigmoid(u):
    return 1.0 / (1.0 + jnp.exp(-u))


def _rms_norm(x, gain):
    ms = jnp.mean(x * x, axis=-1, keepdims=True)
    return x * lax.rsqrt(ms + RMS_EPS) * gain


def _norm_proj_kernel(h_ref, g_ref, w_ref, qkv_ref, rest_ref):
    xn = _rms_norm(h_ref[...], g_ref[...]).astype(BF16)
    qkv_ref[...] = jnp.dot(xn, w_ref[:, :N_QKV], preferred_element_type=F32).astype(BF16)
    rest_ref[...] = jnp.dot(xn, w_ref[:, N_QKV:], preferred_element_type=F32)


def _norm_proj(h2, gain, w_cat):
    m = h2.shape[0]
    n = w_cat.shape[1]
    tm = _pick(m, (384, 256, 128))
    return pl.pallas_call(
        _norm_proj_kernel,
        grid=(m // tm,),
        in_specs=[pl.BlockSpec((tm, D_MODEL), lambda i: (i, 0)),
                  pl.BlockSpec((1, D_MODEL), lambda i: (0, 0)),
                  pl.BlockSpec((D_MODEL, n), lambda i: (0, 0))],
        out_specs=[pl.BlockSpec((tm, N_QKV), lambda i: (i, 0)),
                   pl.BlockSpec((tm, n - N_QKV), lambda i: (i, 0))],
        out_shape=[jax.ShapeDtypeStruct((m, N_QKV), BF16), jax.ShapeDtypeStruct((m, n - N_QKV), F32)],
        compiler_params=_params("arbitrary"),
        name="norm_proj",
    )(h2, gain.reshape(1, D_MODEL), w_cat)


def _sb_unit(q_ref, k_ref, v_ref, lm, o_ref, qi, tq):
    n_pairs = N_HEADS // 2
    pw = 2 * HEAD_DIM
    lane = lax.broadcasted_iota(jnp.int32, (1, pw), 1)
    head_a = lane < HEAD_DIM
    zero_bf = jnp.zeros((), BF16)
    split_rows = lambda x: jnp.concatenate([jnp.where(head_a, x, zero_bf), jnp.where(head_a, zero_bf, x)], axis=0)
    q = q_ref[0] * jnp.asarray(HEAD_DIM ** -0.5, BF16)
    q2 = [split_rows(q[:, p * pw:(p + 1) * pw]) for p in range(n_pairs)]
    row = lax.broadcasted_iota(jnp.int32, (2 * tq, tq), 0)
    col = lax.broadcasted_iota(jnp.int32, (2 * tq, tq), 1)
    causal2 = col < jnp.where(row >= tq, row - tq, row)
    state = {}

    def visit_steps(blocks, r_run, acc):
        chains = [(bi, p) for bi in range(len(blocks)) for p in range(n_pairs)]
        starts = [pl.multiple_of(blk[0] * tq, tq) for blk in blocks]
        ks = [k_ref[0, pl.ds(st, tq), :] for st in starts]
        vs = [v_ref[0, pl.ds(st, tq), :] for st in starts]
        vs = [v if blk[2] is None else jnp.where(blk[2], v, zero_bf) for v, blk in zip(vs, blocks)]
        z = [lax.dot_general(q2[p], ks[bi][:, p * pw:(p + 1) * pw], (((1,), (1,)), ((), ())),
                             preferred_element_type=F32) for bi, p in chains]
        yield
        sp = [_softplus(zi) for zi in z]
        sp = [jnp.where(causal2, s, 0.0) if blocks[bi][1] else s for s, (bi, p) in zip(sp, chains)]
        hi = [s.astype(BF16) for s in sp]
        lo = [(s - h.astype(F32)).astype(BF16) for s, h in zip(sp, hi)]
        wm = [jnp.dot(jnp.concatenate([h, l_], axis=1), lm, preferred_element_type=F32) for h, l_ in zip(hi, lo)]
        yield
        r_run, acc = list(r_run), list(acc)
        for i, (bi, p) in enumerate(chains):
            a = jnp.exp(z[i] - sp[i] - wm[i][:, :tq] - r_run[p])
            if blocks[bi][1]:
                a = jnp.where(causal2, a, 0.0)
            a = a.astype(BF16)
            v2 = split_rows(vs[bi][:, p * pw:(p + 1) * pw])
            acc[p] = acc[p] + jnp.dot(jnp.concatenate([a[:tq], a[tq:]], axis=1), v2, preferred_element_type=F32)
            r_run[p] = r_run[p] + wm[i][:, tq:]
        state["r_run"], state["acc"] = r_run, acc

    eager_blocks = [(qi, True, None)] + [(jnp.maximum(qi - back, 0), False, qi >= back)
                                         for back in range(1, SB_EAGER_BLOCKS + 1)]
    eager = visit_steps(eager_blocks, [jnp.zeros((2 * tq, tq), F32)] * n_pairs,
                        [jnp.zeros((tq, pw), F32)] * n_pairs)

    def finish():
        def more(c):
            j, rr, _ = c
            nearest = functools.reduce(jnp.minimum, [jnp.min(r[:, :1]) for r in rr])
            return jnp.logical_and(j >= 0, nearest < SB_UNDERFLOW)

        def far(c):
            j, rr, ac = c
            for _ in visit_steps([(j, False, None)], rr, ac):
                pass
            return j - 1, state["r_run"], state["acc"]

        _, _, acc = lax.while_loop(more, far, (qi - 1 - SB_EAGER_BLOCKS, state["r_run"], state["acc"]))
        o_ref[0] = jnp.concatenate(acc, axis=1).astype(o_ref.dtype)

    return eager, finish


def _later_and_ones(tq):
    s_from = jnp.arange(2 * tq)[:, None] % tq
    s_to = jnp.arange(2 * tq)[None, :]
    return jnp.where(s_to < tq, s_from > s_to, True).astype(BF16)


def _unit_lower_inverse(a_strict, rowi, coli):
    eye = (rowi == coli).astype(F32)
    same = lambda sh: (rowi >> sh) == (coli >> sh)
    size = a_strict[0].shape[0]
    levels = [same(sh) for sh in range(3, size.bit_length() - 1)] + [None]
    m8 = levels[0]
    n1 = [-jnp.where(m8, a, 0.0) for a in a_strict]
    n2 = _each(_mm, n1, n1)
    n4 = _each(_mm, n2, n2)
    t = [eye + n for n in n1]
    t = _each(lambda ti, ni: ti + _mm(ti, ni), t, n2)
    t = _each(lambda ti, ni: ti + _mm(ti, ni), t, n4)
    for inner, outer in zip(levels[:-1], levels[1:]):
        off = jnp.logical_not(inner) if outer is None else jnp.logical_and(outer, jnp.logical_not(inner))
        ta = _each(lambda ti, a: _mm(ti, jnp.where(off, a, 0.0)), t, a_strict)
        t = _each(lambda ti, tai: ti - _mm(tai, ti), t, ta)
    return t


def _mixers_kernel(*refs, has_vres, n_units, n_qblocks):
    refs = list(refs)
    n_rw_in = 15 if has_vres else 13
    rw_in, refs = refs[:n_rw_in], refs[n_rw_in:]
    sb_in, refs = refs[:3 * n_units + 1], refs[3 * n_units + 1:]
    if has_vres:
        (main_ref, lora_ref, vfirst_ref, mu_main_ref, mu_lora_ref, w0_ref, a0_ref, kk_ref, ka_ref, vres0_ref,
         wl_ref, bd_ref, lnw_ref, lnb_ref, rk_ref) = rw_in
        y_ref, vout_ref, refs = refs[0], None, refs[1:]
    else:
        (main_ref, lora_ref, mu_main_ref, mu_lora_ref, w0_ref, a0_ref, kk_ref, ka_ref,
         wl_ref, bd_ref, lnw_ref, lnb_ref, rk_ref) = rw_in
        y_ref, vout_ref, refs = refs[0], refs[1], refs[2:]
    ysb_refs, (s_ref, pm_ref, pl_ref) = refs[:n_units], refs[n_units:]
    c = RW_CHUNK
    n_batch = main_ref.shape[0]
    n_chains = n_batch * N_HEADS

    @pl.when(pl.program_id(0) == 0)
    def _():
        s_ref[...] = jnp.zeros_like(s_ref)
        pm_ref[...] = jnp.zeros_like(pm_ref)
        pl_ref[...] = jnp.zeros_like(pl_ref)

    lm = sb_in[-1][...]
    q_block = pl.program_id(0) % n_qblocks
    units = [_sb_unit(sb_in[3 * j], sb_in[3 * j + 1], sb_in[3 * j + 2], lm, ysb_refs[j], q_block, SB_BLOCK)
             for j in range(n_units)]

    def attention_step():
        for eager, _ in units:
            next(eager, None)

    first_row = lax.broadcasted_iota(jnp.int32, (c, 1), 0) == 0

    def shifted(x, prev8, mu):
        prev = jnp.where(first_row, prev8[7:8, :], pltpu.roll(x, shift=1, axis=0))
        return x + (prev - x) * mu

    mains = [main_ref[bi] for bi in range(n_batch)]
    loras = [lora_ref[bi] for bi in range(n_batch)]
    xs = jnp.concatenate([shifted(x, pm_ref[bi], mu_main_ref[...]) for bi, x in enumerate(mains)], axis=0)
    lo = jnp.concatenate([shifted(x, pl_ref[bi], mu_lora_ref[...]) for bi, x in enumerate(loras)], axis=0)
    for bi in range(n_batch):
        pm_ref[bi] = mains[bi][c - 8:c, :]
        pl_ref[bi] = loras[bi][c - 8:c, :]

    r = xs[:, 0:C_MIX]
    kr = xs[:, C_MIX:2 * C_MIX]
    vr = xs[:, 2 * C_MIX:3 * C_MIX]
    lane = lax.broadcasted_iota(jnp.int32, (1, C_MIX), 1)
    act = jnp.where(lane < W_LORA, jnp.tanh(lo),
                    jnp.where(jnp.logical_and(lane >= W_LORA + A_LORA, lane < W_LORA + A_LORA + G_LORA),
                              _sigmoid(lo), lo))
    pre = jnp.dot(act.astype(BF16), wl_ref[...], preferred_element_type=F32)
    w_log = -_softplus(-(w0_ref[...] + pre[:, 0:C_MIX])) - 0.5
    ld = -jnp.exp(w_log)
    a = _sigmoid(a0_ref[...] + pre[:, C_MIX:2 * C_MIX])
    g = pre[:, 2 * C_MIX:3 * C_MIX]
    if has_vres:
        v_first = jnp.concatenate([vfirst_ref[bi] for bi in range(n_batch)], axis=0)
        vr = vr + (v_first - vr) * _sigmoid(vres0_ref[...] + pre[:, 3 * C_MIX:4 * C_MIX])
    else:
        for bi in range(n_batch):
            vout_ref[bi] = vr[bi * c:(bi + 1) * c]
    bd = bd_ref[...]
    kk = kr * kk_ref[...]
    kk = kk / jnp.maximum(jnp.sqrt(_split_dot(kk * kk, bd)), 1e-12)
    k = kr * (1.0 + (a - 1.0) * ka_ref[...])
    bvec = kk * a

    rowi = lax.broadcasted_iota(jnp.int32, (c, c), 0)
    coli = lax.broadcasted_iota(jnp.int32, (c, c), 1)
    lower_incl = coli <= rowi
    lower_strict = coli < rowi
    tri = lower_incl.astype(F32)
    rows = lambda x, bi: x[bi * c:(bi + 1) * c]
    kt_all, bt_all, kn_all, rt_all, bh_all, kh_all, g_end = [], [], [], [], [], [], []
    for bi in range(n_batch):
        ld_b = rows(ld, bi)
        cum = jnp.dot(tri, ld_b, precision=lax.Precision.HIGHEST, preferred_element_type=F32)
        cum_end = cum[c - 1:c, :]
        e_neg = jnp.exp(-cum)
        to_end = jnp.exp(cum_end - cum)
        kt_all.append(rows(kk, bi) * jnp.exp(cum - ld_b))
        bt_all.append(rows(bvec, bi) * e_neg)
        kn_all.append(rows(k, bi) * e_neg)
        rt_all.append(rows(r, bi) * jnp.exp(cum))
        bh_all.append(rows(bvec, bi) * to_end)
        kh_all.append(rows(k, bi) * to_end)
        g_end.append(jnp.exp(cum_end))
    v_all = [rows(vr, bi) for bi in range(n_batch)]

    heads = lambda xb: [x[:, h * HEAD_DIM:(h + 1) * HEAD_DIM] for x in xb for h in range(N_HEADS)]
    kt, bt, kn, rt, bh, kh, v, g_h = (heads(x) for x in (kt_all, bt_all, kn_all, rt_all, bh_all, kh_all, v_all,
                                                        g_end))
    hd = HEAD_DIM
    kt_rt = _each(lambda x, y: jnp.concatenate([x, y], axis=0), kt, rt)
    on_b = _each(_mm_nt, kt_rt, bt)
    on_k = _each(_mm_nt, kt_rt, kn)
    a_b = [jnp.where(lower_strict, x[:c], 0.0) for x in on_b]
    a_rb = [jnp.where(lower_incl, x[c:], 0.0) for x in on_b]
    a_k = [jnp.where(lower_strict, x[:c], 0.0) for x in on_k]
    a_rk = [jnp.where(lower_incl, x[c:], 0.0) for x in on_k]
    av = _each(lambda x, y, vv: _mm(jnp.concatenate([x, y], axis=0), vv), a_k, a_rk, v)
    akv = [x[:c] for x in av]
    arkv = [x[c:] for x in av]
    vkh = _each(_mm_tn, v, kh)
    attention_step()
    t_inv = _unit_lower_inverse(a_b, rowi, coli)
    attention_step()
    ku = _each(lambda t_, x, y: _mm(t_, jnp.concatenate([x, y], axis=1)), t_inv, kt, akv)
    ry = _each(_mm, a_rb, ku)
    r_hat = _each(lambda x, z: x - z[:, :hd], rt, ry)
    y_bar = _each(lambda x, z: x - z[:, hd:], arkv, ry)
    eh = _each(_mm_tn, ku, bh)
    attention_step()
    e_bar = [x[:hd] for x in eh]
    h_add = _each(lambda x, z: x - z[hd:], vkh, eh)
    s = [s_ref[h] for h in range(n_chains)]
    ys = _each(lambda rh, sh, yb: _mm_nt(rh, sh) + yb, r_hat, s, y_bar)
    s_new = _each(lambda sh, gh, eb, ha: sh * gh - _mm(sh, eb) + ha, s, g_h, e_bar, h_add)
    for h in range(n_chains):
        s_ref[h] = s_new[h]

    y = jnp.concatenate([jnp.concatenate(ys[bi * N_HEADS:(bi + 1) * N_HEADS], axis=-1) for bi in range(n_batch)],
                        axis=0)
    inv_n = 1.0 / HEAD_DIM
    mean = _split_dot(y, bd) * inv_n
    d = y - mean
    var = _split_dot(d * d, bd) * inv_n
    yn = d * lax.rsqrt(var + GN_EPS) * lnw_ref[...] + lnb_ref[...]
    bonus = _split_dot(r * k * rk_ref[...], bd) * vr
    out = (yn + bonus) * g
    for bi in range(n_batch):
        y_ref[bi] = out[bi * c:(bi + 1) * c].astype(y_ref.dtype)

    for _, finish in units:
        finish()


def _mixers(qkv3, rest3, v_first, mu_main, mu_lora, w0, a0, k_k, k_a, vres0, w_lora, bd, ln_w, ln_b, r_k):
    b, l, _ = rest3.shape
    has_vres = v_first is not None
    c = RW_CHUNK
    tq = SB_BLOCK
    n_q = l // tq
    n_units = b // 2
    assert b % 2 == 0 and l // c == 2 * n_q, "one query block of two batch elements per unit and chunk pair"
    tok = pl.BlockSpec((b, c, C_MIX), lambda ci: (0, ci, 0))
    row = lambda width: pl.BlockSpec((1, width), lambda ci: (0, 0))
    in_specs = [pl.BlockSpec((b, c, 3 * C_MIX), lambda ci: (0, ci, COL_RW // (3 * C_MIX))),
                pl.BlockSpec((b, c, C_MIX), lambda ci: (0, ci, COL_LORA // C_MIX))]
    args = [rest3, rest3]
    if has_vres:
        in_specs.append(tok)
        args.append(v_first)
    in_specs += [row(3 * C_MIX)] + [row(C_MIX)] * 5
    args += [mu_main, mu_lora, w0, a0, k_k, k_a]
    if has_vres:
        in_specs.append(row(C_MIX))
        args.append(vres0)
    in_specs += [pl.BlockSpec((C_MIX, 4 * C_MIX), lambda ci: (0, 0)), pl.BlockSpec((C_MIX, C_MIX), lambda ci: (0, 0)),
                 row(C_MIX), row(C_MIX), row(C_MIX)]
    args += [w_lora, bd, ln_w, ln_b, r_k]
    for j in range(n_units):
        which = lambda ci, j=j: 2 * j + ci // n_q
        in_specs += [pl.BlockSpec((1, tq, C_MIX), lambda ci, w=which: (w(ci), ci % n_q, 0)),
                     pl.BlockSpec((1, l, C_MIX), lambda ci, w=which: (w(ci), 0, 1), pipeline_mode=pl.Buffered(1)),
                     pl.BlockSpec((1, l, C_MIX), lambda ci, w=which: (w(ci), 0, 2), pipeline_mode=pl.Buffered(1))]
        args += [qkv3, qkv3, qkv3]
    in_specs.append(pl.BlockSpec((2 * tq, 2 * tq), lambda ci: (0, 0)))
    args.append(_later_and_ones(tq))
    out_shape = [jax.ShapeDtypeStruct((b, l, C_MIX), BF16)]
    out_specs = [tok]
    if not has_vres:
        out_shape.append(jax.ShapeDtypeStruct((b, l, C_MIX), F32))
        out_specs.append(tok)
    out_shape += [jax.ShapeDtypeStruct((2, l, C_MIX), BF16)] * n_units
    out_specs += [pl.BlockSpec((1, tq, C_MIX), lambda ci: (ci // n_q, ci % n_q, 0))] * n_units
    res = pl.pallas_call(
        functools.partial(_mixers_kernel, has_vres=has_vres, n_units=n_units, n_qblocks=n_q),
        grid=(l // c,),
        in_specs=in_specs,
        out_specs=out_specs,
        out_shape=out_shape,
        scratch_shapes=[pltpu.VMEM((b * N_HEADS, HEAD_DIM, HEAD_DIM), F32),
                        pltpu.VMEM((b, 8, 3 * C_MIX), F32), pltpu.VMEM((b, 8, C_MIX), F32)],
        compiler_params=_params("arbitrary"),
        name="mixers",
    )(*args)
    y_sb = jnp.concatenate(res[-n_units:], axis=0)
    return (y_sb, res[0], None) if has_vres else (y_sb, res[0], res[1])


def _merge_ffn_kernel(h_ref, gsb_ref, grw_ref, ysb_ref, yrw_ref, wsb_ref, wrw_ref, wout_ref,
                      g_ref, wg_ref, wu_ref, wo_ref, o_ref):
    o_sb = jnp.dot(ysb_ref[...], wsb_ref[...], preferred_element_type=F32)
    o_rw = jnp.dot(yrw_ref[...], wrw_ref[...], preferred_element_type=F32)
    merged = _sigmoid(gsb_ref[...]) * o_sb + _sigmoid(grw_ref[...]) * o_rw
    x = h_ref[...] + jnp.dot(merged.astype(BF16), wout_ref[...], preferred_element_type=F32)
    hn = _rms_norm(x, g_ref[...]).astype(BF16)
    gate = jnp.dot(hn, wg_ref[...], preferred_element_type=F32)
    up = jnp.dot(hn, wu_ref[...], preferred_element_type=F32)
    act = gate * _sigmoid(gate) * up
    o_ref[...] = x + jnp.dot(act.astype(BF16), wo_ref[...], preferred_element_type=F32)


def _merge_ffn(h2, rest2, y_sb, y_rw, w_sb, w_rw, w_out, gain, w_ffn_in, w_ffn_out):
    m = h2.shape[0]
    tm = _pick(m, (384, 256, 128))
    rows = lambda width: pl.BlockSpec((tm, width), lambda i: (i, 0))
    full = lambda shape, col=0: pl.BlockSpec(shape, lambda i: (0, col), pipeline_mode=pl.Buffered(1))
    return pl.pallas_call(
        _merge_ffn_kernel,
        grid=(m // tm,),
        in_specs=[rows(D_MODEL),
                  pl.BlockSpec((tm, D_MODEL), lambda i: (i, COL_GATES // D_MODEL)),
                  pl.BlockSpec((tm, D_MODEL), lambda i: (i, COL_GATES // D_MODEL + 1)),
                  rows(C_MIX), rows(C_MIX),
                  full((C_MIX, D_MODEL)), full((C_MIX, D_MODEL)), full((D_MODEL, D_MODEL)),
                  pl.BlockSpec((1, D_MODEL), lambda i: (0, 0)),
                  full((D_MODEL, FFN_HIDDEN)), full((D_MODEL, FFN_HIDDEN), 1), full((FFN_HIDDEN, D_MODEL))],
        out_specs=rows(D_MODEL),
        out_shape=jax.ShapeDtypeStruct((m, D_MODEL), F32),
        compiler_params=_params("arbitrary"),
        name="merge_ffn",
    )(h2, rest2, rest2, y_sb, y_rw, w_sb, w_rw, w_out, gain.reshape(1, D_MODEL), w_ffn_in, w_ffn_in, w_ffn_out)


def _final_norm_kernel(h_ref, g_ref, o_ref):
    o_ref[...] = _rms_norm(h_ref[...], g_ref[...])


def _final_norm(h3, gain, s):
    b = h3.shape[0]
    tm = _pick(s, (512, 256, 128, 16))
    return pl.pallas_call(
        _final_norm_kernel,
        grid=(b, s // tm),
        in_specs=[pl.BlockSpec((pl.Element(1), pl.Element(tm), pl.Element(D_MODEL)),
                               lambda bi, i: (bi, pl.multiple_of(i * tm + N_META, N_META), 0)),
                  pl.BlockSpec((1, 1, D_MODEL), lambda bi, i: (0, 0, 0))],
        out_specs=pl.BlockSpec((1, tm, D_MODEL), lambda bi, i: (bi, i, 0)),
        out_shape=jax.ShapeDtypeStruct((b, s, D_MODEL), F32),
        compiler_params=_params("arbitrary", "arbitrary"),
        name="final_norm",
    )(h3, gain.reshape(1, 1, D_MODEL))


def _head_block_diag():
    idx = jnp.arange(C_MIX) // HEAD_DIM
    return (idx[:, None] == idx[None, :]).astype(BF16)


def _lora_weight(w_up, a_up, g_up, vres_up):
    w = jnp.zeros((C_MIX, 4 * C_MIX), F32)
    o = 0
    for seg, (mat, width) in enumerate(((w_up, W_LORA), (a_up, A_LORA), (g_up, G_LORA), (vres_up, V_LORA))):
        if mat is not None:
            w = w.at[o:o + width, seg * C_MIX:(seg + 1) * C_MIX].set(mat)
        o += width
    return w.astype(BF16)


def kernel(x, meta_tokens, norm_mix, norm_ffn, norm_final, w_in, mu_rw, w0, w_up, a0, a_up, g_up, k_k, k_a, r_k, ln_x_w, ln_x_b, vres_down, vres_mu, vres_up, vres0, w_sb_out, w_rw_out, w_out, w_ffn_in, w_ffn_out):
    b, s, d = x.shape
    depth = w_in.shape[0]
    l_real = N_META + s
    l_pad = -(-l_real // SB_BLOCK) * SB_BLOCK
    meta = jnp.broadcast_to(meta_tokens.astype(x.dtype)[None], (b, N_META, d))
    h = jnp.concatenate([meta, x, jnp.zeros((b, l_pad - l_real, d), x.dtype)], axis=1)
    h2 = h.reshape(b * l_pad, d)
    bd = _head_block_diag()
    n_in = 3 * C_MIX + RW_COLS
    row = lambda vec: vec.reshape(1, -1)

    v_first = None
    for layer in range(depth):
        wl = w_in[layer]
        vdown = vres_down[layer - 1] if layer > 0 else jnp.zeros((d, V_LORA), F32)
        w_cat = jnp.concatenate([wl[:, :n_in], vdown, jnp.zeros((d, C_MIX - LORA_USED), F32), wl[:, n_in:]],
                                axis=1).astype(BF16)
        vmu = vres_mu[layer - 1] if layer > 0 else jnp.zeros((V_LORA,), F32)
        mu_main = row(mu_rw[layer, :3 * C_MIX])
        mu_lora = row(jnp.concatenate([mu_rw[layer, 3 * C_MIX:], vmu, jnp.zeros((C_MIX - LORA_USED,), F32)]))
        w_lora = _lora_weight(w_up[layer], a_up[layer], g_up[layer], vres_up[layer - 1] if layer > 0 else None)

        qkv2, rest2 = _norm_proj(h2, norm_mix[layer], w_cat)
        y_sb, y_rw, v_out = _mixers(
            qkv2.reshape(b, l_pad, N_QKV), rest2.reshape(b, l_pad, N_REST), v_first, mu_main, mu_lora,
            row(w0[layer]), row(a0[layer]),
            row(k_k[layer]), row(k_a[layer]), row(vres0[layer - 1]) if layer > 0 else None, w_lora, bd,
            row(ln_x_w[layer]), row(ln_x_b[layer]), row(r_k[layer].reshape(-1)))
        if layer == 0:
            v_first = v_out
        flat = lambda t: t.reshape(b * l_pad, C_MIX)
        h2 = _merge_ffn(h2, rest2, flat(y_sb), flat(y_rw),
                        w_sb_out[layer].astype(BF16), w_rw_out[layer].astype(BF16), w_out[layer].astype(BF16),
                        norm_ffn[layer], w_ffn_in[layer].astype(BF16), w_ffn_out[layer].astype(BF16))

    return _final_norm(h2.reshape(b, l_pad, d), norm_final, s)
```

```python
import functools

import jax
import jax.numpy as jnp
from jax import lax
from jax.experimental import pallas as pl
from jax.experimental.pallas import tpu as pltpu

D_MODEL = 1024
HEAD_DIM = 64
N_HEADS = 8
C_MIX = N_HEADS * HEAD_DIM
N_META = 16
SB_BLOCK = 128
W_LORA, A_LORA, V_LORA, G_LORA = 64, 64, 32, 160
RW_COLS = 3 * C_MIX + W_LORA + A_LORA + G_LORA
FFN_HIDDEN = 2816
RMS_EPS = 1e-6
GN_EPS = 64e-5
RW_CHUNK = 64
SB_UNDERFLOW = 104.0
SB_EAGER_BLOCKS = 2

N_QKV = 3 * C_MIX
COL_RW = 0
COL_LORA = 1536
COL_GATES = 2048
N_REST = 4096
LORA_USED = W_LORA + A_LORA + G_LORA + V_LORA

V7X_VMEM_LIMIT = 56 * 1024 * 1024

F32 = jnp.float32
BF16 = jnp.bfloat16


def _pick(n, cands):
    for c in cands:
        if n % c == 0:
            return c
    raise ValueError(f"no tile for {n} in {cands}")


def _params(*sem):
    return pltpu.CompilerParams(dimension_semantics=sem, vmem_limit_bytes=V7X_VMEM_LIMIT)


def _mm(a, b):
    return jnp.dot(a.astype(BF16), b.astype(BF16), preferred_element_type=F32)


def _mm_nt(a, b):
    return lax.dot_general(a.astype(BF16), b.astype(BF16), (((1,), (1,)), ((), ())),
                           preferred_element_type=F32)


def _mm_tn(a, b):
    return lax.dot_general(a.astype(BF16), b.astype(BF16), (((0,), (0,)), ((), ())),
                           preferred_element_type=F32)


def _each(fn, *lists):
    return [fn(*xs) for xs in zip(*lists)]


def _split_dot(x, w_bf16):
    hi = x.astype(BF16)
    lo = (x - hi.astype(F32)).astype(BF16)
    return (jnp.dot(hi, w_bf16, preferred_element_type=F32)
            + jnp.dot(lo, w_bf16, preferred_element_type=F32))


def _softplus(u):
    return jnp.maximum(u, 0.0) + jnp.log(1.0 + jnp.exp(-jnp.abs(u)))


def _sigmoid(u):
    return 1.0 / (1.0 + jnp.exp(-u))


def _rms_norm(x, gain):
    ms = jnp.mean(x * x, axis=-1, keepdims=True)
    return x * lax.rsqrt(ms + RMS_EPS) * gain


def _norm_proj_kernel(h_ref, g_ref, w_ref, qkv_ref, rest_ref):
    xn = _rms_norm(h_ref[...], g_ref[...]).astype(BF16)
    qkv_ref[...] = jnp.dot(xn, w_ref[:, :N_QKV], preferred_element_type=F32).astype(BF16)
    rest_ref[...] = jnp.dot(xn, w_ref[:, N_QKV:], preferred_element_type=F32)


def _norm_proj(h2, gain, w_cat):
    m = h2.shape[0]
    n = w_cat.shape[1]
    tm = _pick(m, (384, 256, 128))
    return pl.pallas_call(
        _norm_proj_kernel,
        grid=(m // tm,),
        in_specs=[pl.BlockSpec((tm, D_MODEL), lambda i: (i, 0)),
                  pl.BlockSpec((1, D_MODEL), lambda i: (0, 0)),
                  pl.BlockSpec((D_MODEL, n), lambda i: (0, 0))],
        out_specs=[pl.BlockSpec((tm, N_QKV), lambda i: (i, 0)),
                   pl.BlockSpec((tm, n - N_QKV), lambda i: (i, 0))],
        out_shape=[jax.ShapeDtypeStruct((m, N_QKV), BF16), jax.ShapeDtypeStruct((m, n - N_QKV), F32)],
        compiler_params=_params("arbitrary"),
        name="norm_proj",
    )(h2, gain.reshape(1, D_MODEL), w_cat)


def _sb_unit(q_ref, k_ref, v_ref, lm, o_ref, qi, tq):
    n_pairs = N_HEADS // 2
    pw = 2 * HEAD_DIM
    lane = lax.broadcasted_iota(jnp.int32, (1, pw), 1)
    head_a = lane < HEAD_DIM
    zero_bf = jnp.zeros((), BF16)
    split_rows = lambda x: jnp.concatenate([jnp.where(head_a, x, zero_bf), jnp.where(head_a, zero_bf, x)], axis=0)
    q = q_ref[0] * jnp.asarray(HEAD_DIM ** -0.5, BF16)
    q2 = [split_rows(q[:, p * pw:(p + 1) * pw]) for p in range(n_pairs)]
    row = lax.broadcasted_iota(jnp.int32, (2 * tq, tq), 0)
    col = lax.broadcasted_iota(jnp.int32, (2 * tq, tq), 1)
    causal2 = col < jnp.where(row >= tq, row - tq, row)
    state = {}

    def visit_steps(blocks, r_run, acc):
        chains = [(bi, p) for bi in range(len(blocks)) for p in range(n_pairs)]
        starts = [pl.multiple_of(blk[0] * tq, tq) for blk in blocks]
        ks = [k_ref[0, pl.ds(st, tq), :] for st in starts]
        vs = [v_ref[0, pl.ds(st, tq), :] for st in starts]
        vs = [v if blk[2] is None else jnp.where(blk[2], v, zero_bf) for v, blk in zip(vs, blocks)]
        z = [lax.dot_general(q2[p], ks[bi][:, p * pw:(p + 1) * pw], (((1,), (1,)), ((), ())),
                             preferred_element_type=F32) for bi, p in chains]
        yield
        sp = [_softplus(zi) for zi in z]
        sp = [jnp.where(causal2, s, 0.0) if blocks[bi][1] else s for s, (bi, p) in zip(sp, chains)]
        hi = [s.astype(BF16) for s in sp]
        lo = [(s - h.astype(F32)).astype(BF16) for s, h in zip(sp, hi)]
        wm = [jnp.dot(jnp.concatenate([h, l_], axis=1), lm, preferred_element_type=F32) for h, l_ in zip(hi, lo)]
        yield
        r_run, acc = list(r_run), list(acc)
        for i, (bi, p) in enumerate(chains):
            a = jnp.exp(z[i] - sp[i] - wm[i][:, :tq] - r_run[p])
            if blocks[bi][1]:
                a = jnp.where(causal2, a, 0.0)
            a = a.astype(BF16)
            v2 = split_rows(vs[bi][:, p * pw:(p + 1) * pw])
            acc[p] = acc[p] + jnp.dot(jnp.concatenate([a[:tq], a[tq:]], axis=1), v2, preferred_element_type=F32)
            r_run[p] = r_run[p] + wm[i][:, tq:]
        state["r_run"], state["acc"] = r_run, acc

    eager_blocks = [(qi, True, None)] + [(jnp.maximum(qi - back, 0), False, qi >= back)
                                         for back in range(1, SB_EAGER_BLOCKS + 1)]
    eager = visit_steps(eager_blocks, [jnp.zeros((2 * tq, tq), F32)] * n_pairs,
                        [jnp.zeros((tq, pw), F32)] * n_pairs)

    def visit(blocks, r_run, acc):
        for _ in visit_steps(blocks, r_run, acc):
            pass
        return state["r_run"], state["acc"]

    def store(acc):
        o_ref[0] = jnp.concatenate(acc, axis=1).astype(o_ref.dtype)

    return eager, state, visit, store


def _sb_finish(units, qi):
    def more(c):
        j, rrs, _ = c
        nearest = jnp.min(functools.reduce(jnp.minimum, [r[:, :1] for rr in rrs for r in rr]))
        return jnp.logical_and(j >= 0, nearest < SB_UNDERFLOW)

    def far(c):
        j, rrs, acs = c
        new = [visit([(j, False, None)], rr, ac) for (_, _, visit, _), rr, ac in zip(units, rrs, acs)]
        return j - 1, [n[0] for n in new], [n[1] for n in new]

    start = (qi - 1 - SB_EAGER_BLOCKS, [u[1]["r_run"] for u in units], [u[1]["acc"] for u in units])
    _, _, acs = lax.while_loop(more, far, start)
    for (_, _, _, store), acc in zip(units, acs):
        store(acc)


def _later_and_ones(tq):
    s_from = jnp.arange(2 * tq)[:, None] % tq
    s_to = jnp.arange(2 * tq)[None, :]
    return jnp.where(s_to < tq, s_from > s_to, True).astype(BF16)


def _unit_lower_inverse(a_strict, rowi, coli):
    eye = (rowi == coli).astype(F32)
    same = lambda sh: (rowi >> sh) == (coli >> sh)
    size = a_strict[0].shape[0]
    levels = [same(sh) for sh in range(3, size.bit_length() - 1)] + [None]
    m8 = levels[0]
    n1 = [-jnp.where(m8, a, 0.0) for a in a_strict]
    n2 = _each(_mm, n1, n1)
    n4 = _each(_mm, n2, n2)
    t = [eye + n for n in n1]
    t = _each(lambda ti, ni: ti + _mm(ti, ni), t, n2)
    t = _each(lambda ti, ni: ti + _mm(ti, ni), t, n4)
    for inner, outer in zip(levels[:-1], levels[1:]):
        off = jnp.logical_not(inner) if outer is None else jnp.logical_and(outer, jnp.logical_not(inner))
        ta = _each(lambda ti, a: _mm(ti, jnp.where(off, a, 0.0)), t, a_strict)
        t = _each(lambda ti, tai: ti - _mm(tai, ti), t, ta)
    return t


def _mixers_kernel(*refs, has_vres, n_units, n_qblocks):
    refs = list(refs)
    n_rw_in = 15 if has_vres else 13
    rw_in, refs = refs[:n_rw_in], refs[n_rw_in:]
    sb_in, refs = refs[:3 * n_units + 1], refs[3 * n_units + 1:]
    if has_vres:
        (main_ref, lora_ref, vfirst_ref, mu_main_ref, mu_lora_ref, w0_ref, a0_ref, kk_ref, ka_ref, vres0_ref,
         wl_ref, bd_ref, lnw_ref, lnb_ref, rk_ref) = rw_in
        y_ref, vout_ref, refs = refs[0], None, refs[1:]
    else:
        (main_ref, lora_ref, mu_main_ref, mu_lora_ref, w0_ref, a0_ref, kk_ref, ka_ref,
         wl_ref, bd_ref, lnw_ref, lnb_ref, rk_ref) = rw_in
        y_ref, vout_ref, refs = refs[0], refs[1], refs[2:]
    ysb_refs, (s_ref, pm_ref, pl_ref) = refs[:n_units], refs[n_units:]
    c = RW_CHUNK
    n_batch = main_ref.shape[0]
    n_chains = n_batch * N_HEADS

    @pl.when(pl.program_id(0) == 0)
    def _():
        s_ref[...] = jnp.zeros_like(s_ref)
        pm_ref[...] = jnp.zeros_like(pm_ref)
        pl_ref[...] = jnp.zeros_like(pl_ref)

    lm = sb_in[-1][...]
    q_block = pl.program_id(0) % n_qblocks
    units = [_sb_unit(sb_in[3 * j], sb_in[3 * j + 1], sb_in[3 * j + 2], lm, ysb_refs[j], q_block, SB_BLOCK)
             for j in range(n_units)]

    def attention_step():
        for unit in units:
            next(unit[0], None)

    first_row = lax.broadcasted_iota(jnp.int32, (c, 1), 0) == 0

    def shifted(x, prev8, mu):
        prev = jnp.where(first_row, prev8[7:8, :], pltpu.roll(x, shift=1, axis=0))
        return x + (prev - x) * mu

    mains = [main_ref[bi] for bi in range(n_batch)]
    loras = [lora_ref[bi] for bi in range(n_batch)]
    xs = jnp.concatenate([shifted(x, pm_ref[bi], mu_main_ref[...]) for bi, x in enumerate(mains)], axis=0)
    lo = jnp.concatenate([shifted(x, pl_ref[bi], mu_lora_ref[...]) for bi, x in enumerate(loras)], axis=0)
    for bi in range(n_batch):
        pm_ref[bi] = mains[bi][c - 8:c, :]
        pl_ref[bi] = loras[bi][c - 8:c, :]

    r = xs[:, 0:C_MIX]
    kr = xs[:, C_MIX:2 * C_MIX]
    vr = xs[:, 2 * C_MIX:3 * C_MIX]
    lane = lax.broadcasted_iota(jnp.int32, (1, C_MIX), 1)
    act = jnp.where(lane < W_LORA, jnp.tanh(lo),
                    jnp.where(jnp.logical_and(lane >= W_LORA + A_LORA, lane < W_LORA + A_LORA + G_LORA),
                              _sigmoid(lo), lo))
    pre = jnp.dot(act.astype(BF16), wl_ref[...], preferred_element_type=F32)
    w_log = -_softplus(-(w0_ref[...] + pre[:, 0:C_MIX])) - 0.5
    ld = -jnp.exp(w_log)
    a = _sigmoid(a0_ref[...] + pre[:, C_MIX:2 * C_MIX])
    g = pre[:, 2 * C_MIX:3 * C_MIX]
    if has_vres:
        v_first = jnp.concatenate([vfirst_ref[bi] for bi in range(n_batch)], axis=0)
        vr = vr + (v_first - vr) * _sigmoid(vres0_ref[...] + pre[:, 3 * C_MIX:4 * C_MIX])
    else:
        for bi in range(n_batch):
            vout_ref[bi] = vr[bi * c:(bi + 1) * c]
    bd = bd_ref[...]
    kk = kr * kk_ref[...]
    kk = kk / jnp.maximum(jnp.sqrt(_split_dot(kk * kk, bd)), 1e-12)
    k = kr * (1.0 + (a - 1.0) * ka_ref[...])
    bvec = kk * a

    rowi = lax.broadcasted_iota(jnp.int32, (c, c), 0)
    coli = lax.broadcasted_iota(jnp.int32, (c, c), 1)
    lower_incl = coli <= rowi
    lower_strict = coli < rowi
    tri = lower_incl.astype(F32)
    rows = lambda x, bi: x[bi * c:(bi + 1) * c]
    kt_all, bt_all, kn_all, rt_all, bh_all, kh_all, g_end = [], [], [], [], [], [], []
    for bi in range(n_batch):
        ld_b = rows(ld, bi)
        cum = jnp.dot(tri, ld_b, precision=lax.Precision.HIGHEST, preferred_element_type=F32)
        cum_end = cum[c - 1:c, :]
        e_neg = jnp.exp(-cum)
        to_end = jnp.exp(cum_end - cum)
        kt_all.append(rows(kk, bi) * jnp.exp(cum - ld_b))
        bt_all.append(rows(bvec, bi) * e_neg)
        kn_all.append(rows(k, bi) * e_neg)
        rt_all.append(rows(r, bi) * jnp.exp(cum))
        bh_all.append(rows(bvec, bi) * to_end)
        kh_all.append(rows(k, bi) * to_end)
        g_end.append(jnp.exp(cum_end))
    v_all = [rows(vr, bi) for bi in range(n_batch)]

    heads = lambda xb: [x[:, h * HEAD_DIM:(h + 1) * HEAD_DIM] for x in xb for h in range(N_HEADS)]
    kt, bt, kn, rt, bh, kh, v, g_h = (heads(x) for x in (kt_all, bt_all, kn_all, rt_all, bh_all, kh_all, v_all,
                                                        g_end))
    hd = HEAD_DIM
    kt_rt = _each(lambda x, y: jnp.concatenate([x, y], axis=0), kt, rt)
    on_b = _each(_mm_nt, kt_rt, bt)
    on_k = _each(_mm_nt, kt_rt, kn)
    a_b = [jnp.where(lower_strict, x[:c], 0.0) for x in on_b]
    a_rb = [jnp.where(lower_incl, x[c:], 0.0) for x in on_b]
    a_k = [jnp.where(lower_strict, x[:c], 0.0) for x in on_k]
    a_rk = [jnp.where(lower_incl, x[c:], 0.0) for x in on_k]
    av = _each(lambda x, y, vv: _mm(jnp.concatenate([x, y], axis=0), vv), a_k, a_rk, v)
    akv = [x[:c] for x in av]
    arkv = [x[c:] for x in av]
    vkh = _each(_mm_tn, v, kh)
    attention_step()
    t_inv = _unit_lower_inverse(a_b, rowi, coli)
    attention_step()
    ku = _each(lambda t_, x, y: _mm(t_, jnp.concatenate([x, y], axis=1)), t_inv, kt, akv)
    ry = _each(_mm, a_rb, ku)
    r_hat = _each(lambda x, z: x - z[:, :hd], rt, ry)
    y_bar = _each(lambda x, z: x - z[:, hd:], arkv, ry)
    eh = _each(_mm_tn, ku, bh)
    attention_step()
    e_bar = [x[:hd] for x in eh]
    h_add = _each(lambda x, z: x - z[hd:], vkh, eh)
    s = [s_ref[h] for h in range(n_chains)]
    ys = _each(lambda rh, sh, yb: _mm_nt(rh, sh) + yb, r_hat, s, y_bar)
    s_new = _each(lambda sh, gh, eb, ha: sh * gh - _mm(sh, eb) + ha, s, g_h, e_bar, h_add)
    for h in range(n_chains):
        s_ref[h] = s_new[h]

    y = jnp.concatenate([jnp.concatenate(ys[bi * N_HEADS:(bi + 1) * N_HEADS], axis=-1) for bi in range(n_batch)],
                        axis=0)
    inv_n = 1.0 / HEAD_DIM
    mean = _split_dot(y, bd) * inv_n
    d = y - mean
    var = _split_dot(d * d, bd) * inv_n
    yn = d * lax.rsqrt(var + GN_EPS) * lnw_ref[...] + lnb_ref[...]
    bonus = _split_dot(r * k * rk_ref[...], bd) * vr
    out = (yn + bonus) * g
    for bi in range(n_batch):
        y_ref[bi] = out[bi * c:(bi + 1) * c].astype(y_ref.dtype)

    _sb_finish(units, q_block)


def _mixers(qkv3, rest3, v_first, mu_main, mu_lora, w0, a0, k_k, k_a, vres0, w_lora, bd, ln_w, ln_b, r_k):
    b, l, _ = rest3.shape
    has_vres = v_first is not None
    c = RW_CHUNK
    tq = SB_BLOCK
    n_q = l // tq
    n_units = b // 2
    assert b % 2 == 0 and l // c == 2 * n_q, "one query block of two batch elements per unit and chunk pair"
    tok = pl.BlockSpec((b, c, C_MIX), lambda ci: (0, ci, 0))
    row = lambda width: pl.BlockSpec((1, width), lambda ci: (0, 0))
    in_specs = [pl.BlockSpec((b, c, 3 * C_MIX), lambda ci: (0, ci, COL_RW // (3 * C_MIX))),
                pl.BlockSpec((b, c, C_MIX), lambda ci: (0, ci, COL_LORA // C_MIX))]
    args = [rest3, rest3]
    if has_vres:
        in_specs.append(tok)
        args.append(v_first)
    in_specs += [row(3 * C_MIX)] + [row(C_MIX)] * 5
    args += [mu_main, mu_lora, w0, a0, k_k, k_a]
    if has_vres:
        in_specs.append(row(C_MIX))
        args.append(vres0)
    in_specs += [pl.BlockSpec((C_MIX, 4 * C_MIX), lambda ci: (0, 0)), pl.BlockSpec((C_MIX, C_MIX), lambda ci: (0, 0)),
                 row(C_MIX), row(C_MIX), row(C_MIX)]
    args += [w_lora, bd, ln_w, ln_b, r_k]
    for j in range(n_units):
        which = lambda ci, j=j: 2 * j + ci // n_q
        in_specs += [pl.BlockSpec((1, tq, C_MIX), lambda ci, w=which: (w(ci), ci % n_q, 0)),
                     pl.BlockSpec((1, l, C_MIX), lambda ci, w=which: (w(ci), 0, 1), pipeline_mode=pl.Buffered(1)),
                     pl.BlockSpec((1, l, C_MIX), lambda ci, w=which: (w(ci), 0, 2), pipeline_mode=pl.Buffered(1))]
        args += [qkv3, qkv3, qkv3]
    in_specs.append(pl.BlockSpec((2 * tq, 2 * tq), lambda ci: (0, 0)))
    args.append(_later_and_ones(tq))
    out_shape = [jax.ShapeDtypeStruct((b, l, C_MIX), BF16)]
    out_specs = [tok]
    if not has_vres:
        out_shape.append(jax.ShapeDtypeStruct((b, l, C_MIX), F32))
        out_specs.append(tok)
    out_shape += [jax.ShapeDtypeStruct((2, l, C_MIX), BF16)] * n_units
    out_specs += [pl.BlockSpec((1, tq, C_MIX), lambda ci: (ci // n_q, ci % n_q, 0))] * n_units
    res = pl.pallas_call(
        functools.partial(_mixers_kernel, has_vres=has_vres, n_units=n_units, n_qblocks=n_q),
        grid=(l // c,),
        in_specs=in_specs,
        out_specs=out_specs,
        out_shape=out_shape,
        scratch_shapes=[pltpu.VMEM((b * N_HEADS, HEAD_DIM, HEAD_DIM), F32),
                        pltpu.VMEM((b, 8, 3 * C_MIX), F32), pltpu.VMEM((b, 8, C_MIX), F32)],
        compiler_params=_params("arbitrary"),
        name="mixers",
    )(*args)
    y_sb_units = [t.reshape(2 * l, C_MIX) for t in res[-n_units:]]
    return (y_sb_units, res[0], None) if has_vres else (y_sb_units, res[0], res[1])


def _merge_ffn_kernel(*refs, n_units, blocks_per_unit):
    h_ref, gsb_ref, grw_ref = refs[:3]
    ysb_refs = refs[3:3 + n_units]
    yrw_ref, wsb_ref, wrw_ref, wout_ref, g_ref, wg_ref, wu_ref, wo_ref, o_ref = refs[3 + n_units:]
    unit = pl.program_id(0) // blocks_per_unit
    y_sb = ysb_refs[0][...]
    for j in range(1, n_units):
        y_sb = jnp.where(unit == j, ysb_refs[j][...], y_sb)
    o_sb = jnp.dot(y_sb, wsb_ref[...], preferred_element_type=F32)
    o_rw = jnp.dot(yrw_ref[...], wrw_ref[...], preferred_element_type=F32)
    merged = _sigmoid(gsb_ref[...]) * o_sb + _sigmoid(grw_ref[...]) * o_rw
    x = h_ref[...] + jnp.dot(merged.astype(BF16), wout_ref[...], preferred_element_type=F32)
    hn = _rms_norm(x, g_ref[...]).astype(BF16)
    gate = jnp.dot(hn, wg_ref[...], preferred_element_type=F32)
    up = jnp.dot(hn, wu_ref[...], preferred_element_type=F32)
    act = gate * _sigmoid(gate) * up
    o_ref[...] = x + jnp.dot(act.astype(BF16), wo_ref[...], preferred_element_type=F32)


def _merge_ffn(h2, rest2, y_sb_units, y_rw, w_sb, w_rw, w_out, gain, w_ffn_in, w_ffn_out):
    m = h2.shape[0]
    n_units = len(y_sb_units)
    tm = _pick(m // n_units, (384, 256, 128))
    per_unit = m // n_units // tm
    rows = lambda width: pl.BlockSpec((tm, width), lambda i: (i, 0))
    full = lambda shape, col=0: pl.BlockSpec(shape, lambda i: (0, col), pipeline_mode=pl.Buffered(1))
    unit_rows = [pl.BlockSpec((tm, C_MIX), lambda i, j=j: (jnp.clip(i - j * per_unit, 0, per_unit - 1), 0))
                 for j in range(n_units)]
    return pl.pallas_call(
        functools.partial(_merge_ffn_kernel, n_units=n_units, blocks_per_unit=per_unit),
        grid=(m // tm,),
        in_specs=[rows(D_MODEL),
                  pl.BlockSpec((tm, D_MODEL), lambda i: (i, COL_GATES // D_MODEL)),
                  pl.BlockSpec((tm, D_MODEL), lambda i: (i, COL_GATES // D_MODEL + 1)),
                  *unit_rows, rows(C_MIX),
                  full((C_MIX, D_MODEL)), full((C_MIX, D_MODEL)), full((D_MODEL, D_MODEL)),
                  pl.BlockSpec((1, D_MODEL), lambda i: (0, 0)),
                  full((D_MODEL, FFN_HIDDEN)), full((D_MODEL, FFN_HIDDEN), 1), full((FFN_HIDDEN, D_MODEL))],
        out_specs=rows(D_MODEL),
        out_shape=jax.ShapeDtypeStruct((m, D_MODEL), F32),
        compiler_params=_params("arbitrary"),
        name="merge_ffn",
    )(h2, rest2, rest2, *y_sb_units, y_rw, w_sb, w_rw, w_out, gain.reshape(1, D_MODEL),
      w_ffn_in, w_ffn_in, w_ffn_out)


def _final_norm_kernel(h_ref, g_ref, o_ref):
    o_ref[...] = _rms_norm(h_ref[...], g_ref[...])


def _final_norm(h3, gain, s):
    b = h3.shape[0]
    tm = _pick(s, (512, 256, 128, 16))
    return pl.pallas_call(
        _final_norm_kernel,
        grid=(b, s // tm),
        in_specs=[pl.BlockSpec((pl.Element(1), pl.Element(tm), pl.Element(D_MODEL)),
                               lambda bi, i: (bi, pl.multiple_of(i * tm + N_META, N_META), 0)),
                  pl.BlockSpec((1, 1, D_MODEL), lambda bi, i: (0, 0, 0))],
        out_specs=pl.BlockSpec((1, tm, D_MODEL), lambda bi, i: (bi, i, 0)),
        out_shape=jax.ShapeDtypeStruct((b, s, D_MODEL), F32),
        compiler_params=_params("arbitrary", "arbitrary"),
        name="final_norm",
    )(h3, gain.reshape(1, 1, D_MODEL))


def _head_block_diag():
    idx = jnp.arange(C_MIX) // HEAD_DIM
    return (idx[:, None] == idx[None, :]).astype(BF16)


def _lora_weight(w_up, a_up, g_up, vres_up):
    w = jnp.zeros((C_MIX, 4 * C_MIX), F32)
    o = 0
    for seg, (mat, width) in enumerate(((w_up, W_LORA), (a_up, A_LORA), (g_up, G_LORA), (vres_up, V_LORA))):
        if mat is not None:
            w = w.at[o:o + width, seg * C_MIX:(seg + 1) * C_MIX].set(mat)
        o += width
    return w.astype(BF16)


def kernel(x, meta_tokens, norm_mix, norm_ffn, norm_final, w_in, mu_rw, w0, w_up, a0, a_up, g_up, k_k, k_a, r_k, ln_x_w, ln_x_b, vres_down, vres_mu, vres_up, vres0, w_sb_out, w_rw_out, w_out, w_ffn_in, w_ffn_out):
    b, s, d = x.shape
    depth = w_in.shape[0]
    l_real = N_META + s
    l_pad = -(-l_real // SB_BLOCK) * SB_BLOCK
    meta = jnp.broadcast_to(meta_tokens.astype(x.dtype)[None], (b, N_META, d))
    h = jnp.concatenate([meta, x, jnp.zeros((b, l_pad - l_real, d), x.dtype)], axis=1)
    h2 = h.reshape(b * l_pad, d)
    bd = _head_block_diag()
    n_in = 3 * C_MIX + RW_COLS
    row = lambda vec: vec.reshape(1, -1)

    v_first = None
    for layer in range(depth):
        wl = w_in[layer]
        vdown = vres_down[layer - 1] if layer > 0 else jnp.zeros((d, V_LORA), F32)
        w_cat = jnp.concatenate([wl[:, :n_in], vdown, jnp.zeros((d, C_MIX - LORA_USED), F32), wl[:, n_in:]],
                                axis=1).astype(BF16)
        vmu = vres_mu[layer - 1] if layer > 0 else jnp.zeros((V_LORA,), F32)
        mu_main = row(mu_rw[layer, :3 * C_MIX])
        mu_lora = row(jnp.concatenate([mu_rw[layer, 3 * C_MIX:], vmu, jnp.zeros((C_MIX - LORA_USED,), F32)]))
        w_lora = _lora_weight(w_up[layer], a_up[layer], g_up[layer], vres_up[layer - 1] if layer > 0 else None)

        qkv2, rest2 = _norm_proj(h2, norm_mix[layer], w_cat)
        y_sb, y_rw, v_out = _mixers(
            qkv2.reshape(b, l_pad, N_QKV), rest2.reshape(b, l_pad, N_REST), v_first, mu_main, mu_lora,
            row(w0[layer]), row(a0[layer]),
            row(k_k[layer]), row(k_a[layer]), row(vres0[layer - 1]) if layer > 0 else None, w_lora, bd,
            row(ln_x_w[layer]), row(ln_x_b[layer]), row(r_k[layer].reshape(-1)))
        if layer == 0:
            v_first = v_out
        flat = lambda t: t.reshape(b * l_pad, C_MIX)
        h2 = _merge_ffn(h2, rest2, y_sb, flat(y_rw),
                        w_sb_out[layer].astype(BF16), w_rw_out[layer].astype(BF16), w_out[layer].astype(BF16),
                        norm_ffn[layer], w_ffn_in[layer].astype(BF16), w_ffn_out[layer].astype(BF16))

    return _final_norm(h2.reshape(b, l_pad, d), norm_final, s)
```

```python
import functools

import jax
import jax.numpy as jnp
from jax import lax
from jax.experimental import pallas as pl
from jax.experimental.pallas import tpu as pltpu

D_MODEL = 1024
HEAD_DIM = 64
N_HEADS = 8
C_MIX = N_HEADS * HEAD_DIM
N_META = 16
SB_BLOCK = 128
W_LORA, A_LORA, V_LORA, G_LORA = 64, 64, 32, 160
RW_COLS = 3 * C_MIX + W_LORA + A_LORA + G_LORA
FFN_HIDDEN = 2816
RMS_EPS = 1e-6
GN_EPS = 64e-5
RW_CHUNK = 64
SB_UNDERFLOW = 104.0
SB_EAGER_BLOCKS = 2

N_QKV = 3 * C_MIX
COL_RW = 0
COL_LORA = 1536
COL_GATES = 2048
N_REST = 4096
LORA_USED = W_LORA + A_LORA + G_LORA + V_LORA

V7X_VMEM_LIMIT = 56 * 1024 * 1024

F32 = jnp.float32
BF16 = jnp.bfloat16


def _pick(n, cands):
    for c in cands:
        if n % c == 0:
            return c
    raise ValueError(f"no tile for {n} in {cands}")


def _params(*sem):
    return pltpu.CompilerParams(dimension_semantics=sem, vmem_limit_bytes=V7X_VMEM_LIMIT)


def _mm(a, b):
    return jnp.dot(a.astype(BF16), b.astype(BF16), preferred_element_type=F32)


def _mm_nt(a, b):
    return lax.dot_general(a.astype(BF16), b.astype(BF16), (((1,), (1,)), ((), ())),
                           preferred_element_type=F32)


def _mm_tn(a, b):
    return lax.dot_general(a.astype(BF16), b.astype(BF16), (((0,), (0,)), ((), ())),
                           preferred_element_type=F32)


def _each(fn, *lists):
    return [fn(*xs) for xs in zip(*lists)]


def _split_dot(x, w_bf16):
    hi = x.astype(BF16)
    lo = (x - hi.astype(F32)).astype(BF16)
    return (jnp.dot(hi, w_bf16, preferred_element_type=F32)
            + jnp.dot(lo, w_bf16, preferred_element_type=F32))


def _prefix_dot(tri_bf16, x):
    hi = x.astype(BF16)
    rem = x - hi.astype(F32)
    mid = rem.astype(BF16)
    lo = (rem - mid.astype(F32)).astype(BF16)
    return (jnp.dot(tri_bf16, hi, preferred_element_type=F32) + jnp.dot(tri_bf16, mid, preferred_element_type=F32)
            + jnp.dot(tri_bf16, lo, preferred_element_type=F32))


def _softplus(u):
    return jnp.maximum(u, 0.0) + jnp.log(1.0 + jnp.exp(-jnp.abs(u)))


def _sigmoid(u):
    return 1.0 / (1.0 + jnp.exp(-u))


def _rms_norm(x, gain):
    ms = jnp.mean(x * x, axis=-1, keepdims=True)
    return x * lax.rsqrt(ms + RMS_EPS) * gain


def _norm_proj_kernel(h_ref, g_ref, w_ref, qkv_ref, rest_ref):
    xn = _rms_norm(h_ref[...], g_ref[...]).astype(BF16)
    qkv_ref[...] = jnp.dot(xn, w_ref[:, :N_QKV], preferred_element_type=F32).astype(BF16)
    rest_ref[...] = jnp.dot(xn, w_ref[:, N_QKV:], preferred_element_type=F32).astype(BF16)


def _norm_proj(h2, gain, w_cat):
    m = h2.shape[0]
    n = w_cat.shape[1]
    tm = _pick(m, (384, 256, 128))
    return pl.pallas_call(
        _norm_proj_kernel,
        grid=(m // tm,),
        in_specs=[pl.BlockSpec((tm, D_MODEL), lambda i: (i, 0)),
                  pl.BlockSpec((1, D_MODEL), lambda i: (0, 0)),
                  pl.BlockSpec((D_MODEL, n), lambda i: (0, 0))],
        out_specs=[pl.BlockSpec((tm, N_QKV), lambda i: (i, 0)),
                   pl.BlockSpec((tm, n - N_QKV), lambda i: (i, 0))],
        out_shape=[jax.ShapeDtypeStruct((m, N_QKV), BF16), jax.ShapeDtypeStruct((m, n - N_QKV), BF16)],
        compiler_params=_params("arbitrary"),
        name="norm_proj",
    )(h2, gain.reshape(1, D_MODEL), w_cat)


def _sb_unit(q_ref, k_ref, v_ref, lm, o_ref, qi, tq):
    n_pairs = N_HEADS // 2
    pw = 2 * HEAD_DIM
    lane = lax.broadcasted_iota(jnp.int32, (1, pw), 1)
    head_a = lane < HEAD_DIM
    zero_bf = jnp.zeros((), BF16)
    split_rows = lambda x: jnp.concatenate([jnp.where(head_a, x, zero_bf), jnp.where(head_a, zero_bf, x)], axis=0)
    q = q_ref[0] * jnp.asarray(HEAD_DIM ** -0.5, BF16)
    q2 = [split_rows(q[:, p * pw:(p + 1) * pw]) for p in range(n_pairs)]
    row = lax.broadcasted_iota(jnp.int32, (2 * tq, tq), 0)
    col = lax.broadcasted_iota(jnp.int32, (2 * tq, tq), 1)
    causal2 = col < jnp.where(row >= tq, row - tq, row)
    state = {}

    def visit_steps(blocks, r_run, acc):
        chains = [(bi, p) for bi in range(len(blocks)) for p in range(n_pairs)]
        starts = [pl.multiple_of(blk[0] * tq, tq) for blk in blocks]
        ks = [k_ref[0, pl.ds(st, tq), :] for st in starts]
        vs = [v_ref[0, pl.ds(st, tq), :] for st in starts]
        vs = [v if blk[2] is None else jnp.where(blk[2], v, zero_bf) for v, blk in zip(vs, blocks)]
        z = [lax.dot_general(q2[p], ks[bi][:, p * pw:(p + 1) * pw], (((1,), (1,)), ((), ())),
                             preferred_element_type=F32) for bi, p in chains]
        yield
        sp = [_softplus(zi) for zi in z]
        sp = [jnp.where(causal2, s, 0.0) if blocks[bi][1] else s for s, (bi, p) in zip(sp, chains)]
        hi = [s.astype(BF16) for s in sp]
        lo = [(s - h.astype(F32)).astype(BF16) for s, h in zip(sp, hi)]
        wm = [jnp.dot(jnp.concatenate([h, l_], axis=1), lm, preferred_element_type=F32) for h, l_ in zip(hi, lo)]
        yield
        r_run, acc = list(r_run), list(acc)
        for i, (bi, p) in enumerate(chains):
            a = jnp.exp(z[i] - sp[i] - wm[i][:, :tq] - r_run[p])
            if blocks[bi][1]:
                a = jnp.where(causal2, a, 0.0)
            a = a.astype(BF16)
            v2 = split_rows(vs[bi][:, p * pw:(p + 1) * pw])
            acc[p] = acc[p] + jnp.dot(jnp.concatenate([a[:tq], a[tq:]], axis=1), v2, preferred_element_type=F32)
            r_run[p] = r_run[p] + wm[i][:, tq:]
        state["r_run"], state["acc"] = r_run, acc

    eager_blocks = [(qi, True, None)] + [(jnp.maximum(qi - back, 0), False, qi >= back)
                                         for back in range(1, SB_EAGER_BLOCKS + 1)]
    eager = visit_steps(eager_blocks, [jnp.zeros((2 * tq, tq), F32)] * n_pairs,
                        [jnp.zeros((tq, pw), F32)] * n_pairs)

    def visit(blocks, r_run, acc):
        for _ in visit_steps(blocks, r_run, acc):
            pass
        return state["r_run"], state["acc"]

    def store(acc):
        o_ref[0] = jnp.concatenate(acc, axis=1).astype(o_ref.dtype)

    return eager, state, visit, store


def _sb_finish(units, qi):
    def more(c):
        j, rrs, _ = c
        nearest = jnp.min(functools.reduce(jnp.minimum, [r[:, :1] for rr in rrs for r in rr]))
        return jnp.logical_and(j >= 0, nearest < SB_UNDERFLOW)

    def far(c):
        j, rrs, acs = c
        new = [visit([(j, False, None)], rr, ac) for (_, _, visit, _), rr, ac in zip(units, rrs, acs)]
        return j - 1, [n[0] for n in new], [n[1] for n in new]

    start = (qi - 1 - SB_EAGER_BLOCKS, [u[1]["r_run"] for u in units], [u[1]["acc"] for u in units])
    _, _, acs = lax.while_loop(more, far, start)
    for (_, _, _, store), acc in zip(units, acs):
        store(acc)


def _later_and_ones(tq):
    s_from = jnp.arange(2 * tq)[:, None] % tq
    s_to = jnp.arange(2 * tq)[None, :]
    return jnp.where(s_to < tq, s_from > s_to, True).astype(BF16)


def _unit_lower_inverse(a_strict, rowi, coli):
    eye = (rowi == coli).astype(F32)
    same = lambda sh: (rowi >> sh) == (coli >> sh)
    size = a_strict[0].shape[0]
    levels = [same(sh) for sh in range(3, size.bit_length() - 1)] + [None]
    m8 = levels[0]
    n1 = [-jnp.where(m8, a, 0.0) for a in a_strict]
    n2 = _each(_mm, n1, n1)
    n4 = _each(_mm, n2, n2)
    t = [eye + n for n in n1]
    t = _each(lambda ti, ni: ti + _mm(ti, ni), t, n2)
    t = _each(lambda ti, ni: ti + _mm(ti, ni), t, n4)
    for inner, outer in zip(levels[:-1], levels[1:]):
        off = jnp.logical_not(inner) if outer is None else jnp.logical_and(outer, jnp.logical_not(inner))
        ta = _each(lambda ti, a: _mm(ti, jnp.where(off, a, 0.0)), t, a_strict)
        t = _each(lambda ti, tai: ti - _mm(tai, ti), t, ta)
    return t


def _mixers_kernel(*refs, has_vres, n_units, n_qblocks):
    refs = list(refs)
    n_rw_in = 15 if has_vres else 13
    rw_in, refs = refs[:n_rw_in], refs[n_rw_in:]
    sb_in, refs = refs[:3 * n_units + 1], refs[3 * n_units + 1:]
    if has_vres:
        (main_ref, lora_ref, vfirst_ref, mu_main_ref, mu_lora_ref, w0_ref, a0_ref, kk_ref, ka_ref, vres0_ref,
         wl_ref, bd_ref, lnw_ref, lnb_ref, rk_ref) = rw_in
        y_ref, vout_ref, refs = refs[0], None, refs[1:]
    else:
        (main_ref, lora_ref, mu_main_ref, mu_lora_ref, w0_ref, a0_ref, kk_ref, ka_ref,
         wl_ref, bd_ref, lnw_ref, lnb_ref, rk_ref) = rw_in
        y_ref, vout_ref, refs = refs[0], refs[1], refs[2:]
    ysb_refs, (s_ref, pm_ref, pl_ref) = refs[:n_units], refs[n_units:]
    c = RW_CHUNK
    n_batch = main_ref.shape[0]
    n_chains = n_batch * N_HEADS

    @pl.when(pl.program_id(0) == 0)
    def _():
        s_ref[...] = jnp.zeros_like(s_ref)
        pm_ref[...] = jnp.zeros_like(pm_ref)
        pl_ref[...] = jnp.zeros_like(pl_ref)

    lm = sb_in[-1][...]
    q_block = pl.program_id(0) % n_qblocks
    units = [_sb_unit(sb_in[3 * j], sb_in[3 * j + 1], sb_in[3 * j + 2], lm, ysb_refs[j], q_block, SB_BLOCK)
             for j in range(n_units)]

    def attention_step():
        for unit in units:
            next(unit[0], None)

    first_row = lax.broadcasted_iota(jnp.int32, (c, 1), 0) == 0

    def shifted(x, prev8, mu):
        prev = jnp.where(first_row, prev8[7:8, :], pltpu.roll(x, shift=1, axis=0))
        return x + (prev - x) * mu

    mains = [main_ref[bi].astype(F32) for bi in range(n_batch)]
    loras = [lora_ref[bi].astype(F32) for bi in range(n_batch)]
    xs = jnp.concatenate([shifted(x, pm_ref[bi], mu_main_ref[...]) for bi, x in enumerate(mains)], axis=0)
    lo = jnp.concatenate([shifted(x, pl_ref[bi], mu_lora_ref[...]) for bi, x in enumerate(loras)], axis=0)
    for bi in range(n_batch):
        pm_ref[bi] = mains[bi][c - 8:c, :]
        pl_ref[bi] = loras[bi][c - 8:c, :]

    r = xs[:, 0:C_MIX]
    kr = xs[:, C_MIX:2 * C_MIX]
    vr = xs[:, 2 * C_MIX:3 * C_MIX]
    lane = lax.broadcasted_iota(jnp.int32, (1, C_MIX), 1)
    act = jnp.where(lane < W_LORA, jnp.tanh(lo),
                    jnp.where(jnp.logical_and(lane >= W_LORA + A_LORA, lane < W_LORA + A_LORA + G_LORA),
                              _sigmoid(lo), lo))
    pre = jnp.dot(act.astype(BF16), wl_ref[...], preferred_element_type=F32)
    w_log = -_softplus(-(w0_ref[...] + pre[:, 0:C_MIX])) - 0.5
    ld = -jnp.exp(w_log)
    a = _sigmoid(a0_ref[...] + pre[:, C_MIX:2 * C_MIX])
    g = pre[:, 2 * C_MIX:3 * C_MIX]
    if has_vres:
        v_first = jnp.concatenate([vfirst_ref[bi] for bi in range(n_batch)], axis=0)
        vr = vr + (v_first - vr) * _sigmoid(vres0_ref[...] + pre[:, 3 * C_MIX:4 * C_MIX])
    else:
        for bi in range(n_batch):
            vout_ref[bi] = vr[bi * c:(bi + 1) * c]
    bd = bd_ref[...]
    kk = kr * kk_ref[...]
    kk = kk * lax.rsqrt(jnp.maximum(_split_dot(kk * kk, bd), 1e-24))
    k = kr * (1.0 + (a - 1.0) * ka_ref[...])
    bvec = kk * a

    rowi = lax.broadcasted_iota(jnp.int32, (c, c), 0)
    coli = lax.broadcasted_iota(jnp.int32, (c, c), 1)
    lower_incl = coli <= rowi
    lower_strict = coli < rowi
    tri = lower_incl.astype(BF16)
    rows = lambda x, bi: x[bi * c:(bi + 1) * c]
    kt_all, bt_all, kn_all, rt_all, bh_all, kh_all, g_end = [], [], [], [], [], [], []
    for bi in range(n_batch):
        ld_b = rows(ld, bi)
        cum = _prefix_dot(tri, ld_b)
        cum_end = cum[c - 1:c, :]
        e_neg = jnp.exp(-cum)
        to_end = jnp.exp(cum_end - cum)
        kt_all.append(rows(kk, bi) * jnp.exp(cum - ld_b))
        bt_all.append(rows(bvec, bi) * e_neg)
        kn_all.append(rows(k, bi) * e_neg)
        rt_all.append(rows(r, bi) * jnp.exp(cum))
        bh_all.append(rows(bvec, bi) * to_end)
        kh_all.append(rows(k, bi) * to_end)
        g_end.append(jnp.exp(cum_end))
    v_all = [rows(vr, bi) for bi in range(n_batch)]

    heads = lambda xb: [x[:, h * HEAD_DIM:(h + 1) * HEAD_DIM] for x in xb for h in range(N_HEADS)]
    kt, bt, kn, rt, bh, kh, v, g_h = (heads(x) for x in (kt_all, bt_all, kn_all, rt_all, bh_all, kh_all, v_all,
                                                        g_end))
    hd = HEAD_DIM
    kt_rt = _each(lambda x, y: jnp.concatenate([x, y], axis=0), kt, rt)
    on_b = _each(_mm_nt, kt_rt, bt)
    on_k = _each(_mm_nt, kt_rt, kn)
    a_b = [jnp.where(lower_strict, x[:c], 0.0) for x in on_b]
    a_rb = [jnp.where(lower_incl, x[c:], 0.0) for x in on_b]
    a_k = [jnp.where(lower_strict, x[:c], 0.0) for x in on_k]
    a_rk = [jnp.where(lower_incl, x[c:], 0.0) for x in on_k]
    av = _each(lambda x, y, vv: _mm(jnp.concatenate([x, y], axis=0), vv), a_k, a_rk, v)
    akv = [x[:c] for x in av]
    arkv = [x[c:] for x in av]
    vkh = _each(_mm_tn, v, kh)
    attention_step()
    t_inv = _unit_lower_inverse(a_b, rowi, coli)
    attention_step()
    ku = _each(lambda t_, x, y: _mm(t_, jnp.concatenate([x, y], axis=1)), t_inv, kt, akv)
    ry = _each(_mm, a_rb, ku)
    r_hat = _each(lambda x, z: x - z[:, :hd], rt, ry)
    y_bar = _each(lambda x, z: x - z[:, hd:], arkv, ry)
    eh = _each(_mm_tn, ku, bh)
    attention_step()
    e_bar = [x[:hd] for x in eh]
    h_add = _each(lambda x, z: x - z[hd:], vkh, eh)
    s = [s_ref[h] for h in range(n_chains)]
    ys = _each(lambda rh, sh, yb: _mm_nt(rh, sh) + yb, r_hat, s, y_bar)
    s_new = _each(lambda sh, gh, eb, ha: sh * gh - _mm(sh, eb) + ha, s, g_h, e_bar, h_add)
    for h in range(n_chains):
        s_ref[h] = s_new[h]

    y = jnp.concatenate([jnp.concatenate(ys[bi * N_HEADS:(bi + 1) * N_HEADS], axis=-1) for bi in range(n_batch)],
                        axis=0)
    inv_n = 1.0 / HEAD_DIM
    mean = _split_dot(y, bd) * inv_n
    d = y - mean
    var = _split_dot(d * d, bd) * inv_n
    yn = d * lax.rsqrt(var + GN_EPS) * lnw_ref[...] + lnb_ref[...]
    bonus = _split_dot(r * k * rk_ref[...], bd) * vr
    out = (yn + bonus) * g
    for bi in range(n_batch):
        y_ref[bi] = out[bi * c:(bi + 1) * c].astype(y_ref.dtype)

    _sb_finish(units, q_block)


def _mixers(qkv3, rest3, v_first, mu_main, mu_lora, w0, a0, k_k, k_a, vres0, w_lora, bd, ln_w, ln_b, r_k):
    b, l, _ = rest3.shape
    has_vres = v_first is not None
    c = RW_CHUNK
    tq = SB_BLOCK
    n_q = l // tq
    n_units = b // 2
    assert b % 2 == 0 and l // c == 2 * n_q, "one query block of two batch elements per unit and chunk pair"
    tok = pl.BlockSpec((b, c, C_MIX), lambda ci: (0, ci, 0))
    row = lambda width: pl.BlockSpec((1, width), lambda ci: (0, 0))
    in_specs = [pl.BlockSpec((b, c, 3 * C_MIX), lambda ci: (0, ci, COL_RW // (3 * C_MIX))),
                pl.BlockSpec((b, c, C_MIX), lambda ci: (0, ci, COL_LORA // C_MIX))]
    args = [rest3, rest3]
    if has_vres:
        in_specs.append(tok)
        args.append(v_first)
    in_specs += [row(3 * C_MIX)] + [row(C_MIX)] * 5
    args += [mu_main, mu_lora, w0, a0, k_k, k_a]
    if has_vres:
        in_specs.append(row(C_MIX))
        args.append(vres0)
    in_specs += [pl.BlockSpec((C_MIX, 4 * C_MIX), lambda ci: (0, 0)), pl.BlockSpec((C_MIX, C_MIX), lambda ci: (0, 0)),
                 row(C_MIX), row(C_MIX), row(C_MIX)]
    args += [w_lora, bd, ln_w, ln_b, r_k]
    for j in range(n_units):
        which = lambda ci, j=j: 2 * j + ci // n_q
        in_specs += [pl.BlockSpec((1, tq, C_MIX), lambda ci, w=which: (w(ci), ci % n_q, 0)),
                     pl.BlockSpec((1, l, C_MIX), lambda ci, w=which: (w(ci), 0, 1), pipeline_mode=pl.Buffered(1)),
                     pl.BlockSpec((1, l, C_MIX), lambda ci, w=which: (w(ci), 0, 2), pipeline_mode=pl.Buffered(1))]
        args += [qkv3, qkv3, qkv3]
    in_specs.append(pl.BlockSpec((2 * tq, 2 * tq), lambda ci: (0, 0)))
    args.append(_later_and_ones(tq))
    out_shape = [jax.ShapeDtypeStruct((b, l, C_MIX), BF16)]
    out_specs = [tok]
    if not has_vres:
        out_shape.append(jax.ShapeDtypeStruct((b, l, C_MIX), F32))
        out_specs.append(tok)
    out_shape += [jax.ShapeDtypeStruct((2, l, C_MIX), BF16)] * n_units
    out_specs += [pl.BlockSpec((1, tq, C_MIX), lambda ci: (ci // n_q, ci % n_q, 0))] * n_units
    res = pl.pallas_call(
        functools.partial(_mixers_kernel, has_vres=has_vres, n_units=n_units, n_qblocks=n_q),
        grid=(l // c,),
        in_specs=in_specs,
        out_specs=out_specs,
        out_shape=out_shape,
        scratch_shapes=[pltpu.VMEM((b * N_HEADS, HEAD_DIM, HEAD_DIM), F32),
                        pltpu.VMEM((b, 8, 3 * C_MIX), F32), pltpu.VMEM((b, 8, C_MIX), F32)],
        compiler_params=_params("arbitrary"),
        name="mixers",
    )(*args)
    y_sb_units = [t.reshape(2 * l, C_MIX) for t in res[-n_units:]]
    return (y_sb_units, res[0], None) if has_vres else (y_sb_units, res[0], res[1])


def _merge_ffn_kernel(*refs, n_units, blocks_per_unit):
    h_ref, gsb_ref, grw_ref = refs[:3]
    ysb_refs = refs[3:3 + n_units]
    yrw_ref, wsb_ref, wrw_ref, wout_ref, g_ref, wg_ref, wu_ref, wo_ref, o_ref = refs[3 + n_units:]
    unit = pl.program_id(0) // blocks_per_unit
    y_sb = ysb_refs[0][...]
    for j in range(1, n_units):
        y_sb = jnp.where(unit == j, ysb_refs[j][...], y_sb)
    o_sb = jnp.dot(y_sb, wsb_ref[...], preferred_element_type=F32)
    o_rw = jnp.dot(yrw_ref[...], wrw_ref[...], preferred_element_type=F32)
    merged = _sigmoid(gsb_ref[...].astype(F32)) * o_sb + _sigmoid(grw_ref[...].astype(F32)) * o_rw
    x = h_ref[...] + jnp.dot(merged.astype(BF16), wout_ref[...], preferred_element_type=F32)
    hn = _rms_norm(x, g_ref[...]).astype(BF16)
    gate = jnp.dot(hn, wg_ref[...], preferred_element_type=F32)
    up = jnp.dot(hn, wu_ref[...], preferred_element_type=F32)
    act = gate * _sigmoid(gate) * up
    o_ref[...] = x + jnp.dot(act.astype(BF16), wo_ref[...], preferred_element_type=F32)


def _merge_ffn(h2, rest2, y_sb_units, y_rw, w_sb, w_rw, w_out, gain, w_ffn_in, w_ffn_out):
    m = h2.shape[0]
    n_units = len(y_sb_units)
    tm = _pick(m // n_units, (384, 256, 128))
    per_unit = m // n_units // tm
    rows = lambda width: pl.BlockSpec((tm, width), lambda i: (i, 0))
    full = lambda shape, col=0: pl.BlockSpec(shape, lambda i: (0, col), pipeline_mode=pl.Buffered(1))
    unit_rows = [pl.BlockSpec((tm, C_MIX), lambda i, j=j: (jnp.clip(i - j * per_unit, 0, per_unit - 1), 0))
                 for j in range(n_units)]
    return pl.pallas_call(
        functools.partial(_merge_ffn_kernel, n_units=n_units, blocks_per_unit=per_unit),
        grid=(m // tm,),
        in_specs=[rows(D_MODEL),
                  pl.BlockSpec((tm, D_MODEL), lambda i: (i, COL_GATES // D_MODEL)),
                  pl.BlockSpec((tm, D_MODEL), lambda i: (i, COL_GATES // D_MODEL + 1)),
                  *unit_rows, rows(C_MIX),
                  full((C_MIX, D_MODEL)), full((C_MIX, D_MODEL)), full((D_MODEL, D_MODEL)),
                  pl.BlockSpec((1, D_MODEL), lambda i: (0, 0)),
                  full((D_MODEL, FFN_HIDDEN)), full((D_MODEL, FFN_HIDDEN), 1), full((FFN_HIDDEN, D_MODEL))],
        out_specs=rows(D_MODEL),
        out_shape=jax.ShapeDtypeStruct((m, D_MODEL), F32),
        compiler_params=_params("arbitrary"),
        name="merge_ffn",
    )(h2, rest2, rest2, *y_sb_units, y_rw, w_sb, w_rw, w_out, gain.reshape(1, D_MODEL),
      w_ffn_in, w_ffn_in, w_ffn_out)


def _final_norm_kernel(h_ref, g_ref, o_ref):
    o_ref[...] = _rms_norm(h_ref[...], g_ref[...])


def _final_norm(h3, gain, s):
    b = h3.shape[0]
    tm = _pick(s, (512, 256, 128, 16))
    return pl.pallas_call(
        _final_norm_kernel,
        grid=(b, s // tm),
        in_specs=[pl.BlockSpec((pl.Element(1), pl.Element(tm), pl.Element(D_MODEL)),
                               lambda bi, i: (bi, pl.multiple_of(i * tm + N_META, N_META), 0)),
                  pl.BlockSpec((1, 1, D_MODEL), lambda bi, i: (0, 0, 0))],
        out_specs=pl.BlockSpec((1, tm, D_MODEL), lambda bi, i: (bi, i, 0)),
        out_shape=jax.ShapeDtypeStruct((b, s, D_MODEL), F32),
        compiler_params=_params("arbitrary", "arbitrary"),
        name="final_norm",
    )(h3, gain.reshape(1, 1, D_MODEL))


def _head_block_diag():
    idx = jnp.arange(C_MIX) // HEAD_DIM
    return (idx[:, None] == idx[None, :]).astype(BF16)


def _lora_weight(w_up, a_up, g_up, vres_up):
    w = jnp.zeros((C_MIX, 4 * C_MIX), F32)
    o = 0
    for seg, (mat, width) in enumerate(((w_up, W_LORA), (a_up, A_LORA), (g_up, G_LORA), (vres_up, V_LORA))):
        if mat is not None:
            w = w.at[o:o + width, seg * C_MIX:(seg + 1) * C_MIX].set(mat)
        o += width
    return w.astype(BF16)


def kernel(x, meta_tokens, norm_mix, norm_ffn, norm_final, w_in, mu_rw, w0, w_up, a0, a_up, g_up, k_k, k_a, r_k, ln_x_w, ln_x_b, vres_down, vres_mu, vres_up, vres0, w_sb_out, w_rw_out, w_out, w_ffn_in, w_ffn_out):
    b, s, d = x.shape
    depth = w_in.shape[0]
    l_real = N_META + s
    l_pad = -(-l_real // SB_BLOCK) * SB_BLOCK
    meta = jnp.broadcast_to(meta_tokens.astype(x.dtype)[None], (b, N_META, d))
    h = jnp.concatenate([meta, x, jnp.zeros((b, l_pad - l_real, d), x.dtype)], axis=1)
    h2 = h.reshape(b * l_pad, d)
    bd = _head_block_diag()
    n_in = 3 * C_MIX + RW_COLS
    row = lambda vec: vec.reshape(1, -1)

    v_first = None
    for layer in range(depth):
        wl = w_in[layer]
        vdown = vres_down[layer - 1] if layer > 0 else jnp.zeros((d, V_LORA), F32)
        w_cat = jnp.concatenate([wl[:, :n_in], vdown, jnp.zeros((d, C_MIX - LORA_USED), F32), wl[:, n_in:]],
                                axis=1).astype(BF16)
        vmu = vres_mu[layer - 1] if layer > 0 else jnp.zeros((V_LORA,), F32)
        mu_main = row(mu_rw[layer, :3 * C_MIX])
        mu_lora = row(jnp.concatenate([mu_rw[layer, 3 * C_MIX:], vmu, jnp.zeros((C_MIX - LORA_USED,), F32)]))
        w_lora = _lora_weight(w_up[layer], a_up[layer], g_up[layer], vres_up[layer - 1] if layer > 0 else None)

        qkv2, rest2 = _norm_proj(h2, norm_mix[layer], w_cat)
        y_sb, y_rw, v_out = _mixers(
            qkv2.reshape(b, l_pad, N_QKV), rest2.reshape(b, l_pad, N_REST), v_first, mu_main, mu_lora,
            row(w0[layer]), row(a0[layer]),
            row(k_k[layer]), row(k_a[layer]), row(vres0[layer - 1]) if layer > 0 else None, w_lora, bd,
            row(ln_x_w[layer]), row(ln_x_b[layer]), row(r_k[layer].reshape(-1)))
        if layer == 0:
            v_first = v_out
        flat = lambda t: t.reshape(b * l_pad, C_MIX)
        h2 = _merge_ffn(h2, rest2, y_sb, flat(y_rw),
                        w_sb_out[layer].astype(BF16), w_rw_out[layer].astype(BF16), w_out[layer].astype(BF16),
                        norm_ffn[layer], w_ffn_in[layer].astype(BF16), w_ffn_out[layer].astype(BF16))

    return _final_norm(h2.reshape(b, l_pad, d), norm_final, s)
```

```python
import functools

import jax
import jax.numpy as jnp
from jax import lax
from jax.experimental import pallas as pl
from jax.experimental.pallas import tpu as pltpu

D_MODEL = 1024
HEAD_DIM = 64
N_HEADS = 8
C_MIX = N_HEADS * HEAD_DIM
N_META = 16
SB_BLOCK = 128
W_LORA, A_LORA, V_LORA, G_LORA = 64, 64, 32, 160
RW_COLS = 3 * C_MIX + W_LORA + A_LORA + G_LORA
FFN_HIDDEN = 2816
RMS_EPS = 1e-6
GN_EPS = 64e-5
RW_CHUNK = 64
SB_UNDERFLOW = 104.0
SB_EAGER_BLOCKS = 2

N_QKV = 3 * C_MIX
COL_RW = 0
COL_LORA = 1536
COL_GATES = 2048
N_REST = 4096
LORA_USED = W_LORA + A_LORA + G_LORA + V_LORA
LORA_W = 384
assert W_LORA + A_LORA == 128 and 256 <= W_LORA + A_LORA + G_LORA and LORA_USED <= LORA_W

V7X_VMEM_LIMIT = 56 * 1024 * 1024

F32 = jnp.float32
BF16 = jnp.bfloat16


def _pick(n, cands):
    for c in cands:
        if n % c == 0:
            return c
    raise ValueError(f"no tile for {n} in {cands}")


def _params(*sem):
    return pltpu.CompilerParams(dimension_semantics=sem, vmem_limit_bytes=V7X_VMEM_LIMIT)


def _mm(a, b):
    return jnp.dot(a.astype(BF16), b.astype(BF16), preferred_element_type=F32)


def _mm_nt(a, b):
    return lax.dot_general(a.astype(BF16), b.astype(BF16), (((1,), (1,)), ((), ())),
                           preferred_element_type=F32)


def _mm_tn(a, b):
    return lax.dot_general(a.astype(BF16), b.astype(BF16), (((0,), (0,)), ((), ())),
                           preferred_element_type=F32)


def _each(fn, *lists):
    return [fn(*xs) for xs in zip(*lists)]


def _split_dot(x, w_bf16):
    hi = x.astype(BF16)
    lo = (x - hi.astype(F32)).astype(BF16)
    return (jnp.dot(hi, w_bf16, preferred_element_type=F32)
            + jnp.dot(lo, w_bf16, preferred_element_type=F32))


def _prefix_dot(tri_bf16, x):
    hi = x.astype(BF16)
    rem = x - hi.astype(F32)
    mid = rem.astype(BF16)
    lo = (rem - mid.astype(F32)).astype(BF16)
    return (jnp.dot(tri_bf16, hi, preferred_element_type=F32) + jnp.dot(tri_bf16, mid, preferred_element_type=F32)
            + jnp.dot(tri_bf16, lo, preferred_element_type=F32))


def _softplus(u):
    return jnp.maximum(u, 0.0) + jnp.log(1.0 + jnp.exp(-jnp.abs(u)))


def _sigmoid(u):
    return 1.0 / (1.0 + jnp.exp(-u))


def _rms_norm(x, gain):
    ms = jnp.mean(x * x, axis=-1, keepdims=True)
    return x * lax.rsqrt(ms + RMS_EPS) * gain


def _norm_proj_kernel(h_ref, g_ref, w_ref, qkv_ref, rest_ref):
    xn = _rms_norm(h_ref[...], g_ref[...]).astype(BF16)
    qkv_ref[...] = jnp.dot(xn, w_ref[:, :N_QKV], preferred_element_type=F32).astype(BF16)
    rest_ref[...] = jnp.dot(xn, w_ref[:, N_QKV:], preferred_element_type=F32).astype(BF16)


def _norm_proj(h2, gain, w_cat_all, layer):
    m = h2.shape[0]
    n = w_cat_all.shape[2]
    tm = _pick(m, (384, 256, 128))
    return pl.pallas_call(
        _norm_proj_kernel,
        grid=(m // tm,),
        in_specs=[pl.BlockSpec((tm, D_MODEL), lambda i: (i, 0)),
                  pl.BlockSpec((1, D_MODEL), lambda i: (0, 0)),
                  pl.BlockSpec((None, D_MODEL, n), lambda i: (layer, 0, 0))],
        out_specs=[pl.BlockSpec((tm, N_QKV), lambda i: (i, 0)),
                   pl.BlockSpec((tm, n - N_QKV), lambda i: (i, 0))],
        out_shape=[jax.ShapeDtypeStruct((m, N_QKV), BF16), jax.ShapeDtypeStruct((m, n - N_QKV), BF16)],
        compiler_params=_params("arbitrary"),
        name="norm_proj",
    )(h2, gain.reshape(1, D_MODEL), w_cat_all)


def _sb_unit(q_ref, k_ref, v_ref, lm, o_ref, qi, tq):
    n_pairs = N_HEADS // 2
    pw = 2 * HEAD_DIM
    lane = lax.broadcasted_iota(jnp.int32, (1, pw), 1)
    head_a = lane < HEAD_DIM
    zero_bf = jnp.zeros((), BF16)
    split_rows = lambda x: jnp.concatenate([jnp.where(head_a, x, zero_bf), jnp.where(head_a, zero_bf, x)], axis=0)
    q = q_ref[0] * jnp.asarray(HEAD_DIM ** -0.5, BF16)
    q2 = [split_rows(q[:, p * pw:(p + 1) * pw]) for p in range(n_pairs)]
    row = lax.broadcasted_iota(jnp.int32, (2 * tq, tq), 0)
    col = lax.broadcasted_iota(jnp.int32, (2 * tq, tq), 1)
    causal2 = col < jnp.where(row >= tq, row - tq, row)
    state = {}

    def visit_steps(blocks, r_run, acc):
        chains = [(bi, p) for bi in range(len(blocks)) for p in range(n_pairs)]
        starts = [pl.multiple_of(blk[0] * tq, tq) for blk in blocks]
        ks = [k_ref[0, pl.ds(st, tq), :] for st in starts]
        vs = [v_ref[0, pl.ds(st, tq), :] for st in starts]
        vs = [v if blk[2] is None else jnp.where(blk[2], v, zero_bf) for v, blk in zip(vs, blocks)]
        z = [lax.dot_general(q2[p], ks[bi][:, p * pw:(p + 1) * pw], (((1,), (1,)), ((), ())),
                             preferred_element_type=F32) for bi, p in chains]
        yield
        sp = [_softplus(zi) for zi in z]
        sp = [jnp.where(causal2, s, 0.0) if blocks[bi][1] else s for s, (bi, p) in zip(sp, chains)]
        hi = [s.astype(BF16) for s in sp]
        lo = [(s - h.astype(F32)).astype(BF16) for s, h in zip(sp, hi)]
        wm = [jnp.dot(jnp.concatenate([h, l_], axis=1), lm, preferred_element_type=F32) for h, l_ in zip(hi, lo)]
        yield
        r_run, acc = list(r_run), list(acc)
        for i, (bi, p) in enumerate(chains):
            a = jnp.exp(z[i] - sp[i] - wm[i][:, :tq] - r_run[p])
            if blocks[bi][1]:
                a = jnp.where(causal2, a, 0.0)
            a = a.astype(BF16)
            v2 = split_rows(vs[bi][:, p * pw:(p + 1) * pw])
            acc[p] = acc[p] + jnp.dot(jnp.concatenate([a[:tq], a[tq:]], axis=1), v2, preferred_element_type=F32)
            r_run[p] = r_run[p] + wm[i][:, tq:]
        state["r_run"], state["acc"] = r_run, acc

    eager_blocks = [(qi, True, None)] + [(jnp.maximum(qi - back, 0), False, qi >= back)
                                         for back in range(1, SB_EAGER_BLOCKS + 1)]
    eager = visit_steps(eager_blocks, [jnp.zeros((2 * tq, tq), F32)] * n_pairs,
                        [jnp.zeros((tq, pw), F32)] * n_pairs)

    def visit(blocks, r_run, acc):
        for _ in visit_steps(blocks, r_run, acc):
            pass
        return state["r_run"], state["acc"]

    def store(acc):
        o_ref[0] = jnp.concatenate(acc, axis=1).astype(o_ref.dtype)

    return eager, state, visit, store


def _sb_finish(units, qi):
    def more(c):
        j, rrs, _ = c
        nearest = jnp.min(functools.reduce(jnp.minimum, [r[:, :1] for rr in rrs for r in rr]))
        return jnp.logical_and(j >= 0, nearest < SB_UNDERFLOW)

    def far(c):
        j, rrs, acs = c
        new = [visit([(j, False, None)], rr, ac) for (_, _, visit, _), rr, ac in zip(units, rrs, acs)]
        return j - 1, [n[0] for n in new], [n[1] for n in new]

    start = (qi - 1 - SB_EAGER_BLOCKS, [u[1]["r_run"] for u in units], [u[1]["acc"] for u in units])
    _, _, acs = lax.while_loop(more, far, start)
    for (_, _, _, store), acc in zip(units, acs):
        store(acc)


def _later_and_ones(tq):
    s_from = jnp.arange(2 * tq)[:, None] % tq
    s_to = jnp.arange(2 * tq)[None, :]
    return jnp.where(s_to < tq, s_from > s_to, True).astype(BF16)


def _unit_lower_inverse(a_strict, rowi, coli):
    eye = (rowi == coli).astype(F32)
    same = lambda sh: (rowi >> sh) == (coli >> sh)
    size = a_strict[0].shape[0]
    levels = [same(sh) for sh in range(3, size.bit_length() - 1)] + [None]
    m8 = levels[0]
    n1 = [-jnp.where(m8, a, 0.0) for a in a_strict]
    n2 = _each(_mm, n1, n1)
    n4 = _each(_mm, n2, n2)
    t = [eye + n for n in n1]
    t = _each(lambda ti, ni: ti + _mm(ti, ni), t, n2)
    t = _each(lambda ti, ni: ti + _mm(ti, ni), t, n4)
    for inner, outer in zip(levels[:-1], levels[1:]):
        off = jnp.logical_not(inner) if outer is None else jnp.logical_and(outer, jnp.logical_not(inner))
        ta = _each(lambda ti, a: _mm(ti, jnp.where(off, a, 0.0)), t, a_strict)
        t = _each(lambda ti, tai: ti - _mm(tai, ti), t, ta)
    return t


def _mixers_kernel(*refs, has_vres, n_units, n_qblocks):
    refs = list(refs)
    n_rw_in = 15 if has_vres else 13
    rw_in, refs = refs[:n_rw_in], refs[n_rw_in:]
    sb_in, refs = refs[:3 * n_units + 1], refs[3 * n_units + 1:]
    if has_vres:
        (main_ref, lora_ref, vfirst_ref, mu_main_ref, mu_lora_ref, w0_ref, a0_ref, kk_ref, ka_ref, vres0_ref,
         wl_ref, bd_ref, lnw_ref, lnb_ref, rk_ref) = rw_in
        y_ref, vout_ref, refs = refs[0], None, refs[1:]
    else:
        (main_ref, lora_ref, mu_main_ref, mu_lora_ref, w0_ref, a0_ref, kk_ref, ka_ref,
         wl_ref, bd_ref, lnw_ref, lnb_ref, rk_ref) = rw_in
        y_ref, vout_ref, refs = refs[0], refs[1], refs[2:]
    ysb_refs, (s_ref, pm_ref, pl_ref) = refs[:n_units], refs[n_units:]
    c = RW_CHUNK
    n_batch = main_ref.shape[0]
    n_chains = n_batch * N_HEADS

    @pl.when(pl.program_id(0) == 0)
    def _():
        s_ref[...] = jnp.zeros_like(s_ref)
        pm_ref[...] = jnp.zeros_like(pm_ref)
        pl_ref[...] = jnp.zeros_like(pl_ref)

    lm = sb_in[-1][...]
    q_block = pl.program_id(0) % n_qblocks
    units = [_sb_unit(sb_in[3 * j], sb_in[3 * j + 1], sb_in[3 * j + 2], lm, ysb_refs[j], q_block, SB_BLOCK)
             for j in range(n_units)]

    def attention_step():
        for unit in units:
            next(unit[0], None)

    first_row = lax.broadcasted_iota(jnp.int32, (c, 1), 0) == 0

    def shifted(x, prev8, mu):
        prev = jnp.where(first_row, prev8[7:8, :], pltpu.roll(x, shift=1, axis=0))
        return x + (prev - x) * mu

    mains = [main_ref[bi].astype(F32) for bi in range(n_batch)]
    loras = [lora_ref[bi].astype(F32) for bi in range(n_batch)]
    xs = jnp.concatenate([shifted(x, pm_ref[bi], mu_main_ref[...]) for bi, x in enumerate(mains)], axis=0)
    lo = jnp.concatenate([shifted(x, pl_ref[bi], mu_lora_ref[...]) for bi, x in enumerate(loras)], axis=0)
    for bi in range(n_batch):
        pm_ref[bi] = mains[bi][c - 8:c, :]
        pl_ref[bi] = loras[bi][c - 8:c, :]

    r = xs[:, 0:C_MIX]
    kr = xs[:, C_MIX:2 * C_MIX]
    vr = xs[:, 2 * C_MIX:3 * C_MIX]
    lane = lax.broadcasted_iota(jnp.int32, (1, 128), 1)
    g_tail = W_LORA + A_LORA + G_LORA - 256
    act = jnp.concatenate([jnp.where(lane < W_LORA, jnp.tanh(lo[:, 0:128]), lo[:, 0:128]),
                           _sigmoid(lo[:, 128:256]),
                           jnp.where(lane < g_tail, _sigmoid(lo[:, 256:384]), lo[:, 256:384])], axis=1)
    pre = jnp.dot(act.astype(BF16), wl_ref[...], preferred_element_type=F32)
    w_log = -_softplus(-(w0_ref[...] + pre[:, 0:C_MIX])) - 0.5
    ld = -jnp.exp(w_log)
    a = _sigmoid(a0_ref[...] + pre[:, C_MIX:2 * C_MIX])
    g = pre[:, 2 * C_MIX:3 * C_MIX]
    if has_vres:
        v_first = jnp.concatenate([vfirst_ref[bi] for bi in range(n_batch)], axis=0)
        vr = vr + (v_first - vr) * _sigmoid(vres0_ref[...] + pre[:, 3 * C_MIX:4 * C_MIX])
    else:
        for bi in range(n_batch):
            vout_ref[bi] = vr[bi * c:(bi + 1) * c]
    bd = bd_ref[...]
    kk = kr * kk_ref[...]
    kk = kk * lax.rsqrt(jnp.maximum(_split_dot(kk * kk, bd), 1e-24))
    k = kr * (1.0 + (a - 1.0) * ka_ref[...])
    bvec = kk * a

    rowi = lax.broadcasted_iota(jnp.int32, (c, c), 0)
    coli = lax.broadcasted_iota(jnp.int32, (c, c), 1)
    lower_incl = coli <= rowi
    lower_strict = coli < rowi
    tri = lower_incl.astype(BF16)
    rows = lambda x, bi: x[bi * c:(bi + 1) * c]
    kt_all, bt_all, kn_all, rt_all, bh_all, kh_all, g_end = [], [], [], [], [], [], []
    for bi in range(n_batch):
        ld_b = rows(ld, bi)
        cum = _prefix_dot(tri, ld_b)
        cum_end = cum[c - 1:c, :]
        e_neg = jnp.exp(-cum)
        to_end = jnp.exp(cum_end - cum)
        kt_all.append(rows(kk, bi) * jnp.exp(cum - ld_b))
        bt_all.append(rows(bvec, bi) * e_neg)
        kn_all.append(rows(k, bi) * e_neg)
        rt_all.append(rows(r, bi) * jnp.exp(cum))
        bh_all.append(rows(bvec, bi) * to_end)
        kh_all.append(rows(k, bi) * to_end)
        g_end.append(jnp.exp(cum_end))
    v_all = [rows(vr, bi) for bi in range(n_batch)]

    heads = lambda xb: [x[:, h * HEAD_DIM:(h + 1) * HEAD_DIM] for x in xb for h in range(N_HEADS)]
    kt, bt, kn, rt, bh, kh, v, g_h = (heads(x) for x in (kt_all, bt_all, kn_all, rt_all, bh_all, kh_all, v_all,
                                                        g_end))
    hd = HEAD_DIM
    kt_rt = _each(lambda x, y: jnp.concatenate([x, y], axis=0), kt, rt)
    on_b = _each(_mm_nt, kt_rt, bt)
    on_k = _each(_mm_nt, kt_rt, kn)
    a_b = [jnp.where(lower_strict, x[:c], 0.0) for x in on_b]
    a_rb = [jnp.where(lower_incl, x[c:], 0.0) for x in on_b]
    a_k = [jnp.where(lower_strict, x[:c], 0.0) for x in on_k]
    a_rk = [jnp.where(lower_incl, x[c:], 0.0) for x in on_k]
    av = _each(lambda x, y, vv: _mm(jnp.concatenate([x, y], axis=0), vv), a_k, a_rk, v)
    akv = [x[:c] for x in av]
    arkv = [x[c:] for x in av]
    vkh = _each(_mm_tn, v, kh)
    attention_step()
    t_inv = _unit_lower_inverse(a_b, rowi, coli)
    attention_step()
    ku = _each(lambda t_, x, y: _mm(t_, jnp.concatenate([x, y], axis=1)), t_inv, kt, akv)
    ry = _each(_mm, a_rb, ku)
    r_hat = _each(lambda x, z: x - z[:, :hd], rt, ry)
    y_bar = _each(lambda x, z: x - z[:, hd:], arkv, ry)
    eh = _each(_mm_tn, ku, bh)
    attention_step()
    e_bar = [x[:hd] for x in eh]
    h_add = _each(lambda x, z: x - z[hd:], vkh, eh)
    s = [s_ref[h] for h in range(n_chains)]
    ys = _each(lambda rh, sh, yb: _mm_nt(rh, sh) + yb, r_hat, s, y_bar)
    s_new = _each(lambda sh, gh, eb, ha: sh * gh - _mm(sh, eb) + ha, s, g_h, e_bar, h_add)
    for h in range(n_chains):
        s_ref[h] = s_new[h]

    y = jnp.concatenate([jnp.concatenate(ys[bi * N_HEADS:(bi + 1) * N_HEADS], axis=-1) for bi in range(n_batch)],
                        axis=0)
    inv_n = 1.0 / HEAD_DIM
    mean = _split_dot(y, bd) * inv_n
    d = y - mean
    var = _split_dot(d * d, bd) * inv_n
    yn = d * lax.rsqrt(var + GN_EPS) * lnw_ref[...] + lnb_ref[...]
    bonus = _split_dot(r * k * rk_ref[...], bd) * vr
    out = (yn + bonus) * g
    for bi in range(n_batch):
        y_ref[bi] = out[bi * c:(bi + 1) * c].astype(y_ref.dtype)

    _sb_finish(units, q_block)


def _mixers(qkv3, rest3, v_first, mu_main, mu_lora, w0, a0, k_k, k_a, vres0, w_lora, bd, ln_w, ln_b, r_k):
    b, l, _ = rest3.shape
    has_vres = v_first is not None
    c = RW_CHUNK
    tq = SB_BLOCK
    n_q = l // tq
    n_units = b // 2
    assert b % 2 == 0 and l // c == 2 * n_q, "one query block of two batch elements per unit and chunk pair"
    tok = pl.BlockSpec((b, c, C_MIX), lambda ci: (0, ci, 0))
    row = lambda width: pl.BlockSpec((1, width), lambda ci: (0, 0))
    in_specs = [pl.BlockSpec((b, c, 3 * C_MIX), lambda ci: (0, ci, COL_RW // (3 * C_MIX))),
                pl.BlockSpec((b, c, LORA_W), lambda ci: (0, ci, COL_LORA // LORA_W))]
    args = [rest3, rest3]
    if has_vres:
        in_specs.append(tok)
        args.append(v_first)
    in_specs += [row(3 * C_MIX), row(LORA_W)] + [row(C_MIX)] * 4
    args += [mu_main, mu_lora, w0, a0, k_k, k_a]
    if has_vres:
        in_specs.append(row(C_MIX))
        args.append(vres0)
    in_specs += [pl.BlockSpec((LORA_W, 4 * C_MIX), lambda ci: (0, 0)), pl.BlockSpec((C_MIX, C_MIX), lambda ci: (0, 0)),
                 row(C_MIX), row(C_MIX), row(C_MIX)]
    args += [w_lora, bd, ln_w, ln_b, r_k]
    for j in range(n_units):
        which = lambda ci, j=j: 2 * j + ci // n_q
        in_specs += [pl.BlockSpec((1, tq, C_MIX), lambda ci, w=which: (w(ci), ci % n_q, 0)),
                     pl.BlockSpec((1, l, C_MIX), lambda ci, w=which: (w(ci), 0, 1), pipeline_mode=pl.Buffered(1)),
                     pl.BlockSpec((1, l, C_MIX), lambda ci, w=which: (w(ci), 0, 2), pipeline_mode=pl.Buffered(1))]
        args += [qkv3, qkv3, qkv3]
    in_specs.append(pl.BlockSpec((2 * tq, 2 * tq), lambda ci: (0, 0)))
    args.append(_later_and_ones(tq))
    out_shape = [jax.ShapeDtypeStruct((b, l, C_MIX), BF16)]
    out_specs = [tok]
    if not has_vres:
        out_shape.append(jax.ShapeDtypeStruct((b, l, C_MIX), F32))
        out_specs.append(tok)
    out_shape += [jax.ShapeDtypeStruct((2, l, C_MIX), BF16)] * n_units
    out_specs += [pl.BlockSpec((1, tq, C_MIX), lambda ci: (ci // n_q, ci % n_q, 0))] * n_units
    res = pl.pallas_call(
        functools.partial(_mixers_kernel, has_vres=has_vres, n_units=n_units, n_qblocks=n_q),
        grid=(l // c,),
        in_specs=in_specs,
        out_specs=out_specs,
        out_shape=out_shape,
        scratch_shapes=[pltpu.VMEM((b * N_HEADS, HEAD_DIM, HEAD_DIM), F32),
                        pltpu.VMEM((b, 8, 3 * C_MIX), F32), pltpu.VMEM((b, 8, LORA_W), F32)],
        compiler_params=_params("arbitrary"),
        name="mixers",
    )(*args)
    y_sb_units = [t.reshape(2 * l, C_MIX) for t in res[-n_units:]]
    return (y_sb_units, res[0], None) if has_vres else (y_sb_units, res[0], res[1])


def _merge_ffn_kernel(*refs, n_units, blocks_per_unit):
    h_ref, gsb_ref, grw_ref = refs[:3]
    ysb_refs = refs[3:3 + n_units]
    yrw_ref, wsb_ref, wrw_ref, wout_ref, g_ref, wg_ref, wu_ref, wo_ref, o_ref = refs[3 + n_units:]
    unit = pl.program_id(0) // blocks_per_unit
    y_sb = ysb_refs[0][...]
    for j in range(1, n_units):
        y_sb = jnp.where(unit == j, ysb_refs[j][...], y_sb)
    o_sb = jnp.dot(y_sb, wsb_ref[...], preferred_element_type=F32)
    o_rw = jnp.dot(yrw_ref[...], wrw_ref[...], preferred_element_type=F32)
    merged = _sigmoid(gsb_ref[...].astype(F32)) * o_sb + _sigmoid(grw_ref[...].astype(F32)) * o_rw
    x = h_ref[...] + jnp.dot(merged.astype(BF16), wout_ref[...], preferred_element_type=F32)
    hn = _rms_norm(x, g_ref[...]).astype(BF16)
    gate = jnp.dot(hn, wg_ref[...], preferred_element_type=F32)
    up = jnp.dot(hn, wu_ref[...], preferred_element_type=F32)
    act = gate * _sigmoid(gate) * up
    o_ref[...] = x + jnp.dot(act.astype(BF16), wo_ref[...], preferred_element_type=F32)


def _merge_ffn(h2, rest2, y_sb_units, y_rw, w_sb, w_rw, w_out, gain, w_ffn_in, w_ffn_out, layer):
    m = h2.shape[0]
    n_units = len(y_sb_units)
    tm = _pick(m // n_units, (384, 256, 128))
    per_unit = m // n_units // tm
    rows = lambda width: pl.BlockSpec((tm, width), lambda i: (i, 0))
    full = lambda shape, col=0: pl.BlockSpec((None,) + shape, lambda i: (layer, 0, col),
                                             pipeline_mode=pl.Buffered(1))
    unit_rows = [pl.BlockSpec((tm, C_MIX), lambda i, j=j: (jnp.clip(i - j * per_unit, 0, per_unit - 1), 0))
                 for j in range(n_units)]
    return pl.pallas_call(
        functools.partial(_merge_ffn_kernel, n_units=n_units, blocks_per_unit=per_unit),
        grid=(m // tm,),
        in_specs=[rows(D_MODEL),
                  pl.BlockSpec((tm, D_MODEL), lambda i: (i, COL_GATES // D_MODEL)),
                  pl.BlockSpec((tm, D_MODEL), lambda i: (i, COL_GATES // D_MODEL + 1)),
                  *unit_rows, rows(C_MIX),
                  full((C_MIX, D_MODEL)), full((C_MIX, D_MODEL)), full((D_MODEL, D_MODEL)),
                  pl.BlockSpec((1, D_MODEL), lambda i: (0, 0)),
                  full((D_MODEL, FFN_HIDDEN)), full((D_MODEL, FFN_HIDDEN), 1), full((FFN_HIDDEN, D_MODEL))],
        out_specs=rows(D_MODEL),
        out_shape=jax.ShapeDtypeStruct((m, D_MODEL), F32),
        compiler_params=_params("arbitrary"),
        name="merge_ffn",
    )(h2, rest2, rest2, *y_sb_units, y_rw, w_sb, w_rw, w_out, gain.reshape(1, D_MODEL),
      w_ffn_in, w_ffn_in, w_ffn_out)


def _final_norm_kernel(h_ref, g_ref, o_ref):
    o_ref[...] = _rms_norm(h_ref[...], g_ref[...])


def _final_norm(h3, gain, s):
    b = h3.shape[0]
    tm = _pick(s, (512, 256, 128, 16))
    return pl.pallas_call(
        _final_norm_kernel,
        grid=(b, s // tm),
        in_specs=[pl.BlockSpec((pl.Element(1), pl.Element(tm), pl.Element(D_MODEL)),
                               lambda bi, i: (bi, pl.multiple_of(i * tm + N_META, N_META), 0)),
                  pl.BlockSpec((1, 1, D_MODEL), lambda bi, i: (0, 0, 0))],
        out_specs=pl.BlockSpec((1, tm, D_MODEL), lambda bi, i: (bi, i, 0)),
        out_shape=jax.ShapeDtypeStruct((b, s, D_MODEL), F32),
        compiler_params=_params("arbitrary", "arbitrary"),
        name="final_norm",
    )(h3, gain.reshape(1, 1, D_MODEL))


def _head_block_diag():
    idx = jnp.arange(C_MIX) // HEAD_DIM
    return (idx[:, None] == idx[None, :]).astype(BF16)


def _lora_weight(w_up, a_up, g_up, vres_up):
    w = jnp.zeros((LORA_W, 4 * C_MIX), F32)
    o = 0
    for seg, (mat, width) in enumerate(((w_up, W_LORA), (a_up, A_LORA), (g_up, G_LORA), (vres_up, V_LORA))):
        if mat is not None:
            w = w.at[o:o + width, seg * C_MIX:(seg + 1) * C_MIX].set(mat)
        o += width
    return w.astype(BF16)


def kernel(x, meta_tokens, norm_mix, norm_ffn, norm_final, w_in, mu_rw, w0, w_up, a0, a_up, g_up, k_k, k_a, r_k, ln_x_w, ln_x_b, vres_down, vres_mu, vres_up, vres0, w_sb_out, w_rw_out, w_out, w_ffn_in, w_ffn_out):
    b, s, d = x.shape
    depth = w_in.shape[0]
    l_real = N_META + s
    l_pad = -(-l_real // SB_BLOCK) * SB_BLOCK
    meta = jnp.broadcast_to(meta_tokens.astype(x.dtype)[None], (b, N_META, d))
    h = jnp.concatenate([meta, x, jnp.zeros((b, l_pad - l_real, d), x.dtype)], axis=1)
    h2 = h.reshape(b * l_pad, d)
    bd = _head_block_diag()
    n_in = 3 * C_MIX + RW_COLS
    row = lambda vec: vec.reshape(1, -1)

    vdown_all = jnp.concatenate([jnp.zeros((1, d, V_LORA), F32), vres_down], axis=0)
    w_cat_all = jnp.concatenate([w_in[:, :, :n_in], vdown_all, jnp.zeros((depth, d, C_MIX - LORA_USED), F32),
                                 w_in[:, :, n_in:]], axis=2).astype(BF16)
    w_sb_all, w_rw_all, w_out_all = (w.astype(BF16) for w in (w_sb_out, w_rw_out, w_out))
    w_ffn_in_all, w_ffn_out_all = w_ffn_in.astype(BF16), w_ffn_out.astype(BF16)

    v_first = None
    for layer in range(depth):
        vmu = vres_mu[layer - 1] if layer > 0 else jnp.zeros((V_LORA,), F32)
        mu_main = row(mu_rw[layer, :3 * C_MIX])
        mu_lora = row(jnp.concatenate([mu_rw[layer, 3 * C_MIX:], vmu, jnp.zeros((LORA_W - LORA_USED,), F32)]))
        w_lora = _lora_weight(w_up[layer], a_up[layer], g_up[layer], vres_up[layer - 1] if layer > 0 else None)

        qkv2, rest2 = _norm_proj(h2, norm_mix[layer], w_cat_all, layer)
        y_sb, y_rw, v_out = _mixers(
            qkv2.reshape(b, l_pad, N_QKV), rest2.reshape(b, l_pad, N_REST), v_first, mu_main, mu_lora,
            row(w0[layer]), row(a0[layer]),
            row(k_k[layer]), row(k_a[layer]), row(vres0[layer - 1]) if layer > 0 else None, w_lora, bd,
            row(ln_x_w[layer]), row(ln_x_b[layer]), row(r_k[layer].reshape(-1)))
        if layer == 0:
            v_first = v_out
        flat = lambda t: t.reshape(b * l_pad, C_MIX)
        h2 = _merge_ffn(h2, rest2, y_sb, flat(y_rw), w_sb_all, w_rw_all, w_out_all,
                        norm_ffn[layer], w_ffn_in_all, w_ffn_out_all, layer)

    return _final_norm(h2.reshape(b, l_pad, d), norm_final, s)
```

```python
import functools

import jax
import jax.numpy as jnp
from jax import lax
from jax.experimental import pallas as pl
from jax.experimental.pallas import tpu as pltpu

D_MODEL = 1024
HEAD_DIM = 64
N_HEADS = 8
C_MIX = N_HEADS * HEAD_DIM
N_META = 16
SB_BLOCK = 128
W_LORA, A_LORA, V_LORA, G_LORA = 64, 64, 32, 160
RW_COLS = 3 * C_MIX + W_LORA + A_LORA + G_LORA
FFN_HIDDEN = 2816
RMS_EPS = 1e-6
GN_EPS = 64e-5
RW_CHUNK = 64
SB_UNDERFLOW = 104.0
SB_EAGER_BLOCKS = 2

N_QKV = 3 * C_MIX
COL_RW = 0
COL_LORA = 1536
COL_GATES = 2048
N_REST = 4096
LORA_USED = W_LORA + A_LORA + G_LORA + V_LORA
LORA_W = 384
assert W_LORA + A_LORA == 128 and 256 <= W_LORA + A_LORA + G_LORA and LORA_USED <= LORA_W

V7X_VMEM_LIMIT = 56 * 1024 * 1024

F32 = jnp.float32
BF16 = jnp.bfloat16


def _pick(n, cands):
    for c in cands:
        if n % c == 0:
            return c
    raise ValueError(f"no tile for {n} in {cands}")


def _params(*sem):
    return pltpu.CompilerParams(dimension_semantics=sem, vmem_limit_bytes=V7X_VMEM_LIMIT)


def _mm(a, b):
    return jnp.dot(a.astype(BF16), b.astype(BF16), preferred_element_type=F32)


def _mm_nt(a, b):
    return lax.dot_general(a.astype(BF16), b.astype(BF16), (((1,), (1,)), ((), ())),
                           preferred_element_type=F32)


def _mm_tn(a, b):
    return lax.dot_general(a.astype(BF16), b.astype(BF16), (((0,), (0,)), ((), ())),
                           preferred_element_type=F32)


def _each(fn, *lists):
    return [fn(*xs) for xs in zip(*lists)]


def _split_dot(x, w_bf16):
    hi = x.astype(BF16)
    lo = (x - hi.astype(F32)).astype(BF16)
    return (jnp.dot(hi, w_bf16, preferred_element_type=F32)
            + jnp.dot(lo, w_bf16, preferred_element_type=F32))


def _prefix_dot(tri_bf16, x):
    hi = x.astype(BF16)
    rem = x - hi.astype(F32)
    mid = rem.astype(BF16)
    lo = (rem - mid.astype(F32)).astype(BF16)
    return (jnp.dot(tri_bf16, hi, preferred_element_type=F32) + jnp.dot(tri_bf16, mid, preferred_element_type=F32)
            + jnp.dot(tri_bf16, lo, preferred_element_type=F32))


def _softplus(u):
    return jnp.maximum(u, 0.0) + jnp.log(1.0 + jnp.exp(-jnp.abs(u)))


def _sigmoid(u):
    return 1.0 / (1.0 + jnp.exp(-u))


def _rms_norm(x, gain):
    ms = jnp.mean(x * x, axis=-1, keepdims=True)
    return x * lax.rsqrt(ms + RMS_EPS) * gain


def _proj_weight_kernel(w_ref, vd_ref, o_ref):
    n_in = 3 * C_MIX + RW_COLS
    w = w_ref[0]
    pad = jnp.zeros((w.shape[0], C_MIX - LORA_USED), F32)
    o_ref[0] = jnp.concatenate([w[:, :n_in], vd_ref[0], pad, w[:, n_in:]], axis=1).astype(BF16)


def _proj_weights(w_in, vdown_all):
    depth, d, n = w_in.shape
    n_out = n + C_MIX - (RW_COLS - 3 * C_MIX)
    tr = _pick(d, (256, 128))
    return pl.pallas_call(
        _proj_weight_kernel,
        grid=(depth, d // tr),
        in_specs=[pl.BlockSpec((1, tr, n), lambda li, i: (li, i, 0)),
                  pl.BlockSpec((1, tr, V_LORA), lambda li, i: (li, i, 0))],
        out_specs=pl.BlockSpec((1, tr, n_out), lambda li, i: (li, i, 0)),
        out_shape=jax.ShapeDtypeStruct((depth, d, n_out), BF16),
        compiler_params=_params("arbitrary", "arbitrary"),
        name="proj_weights",
    )(w_in, vdown_all)


def _norm_proj_kernel(h_ref, g_ref, w_ref, qkv_ref, rest_ref):
    xn = _rms_norm(h_ref[...], g_ref[...]).astype(BF16)
    qkv_ref[...] = jnp.dot(xn, w_ref[:, :N_QKV], preferred_element_type=F32).astype(BF16)
    rest_ref[...] = jnp.dot(xn, w_ref[:, N_QKV:], preferred_element_type=F32).astype(BF16)


def _norm_proj(h2, gain, w_cat_all, layer):
    m = h2.shape[0]
    n = w_cat_all.shape[2]
    tm = _pick(m, (384, 256, 128))
    return pl.pallas_call(
        _norm_proj_kernel,
        grid=(m // tm,),
        in_specs=[pl.BlockSpec((tm, D_MODEL), lambda i: (i, 0)),
                  pl.BlockSpec((1, D_MODEL), lambda i: (0, 0)),
                  pl.BlockSpec((None, D_MODEL, n), lambda i: (layer, 0, 0))],
        out_specs=[pl.BlockSpec((tm, N_QKV), lambda i: (i, 0)),
                   pl.BlockSpec((tm, n - N_QKV), lambda i: (i, 0))],
        out_shape=[jax.ShapeDtypeStruct((m, N_QKV), BF16), jax.ShapeDtypeStruct((m, n - N_QKV), BF16)],
        compiler_params=_params("arbitrary"),
        name="norm_proj",
    )(h2, gain.reshape(1, D_MODEL), w_cat_all)


def _sb_unit(q_ref, k_ref, v_ref, lm, o_ref, qi, tq):
    n_pairs = N_HEADS // 2
    pw = 2 * HEAD_DIM
    lane = lax.broadcasted_iota(jnp.int32, (1, pw), 1)
    head_a = lane < HEAD_DIM
    zero_bf = jnp.zeros((), BF16)
    split_rows = lambda x: jnp.concatenate([jnp.where(head_a, x, zero_bf), jnp.where(head_a, zero_bf, x)], axis=0)
    q = q_ref[0] * jnp.asarray(HEAD_DIM ** -0.5, BF16)
    q2 = [split_rows(q[:, p * pw:(p + 1) * pw]) for p in range(n_pairs)]
    row = lax.broadcasted_iota(jnp.int32, (2 * tq, tq), 0)
    col = lax.broadcasted_iota(jnp.int32, (2 * tq, tq), 1)
    causal2 = col < jnp.where(row >= tq, row - tq, row)
    state = {}

    def visit_steps(blocks, r_run, acc):
        chains = [(bi, p) for bi in range(len(blocks)) for p in range(n_pairs)]
        starts = [pl.multiple_of(blk[0] * tq, tq) for blk in blocks]
        ks = [k_ref[0, pl.ds(st, tq), :] for st in starts]
        vs = [v_ref[0, pl.ds(st, tq), :] for st in starts]
        vs = [v if blk[2] is None else jnp.where(blk[2], v, zero_bf) for v, blk in zip(vs, blocks)]
        z = [lax.dot_general(q2[p], ks[bi][:, p * pw:(p + 1) * pw], (((1,), (1,)), ((), ())),
                             preferred_element_type=F32) for bi, p in chains]
        yield
        sp = [_softplus(zi) for zi in z]
        sp = [jnp.where(causal2, s, 0.0) if blocks[bi][1] else s for s, (bi, p) in zip(sp, chains)]
        hi = [s.astype(BF16) for s in sp]
        lo = [(s - h.astype(F32)).astype(BF16) for s, h in zip(sp, hi)]
        wm = [jnp.dot(jnp.concatenate([h, l_], axis=1), lm, preferred_element_type=F32) for h, l_ in zip(hi, lo)]
        yield
        r_run, acc = list(r_run), list(acc)
        for i, (bi, p) in enumerate(chains):
            a = jnp.exp(z[i] - sp[i] - wm[i][:, :tq] - r_run[p])
            if blocks[bi][1]:
                a = jnp.where(causal2, a, 0.0)
            a = a.astype(BF16)
            v2 = split_rows(vs[bi][:, p * pw:(p + 1) * pw])
            acc[p] = acc[p] + jnp.dot(jnp.concatenate([a[:tq], a[tq:]], axis=1), v2, preferred_element_type=F32)
            r_run[p] = r_run[p] + wm[i][:, tq:]
        state["r_run"], state["acc"] = r_run, acc

    eager_blocks = [(qi, True, None)] + [(jnp.maximum(qi - back, 0), False, qi >= back)
                                         for back in range(1, SB_EAGER_BLOCKS + 1)]
    eager = visit_steps(eager_blocks, [jnp.zeros((2 * tq, tq), F32)] * n_pairs,
                        [jnp.zeros((tq, pw), F32)] * n_pairs)

    def visit(blocks, r_run, acc):
        for _ in visit_steps(blocks, r_run, acc):
            pass
        return state["r_run"], state["acc"]

    def store(acc):
        o_ref[0] = jnp.concatenate(acc, axis=1).astype(o_ref.dtype)

    return eager, state, visit, store


def _sb_finish(units, qi):
    def more(c):
        j, rrs, _ = c
        nearest = jnp.min(functools.reduce(jnp.minimum, [r[:, :1] for rr in rrs for r in rr]))
        return jnp.logical_and(j >= 0, nearest < SB_UNDERFLOW)

    def far(c):
        j, rrs, acs = c
        new = [visit([(j, False, None)], rr, ac) for (_, _, visit, _), rr, ac in zip(units, rrs, acs)]
        return j - 1, [n[0] for n in new], [n[1] for n in new]

    start = (qi - 1 - SB_EAGER_BLOCKS, [u[1]["r_run"] for u in units], [u[1]["acc"] for u in units])
    _, _, acs = lax.while_loop(more, far, start)
    for (_, _, _, store), acc in zip(units, acs):
        store(acc)


def _later_and_ones(tq):
    s_from = jnp.arange(2 * tq)[:, None] % tq
    s_to = jnp.arange(2 * tq)[None, :]
    return jnp.where(s_to < tq, s_from > s_to, True).astype(BF16)


def _unit_lower_inverse(a_strict, rowi, coli):
    eye = (rowi == coli).astype(F32)
    same = lambda sh: (rowi >> sh) == (coli >> sh)
    size = a_strict[0].shape[0]
    levels = [same(sh) for sh in range(3, size.bit_length() - 1)] + [None]
    m8 = levels[0]
    n1 = [-jnp.where(m8, a, 0.0) for a in a_strict]
    n2 = _each(_mm, n1, n1)
    n4 = _each(_mm, n2, n2)
    t = [eye + n for n in n1]
    t = _each(lambda ti, ni: ti + _mm(ti, ni), t, n2)
    t = _each(lambda ti, ni: ti + _mm(ti, ni), t, n4)
    for inner, outer in zip(levels[:-1], levels[1:]):
        off = jnp.logical_not(inner) if outer is None else jnp.logical_and(outer, jnp.logical_not(inner))
        ta = _each(lambda ti, a: _mm(ti, jnp.where(off, a, 0.0)), t, a_strict)
        t = _each(lambda ti, tai: ti - _mm(tai, ti), t, ta)
    return t


def _mixers_kernel(*refs, has_vres, n_units, n_qblocks):
    refs = list(refs)
    n_rw_in = 15 if has_vres else 13
    rw_in, refs = refs[:n_rw_in], refs[n_rw_in:]
    sb_in, refs = refs[:3 * n_units + 1], refs[3 * n_units + 1:]
    if has_vres:
        (main_ref, lora_ref, vfirst_ref, mu_main_ref, mu_lora_ref, w0_ref, a0_ref, kk_ref, ka_ref, vres0_ref,
         wl_ref, bd_ref, lnw_ref, lnb_ref, rk_ref) = rw_in
        y_ref, vout_ref, refs = refs[0], None, refs[1:]
    else:
        (main_ref, lora_ref, mu_main_ref, mu_lora_ref, w0_ref, a0_ref, kk_ref, ka_ref,
         wl_ref, bd_ref, lnw_ref, lnb_ref, rk_ref) = rw_in
        y_ref, vout_ref, refs = refs[0], refs[1], refs[2:]
    ysb_refs, (s_ref, pm_ref, pl_ref) = refs[:n_units], refs[n_units:]
    c = RW_CHUNK
    n_batch = main_ref.shape[0]
    n_chains = n_batch * N_HEADS

    @pl.when(pl.program_id(0) == 0)
    def _():
        s_ref[...] = jnp.zeros_like(s_ref)
        pm_ref[...] = jnp.zeros_like(pm_ref)
        pl_ref[...] = jnp.zeros_like(pl_ref)

    lm = sb_in[-1][...]
    q_block = pl.program_id(0) % n_qblocks
    units = [_sb_unit(sb_in[3 * j], sb_in[3 * j + 1], sb_in[3 * j + 2], lm, ysb_refs[j], q_block, SB_BLOCK)
             for j in range(n_units)]

    def attention_step():
        for unit in units:
            next(unit[0], None)

    first_row = lax.broadcasted_iota(jnp.int32, (c, 1), 0) == 0

    def shifted(x, prev8, mu):
        prev = jnp.where(first_row, prev8[7:8, :], pltpu.roll(x, shift=1, axis=0))
        return x + (prev - x) * mu

    mains = [main_ref[bi].astype(F32) for bi in range(n_batch)]
    loras = [lora_ref[bi].astype(F32) for bi in range(n_batch)]
    xs = jnp.concatenate([shifted(x, pm_ref[bi], mu_main_ref[...]) for bi, x in enumerate(mains)], axis=0)
    lo = jnp.concatenate([shifted(x, pl_ref[bi], mu_lora_ref[...]) for bi, x in enumerate(loras)], axis=0)
    for bi in range(n_batch):
        pm_ref[bi] = mains[bi][c - 8:c, :]
        pl_ref[bi] = loras[bi][c - 8:c, :]

    r = xs[:, 0:C_MIX]
    kr = xs[:, C_MIX:2 * C_MIX]
    vr = xs[:, 2 * C_MIX:3 * C_MIX]
    lane = lax.broadcasted_iota(jnp.int32, (1, 128), 1)
    g_tail = W_LORA + A_LORA + G_LORA - 256
    act = jnp.concatenate([jnp.where(lane < W_LORA, jnp.tanh(lo[:, 0:128]), lo[:, 0:128]),
                           _sigmoid(lo[:, 128:256]),
                           jnp.where(lane < g_tail, _sigmoid(lo[:, 256:384]), lo[:, 256:384])], axis=1)
    pre = jnp.dot(act.astype(BF16), wl_ref[...], preferred_element_type=F32)
    w_log = -_softplus(-(w0_ref[...] + pre[:, 0:C_MIX])) - 0.5
    ld = -jnp.exp(w_log)
    a = _sigmoid(a0_ref[...] + pre[:, C_MIX:2 * C_MIX])
    g = pre[:, 2 * C_MIX:3 * C_MIX]
    if has_vres:
        v_first = jnp.concatenate([vfirst_ref[bi] for bi in range(n_batch)], axis=0)
        vr = vr + (v_first - vr) * _sigmoid(vres0_ref[...] + pre[:, 3 * C_MIX:4 * C_MIX])
    else:
        for bi in range(n_batch):
            vout_ref[bi] = vr[bi * c:(bi + 1) * c]
    bd = bd_ref[...]
    kk = kr * kk_ref[...]
    kk = kk * lax.rsqrt(jnp.maximum(_split_dot(kk * kk, bd), 1e-24))
    k = kr * (1.0 + (a - 1.0) * ka_ref[...])
    bvec = kk * a

    rowi = lax.broadcasted_iota(jnp.int32, (c, c), 0)
    coli = lax.broadcasted_iota(jnp.int32, (c, c), 1)
    lower_incl = coli <= rowi
    lower_strict = coli < rowi
    tri = lower_incl.astype(BF16)
    rows = lambda x, bi: x[bi * c:(bi + 1) * c]
    kt_all, bt_all, kn_all, rt_all, bh_all, kh_all, g_end = [], [], [], [], [], [], []
    for bi in range(n_batch):
        ld_b = rows(ld, bi)
        cum = _prefix_dot(tri, ld_b)
        cum_end = cum[c - 1:c, :]
        e_neg = jnp.exp(-cum)
        to_end = jnp.exp(cum_end - cum)
        kt_all.append(rows(kk, bi) * jnp.exp(cum - ld_b))
        bt_all.append(rows(bvec, bi) * e_neg)
        kn_all.append(rows(k, bi) * e_neg)
        rt_all.append(rows(r, bi) * jnp.exp(cum))
        bh_all.append(rows(bvec, bi) * to_end)
        kh_all.append(rows(k, bi) * to_end)
        g_end.append(jnp.exp(cum_end))
    v_all = [rows(vr, bi) for bi in range(n_batch)]

    heads = lambda xb: [x[:, h * HEAD_DIM:(h + 1) * HEAD_DIM] for x in xb for h in range(N_HEADS)]
    kt, bt, kn, rt, bh, kh, v, g_h = (heads(x) for x in (kt_all, bt_all, kn_all, rt_all, bh_all, kh_all, v_all,
                                                        g_end))
    hd = HEAD_DIM
    kt_rt = _each(lambda x, y: jnp.concatenate([x, y], axis=0), kt, rt)
    on_b = _each(_mm_nt, kt_rt, bt)
    on_k = _each(_mm_nt, kt_rt, kn)
    a_b = [jnp.where(lower_strict, x[:c], 0.0) for x in on_b]
    a_rb = [jnp.where(lower_incl, x[c:], 0.0) for x in on_b]
    a_k = [jnp.where(lower_strict, x[:c], 0.0) for x in on_k]
    a_rk = [jnp.where(lower_incl, x[c:], 0.0) for x in on_k]
    av = _each(lambda x, y, vv: _mm(jnp.concatenate([x, y], axis=0), vv), a_k, a_rk, v)
    akv = [x[:c] for x in av]
    arkv = [x[c:] for x in av]
    vkh = _each(_mm_tn, v, kh)
    attention_step()
    t_inv = _unit_lower_inverse(a_b, rowi, coli)
    attention_step()
    ku = _each(lambda t_, x, y: _mm(t_, jnp.concatenate([x, y], axis=1)), t_inv, kt, akv)
    ry = _each(_mm, a_rb, ku)
    r_hat = _each(lambda x, z: x - z[:, :hd], rt, ry)
    y_bar = _each(lambda x, z: x - z[:, hd:], arkv, ry)
    eh = _each(_mm_tn, ku, bh)
    attention_step()
    e_bar = [x[:hd] for x in eh]
    h_add = _each(lambda x, z: x - z[hd:], vkh, eh)
    s = [s_ref[h] for h in range(n_chains)]
    ys = _each(lambda rh, sh, yb: _mm_nt(rh, sh) + yb, r_hat, s, y_bar)
    s_new = _each(lambda sh, gh, eb, ha: sh * gh - _mm(sh, eb) + ha, s, g_h, e_bar, h_add)
    for h in range(n_chains):
        s_ref[h] = s_new[h]

    y = jnp.concatenate([jnp.concatenate(ys[bi * N_HEADS:(bi + 1) * N_HEADS], axis=-1) for bi in range(n_batch)],
                        axis=0)
    inv_n = 1.0 / HEAD_DIM
    mean = _split_dot(y, bd) * inv_n
    d = y - mean
    var = _split_dot(d * d, bd) * inv_n
    yn = d * lax.rsqrt(var + GN_EPS) * lnw_ref[...] + lnb_ref[...]
    bonus = _split_dot(r * k * rk_ref[...], bd) * vr
    out = (yn + bonus) * g
    for bi in range(n_batch):
        y_ref[bi] = out[bi * c:(bi + 1) * c].astype(y_ref.dtype)

    _sb_finish(units, q_block)


def _mixers(qkv3, rest3, v_first, mu_main, mu_lora, w0, a0, k_k, k_a, vres0, w_lora, bd, ln_w, ln_b, r_k):
    b, l, _ = rest3.shape
    has_vres = v_first is not None
    c = RW_CHUNK
    tq = SB_BLOCK
    n_q = l // tq
    n_units = b // 2
    assert b % 2 == 0 and l // c == 2 * n_q, "one query block of two batch elements per unit and chunk pair"
    tok = pl.BlockSpec((b, c, C_MIX), lambda ci: (0, ci, 0))
    row = lambda width: pl.BlockSpec((1, width), lambda ci: (0, 0))
    in_specs = [pl.BlockSpec((b, c, 3 * C_MIX), lambda ci: (0, ci, COL_RW // (3 * C_MIX))),
                pl.BlockSpec((b, c, LORA_W), lambda ci: (0, ci, COL_LORA // LORA_W))]
    args = [rest3, rest3]
    if has_vres:
        in_specs.append(tok)
        args.append(v_first)
    in_specs += [row(3 * C_MIX), row(LORA_W)] + [row(C_MIX)] * 4
    args += [mu_main, mu_lora, w0, a0, k_k, k_a]
    if has_vres:
        in_specs.append(row(C_MIX))
        args.append(vres0)
    in_specs += [pl.BlockSpec((LORA_W, 4 * C_MIX), lambda ci: (0, 0)), pl.BlockSpec((C_MIX, C_MIX), lambda ci: (0, 0)),
                 row(C_MIX), row(C_MIX), row(C_MIX)]
    args += [w_lora, bd, ln_w, ln_b, r_k]
    for j in range(n_units):
        which = lambda ci, j=j: 2 * j + ci // n_q
        in_specs += [pl.BlockSpec((1, tq, C_MIX), lambda ci, w=which: (w(ci), ci % n_q, 0)),
                     pl.BlockSpec((1, l, C_MIX), lambda ci, w=which: (w(ci), 0, 1), pipeline_mode=pl.Buffered(1)),
                     pl.BlockSpec((1, l, C_MIX), lambda ci, w=which: (w(ci), 0, 2), pipeline_mode=pl.Buffered(1))]
        args += [qkv3, qkv3, qkv3]
    in_specs.append(pl.BlockSpec((2 * tq, 2 * tq), lambda ci: (0, 0)))
    args.append(_later_and_ones(tq))
    out_shape = [jax.ShapeDtypeStruct((b, l, C_MIX), BF16)]
    out_specs = [tok]
    if not has_vres:
        out_shape.append(jax.ShapeDtypeStruct((b, l, C_MIX), F32))
        out_specs.append(tok)
    out_shape += [jax.ShapeDtypeStruct((2, l, C_MIX), BF16)] * n_units
    out_specs += [pl.BlockSpec((1, tq, C_MIX), lambda ci: (ci // n_q, ci % n_q, 0))] * n_units
    res = pl.pallas_call(
        functools.partial(_mixers_kernel, has_vres=has_vres, n_units=n_units, n_qblocks=n_q),
        grid=(l // c,),
        in_specs=in_specs,
        out_specs=out_specs,
        out_shape=out_shape,
        scratch_shapes=[pltpu.VMEM((b * N_HEADS, HEAD_DIM, HEAD_DIM), F32),
                        pltpu.VMEM((b, 8, 3 * C_MIX), F32), pltpu.VMEM((b, 8, LORA_W), F32)],
        compiler_params=_params("arbitrary"),
        name="mixers",
    )(*args)
    y_sb_units = [t.reshape(2 * l, C_MIX) for t in res[-n_units:]]
    return (y_sb_units, res[0], None) if has_vres else (y_sb_units, res[0], res[1])


def _merge_ffn_kernel(*refs, n_units, blocks_per_unit):
    h_ref, gsb_ref, grw_ref = refs[:3]
    ysb_refs = refs[3:3 + n_units]
    yrw_ref, wsb_ref, wrw_ref, wout_ref, g_ref, wg_ref, wu_ref, wo_ref, o_ref = refs[3 + n_units:]
    unit = pl.program_id(0) // blocks_per_unit
    y_sb = ysb_refs[0][...]
    for j in range(1, n_units):
        y_sb = jnp.where(unit == j, ysb_refs[j][...], y_sb)
    o_sb = jnp.dot(y_sb, wsb_ref[...], preferred_element_type=F32)
    o_rw = jnp.dot(yrw_ref[...], wrw_ref[...], preferred_element_type=F32)
    merged = _sigmoid(gsb_ref[...].astype(F32)) * o_sb + _sigmoid(grw_ref[...].astype(F32)) * o_rw
    x = h_ref[...] + jnp.dot(merged.astype(BF16), wout_ref[...], preferred_element_type=F32)
    hn = _rms_norm(x, g_ref[...]).astype(BF16)
    gate = jnp.dot(hn, wg_ref[...], preferred_element_type=F32)
    up = jnp.dot(hn, wu_ref[...], preferred_element_type=F32)
    act = gate * _sigmoid(gate) * up
    o_ref[...] = x + jnp.dot(act.astype(BF16), wo_ref[...], preferred_element_type=F32)


def _merge_ffn(h2, rest2, y_sb_units, y_rw, w_sb, w_rw, w_out, gain, w_ffn_in, w_ffn_out, layer):
    m = h2.shape[0]
    n_units = len(y_sb_units)
    tm = _pick(m // n_units, (384, 256, 128))
    per_unit = m // n_units // tm
    rows = lambda width: pl.BlockSpec((tm, width), lambda i: (i, 0))
    full = lambda shape, col=0: pl.BlockSpec((None,) + shape, lambda i: (layer, 0, col),
                                             pipeline_mode=pl.Buffered(1))
    unit_rows = [pl.BlockSpec((tm, C_MIX), lambda i, j=j: (jnp.clip(i - j * per_unit, 0, per_unit - 1), 0))
                 for j in range(n_units)]
    return pl.pallas_call(
        functools.partial(_merge_ffn_kernel, n_units=n_units, blocks_per_unit=per_unit),
        grid=(m // tm,),
        in_specs=[rows(D_MODEL),
                  pl.BlockSpec((tm, D_MODEL), lambda i: (i, COL_GATES // D_MODEL)),
                  pl.BlockSpec((tm, D_MODEL), lambda i: (i, COL_GATES // D_MODEL + 1)),
                  *unit_rows, rows(C_MIX),
                  full((C_MIX, D_MODEL)), full((C_MIX, D_MODEL)), full((D_MODEL, D_MODEL)),
                  pl.BlockSpec((1, D_MODEL), lambda i: (0, 0)),
                  full((D_MODEL, FFN_HIDDEN)), full((D_MODEL, FFN_HIDDEN), 1), full((FFN_HIDDEN, D_MODEL))],
        out_specs=rows(D_MODEL),
        out_shape=jax.ShapeDtypeStruct((m, D_MODEL), F32),
        compiler_params=_params("arbitrary"),
        name="merge_ffn",
    )(h2, rest2, rest2, *y_sb_units, y_rw, w_sb, w_rw, w_out, gain.reshape(1, D_MODEL),
      w_ffn_in, w_ffn_in, w_ffn_out)


def _final_norm_kernel(h_ref, g_ref, o_ref):
    o_ref[...] = _rms_norm(h_ref[...], g_ref[...])


def _final_norm(h3, gain, s):
    b = h3.shape[0]
    tm = _pick(s, (512, 256, 128, 16))
    return pl.pallas_call(
        _final_norm_kernel,
        grid=(b, s // tm),
        in_specs=[pl.BlockSpec((pl.Element(1), pl.Element(tm), pl.Element(D_MODEL)),
                               lambda bi, i: (bi, pl.multiple_of(i * tm + N_META, N_META), 0)),
                  pl.BlockSpec((1, 1, D_MODEL), lambda bi, i: (0, 0, 0))],
        out_specs=pl.BlockSpec((1, tm, D_MODEL), lambda bi, i: (bi, i, 0)),
        out_shape=jax.ShapeDtypeStruct((b, s, D_MODEL), F32),
        compiler_params=_params("arbitrary", "arbitrary"),
        name="final_norm",
    )(h3, gain.reshape(1, 1, D_MODEL))


def _head_block_diag():
    idx = jnp.arange(C_MIX) // HEAD_DIM
    return (idx[:, None] == idx[None, :]).astype(BF16)


def _lora_weight(w_up, a_up, g_up, vres_up):
    w = jnp.zeros((LORA_W, 4 * C_MIX), F32)
    o = 0
    for seg, (mat, width) in enumerate(((w_up, W_LORA), (a_up, A_LORA), (g_up, G_LORA), (vres_up, V_LORA))):
        if mat is not None:
            w = w.at[o:o + width, seg * C_MIX:(seg + 1) * C_MIX].set(mat)
        o += width
    return w.astype(BF16)


def kernel(x, meta_tokens, norm_mix, norm_ffn, norm_final, w_in, mu_rw, w0, w_up, a0, a_up, g_up, k_k, k_a, r_k, ln_x_w, ln_x_b, vres_down, vres_mu, vres_up, vres0, w_sb_out, w_rw_out, w_out, w_ffn_in, w_ffn_out):
    b, s, d = x.shape
    depth = w_in.shape[0]
    l_real = N_META + s
    l_pad = -(-l_real // SB_BLOCK) * SB_BLOCK
    meta = jnp.broadcast_to(meta_tokens.astype(x.dtype)[None], (b, N_META, d))
    h = jnp.concatenate([meta, x, jnp.zeros((b, l_pad - l_real, d), x.dtype)], axis=1)
    h2 = h.reshape(b * l_pad, d)
    bd = _head_block_diag()
    n_in = 3 * C_MIX + RW_COLS
    row = lambda vec: vec.reshape(1, -1)

    vdown_all = jnp.concatenate([jnp.zeros((1, d, V_LORA), F32), vres_down], axis=0)
    w_cat_all = _proj_weights(w_in, vdown_all)
    w_sb_all, w_rw_all, w_out_all = (w.astype(BF16) for w in (w_sb_out, w_rw_out, w_out))
    w_ffn_in_all, w_ffn_out_all = w_ffn_in.astype(BF16), w_ffn_out.astype(BF16)

    v_first = None
    for layer in range(depth):
        vmu = vres_mu[layer - 1] if layer > 0 else jnp.zeros((V_LORA,), F32)
        mu_main = row(mu_rw[layer, :3 * C_MIX])
        mu_lora = row(jnp.concatenate([mu_rw[layer, 3 * C_MIX:], vmu, jnp.zeros((LORA_W - LORA_USED,), F32)]))
        w_lora = _lora_weight(w_up[layer], a_up[layer], g_up[layer], vres_up[layer - 1] if layer > 0 else None)

        qkv2, rest2 = _norm_proj(h2, norm_mix[layer], w_cat_all, layer)
        y_sb, y_rw, v_out = _mixers(
            qkv2.reshape(b, l_pad, N_QKV), rest2.reshape(b, l_pad, N_REST), v_first, mu_main, mu_lora,
            row(w0[layer]), row(a0[layer]),
            row(k_k[layer]), row(k_a[layer]), row(vres0[layer - 1]) if layer > 0 else None, w_lora, bd,
            row(ln_x_w[layer]), row(ln_x_b[layer]), row(r_k[layer].reshape(-1)))
        if layer == 0:
            v_first = v_out
        flat = lambda t: t.reshape(b * l_pad, C_MIX)
        h2 = _merge_ffn(h2, rest2, y_sb, flat(y_rw), w_sb_all, w_rw_all, w_out_all,
                        norm_ffn[layer], w_ffn_in_all, w_ffn_out_all, layer)

    return _final_norm(h2.reshape(b, l_pad, d), norm_final, s)
```

```python
import functools

import jax
import jax.numpy as jnp
from jax import lax
from jax.experimental import pallas as pl
from jax.experimental.pallas import tpu as pltpu

D_MODEL = 1024
HEAD_DIM = 64
N_HEADS = 8
C_MIX = N_HEADS * HEAD_DIM
N_META = 16
SB_BLOCK = 128
W_LORA, A_LORA, V_LORA, G_LORA = 64, 64, 32, 160
RW_COLS = 3 * C_MIX + W_LORA + A_LORA + G_LORA
FFN_HIDDEN = 2816
RMS_EPS = 1e-6
GN_EPS = 64e-5
RW_CHUNK = 64
SB_UNDERFLOW = 104.0
SB_EAGER_BLOCKS = 2

N_QKV = 3 * C_MIX
COL_RW = 0
COL_LORA = 1536
COL_GATES = 2048
N_REST = 4096
LORA_USED = W_LORA + A_LORA + G_LORA + V_LORA
LORA_W = 384
assert W_LORA + A_LORA == 128 and 256 <= W_LORA + A_LORA + G_LORA and LORA_USED <= LORA_W

V7X_VMEM_LIMIT = 56 * 1024 * 1024

F32 = jnp.float32
BF16 = jnp.bfloat16


def _pick(n, cands):
    for c in cands:
        if n % c == 0:
            return c
    raise ValueError(f"no tile for {n} in {cands}")


def _params(*sem):
    return pltpu.CompilerParams(dimension_semantics=sem, vmem_limit_bytes=V7X_VMEM_LIMIT)


def _mm(a, b):
    return jnp.dot(a.astype(BF16), b.astype(BF16), preferred_element_type=F32)


def _mm_nt(a, b):
    return lax.dot_general(a.astype(BF16), b.astype(BF16), (((1,), (1,)), ((), ())),
                           preferred_element_type=F32)


def _mm_tn(a, b):
    return lax.dot_general(a.astype(BF16), b.astype(BF16), (((0,), (0,)), ((), ())),
                           preferred_element_type=F32)


def _each(fn, *lists):
    return [fn(*xs) for xs in zip(*lists)]


def _split_dot(x, w_bf16):
    hi = x.astype(BF16)
    lo = (x - hi.astype(F32)).astype(BF16)
    return (jnp.dot(hi, w_bf16, preferred_element_type=F32)
            + jnp.dot(lo, w_bf16, preferred_element_type=F32))


def _prefix_dot(tri_bf16, x):
    hi = x.astype(BF16)
    rem = x - hi.astype(F32)
    mid = rem.astype(BF16)
    lo = (rem - mid.astype(F32)).astype(BF16)
    return (jnp.dot(tri_bf16, hi, preferred_element_type=F32) + jnp.dot(tri_bf16, mid, preferred_element_type=F32)
            + jnp.dot(tri_bf16, lo, preferred_element_type=F32))


def _softplus(u):
    return jnp.maximum(u, 0.0) + jnp.log(1.0 + jnp.exp(-jnp.abs(u)))


def _sigmoid(u):
    return 1.0 / (1.0 + jnp.exp(-u))


def _rms_norm(x, gain):
    ms = jnp.mean(x * x, axis=-1, keepdims=True)
    return x * lax.rsqrt(ms + RMS_EPS) * gain


def _proj_weight_kernel(w_ref, vd_ref, o_ref):
    n_in = 3 * C_MIX + RW_COLS
    w = w_ref[...]
    pad = jnp.zeros((w.shape[0], C_MIX - LORA_USED), F32)
    o_ref[...] = jnp.concatenate([w[:, :n_in], vd_ref[...], pad, w[:, n_in:]], axis=1).astype(BF16)


def _proj_weights(w_in, vdown_all):
    depth, d, n = w_in.shape
    n_out = n + C_MIX - (RW_COLS - 3 * C_MIX)
    tr = _pick(d, (256, 128))
    out = pl.pallas_call(
        _proj_weight_kernel,
        grid=(depth * d // tr,),
        in_specs=[pl.BlockSpec((tr, n), lambda i: (i, 0)),
                  pl.BlockSpec((tr, V_LORA), lambda i: (i, 0))],
        out_specs=pl.BlockSpec((tr, n_out), lambda i: (i, 0)),
        out_shape=jax.ShapeDtypeStruct((depth * d, n_out), BF16),
        compiler_params=_params("arbitrary"),
        name="proj_weights",
    )(w_in.reshape(depth * d, n), vdown_all.reshape(depth * d, V_LORA))
    return out.reshape(depth, d, n_out)


def _norm_proj_kernel(h_ref, g_ref, w_ref, qkv_ref, rest_ref):
    xn = _rms_norm(h_ref[...], g_ref[...]).astype(BF16)
    qkv_ref[...] = jnp.dot(xn, w_ref[:, :N_QKV], preferred_element_type=F32).astype(BF16)
    rest_ref[...] = jnp.dot(xn, w_ref[:, N_QKV:], preferred_element_type=F32).astype(BF16)


def _norm_proj(h2, gain, w_cat_all, layer):
    m = h2.shape[0]
    n = w_cat_all.shape[2]
    tm = _pick(m, (384, 256, 128))
    return pl.pallas_call(
        _norm_proj_kernel,
        grid=(m // tm,),
        in_specs=[pl.BlockSpec((tm, D_MODEL), lambda i: (i, 0)),
                  pl.BlockSpec((1, D_MODEL), lambda i: (0, 0)),
                  pl.BlockSpec((None, D_MODEL, n), lambda i: (layer, 0, 0))],
        out_specs=[pl.BlockSpec((tm, N_QKV), lambda i: (i, 0)),
                   pl.BlockSpec((tm, n - N_QKV), lambda i: (i, 0))],
        out_shape=[jax.ShapeDtypeStruct((m, N_QKV), BF16), jax.ShapeDtypeStruct((m, n - N_QKV), BF16)],
        compiler_params=_params("arbitrary"),
        name="norm_proj",
    )(h2, gain.reshape(1, D_MODEL), w_cat_all)


def _sb_unit(q_ref, k_ref, v_ref, lm, o_ref, qi, tq):
    n_pairs = N_HEADS // 2
    pw = 2 * HEAD_DIM
    lane = lax.broadcasted_iota(jnp.int32, (1, pw), 1)
    head_a = lane < HEAD_DIM
    zero_bf = jnp.zeros((), BF16)
    split_rows = lambda x: jnp.concatenate([jnp.where(head_a, x, zero_bf), jnp.where(head_a, zero_bf, x)], axis=0)
    q = q_ref[0] * jnp.asarray(HEAD_DIM ** -0.5, BF16)
    q2 = [split_rows(q[:, p * pw:(p + 1) * pw]) for p in range(n_pairs)]
    row = lax.broadcasted_iota(jnp.int32, (2 * tq, tq), 0)
    col = lax.broadcasted_iota(jnp.int32, (2 * tq, tq), 1)
    causal2 = col < jnp.where(row >= tq, row - tq, row)
    state = {}

    def visit_steps(blocks, r_run, acc):
        chains = [(bi, p) for bi in range(len(blocks)) for p in range(n_pairs)]
        starts = [pl.multiple_of(blk[0] * tq, tq) for blk in blocks]
        ks = [k_ref[0, pl.ds(st, tq), :] for st in starts]
        vs = [v_ref[0, pl.ds(st, tq), :] for st in starts]
        vs = [v if blk[2] is None else jnp.where(blk[2], v, zero_bf) for v, blk in zip(vs, blocks)]
        z = [lax.dot_general(q2[p], ks[bi][:, p * pw:(p + 1) * pw], (((1,), (1,)), ((), ())),
                             preferred_element_type=F32) for bi, p in chains]
        yield
        sp = [_softplus(zi) for zi in z]
        sp = [jnp.where(causal2, s, 0.0) if blocks[bi][1] else s for s, (bi, p) in zip(sp, chains)]
        hi = [s.astype(BF16) for s in sp]
        lo = [(s - h.astype(F32)).astype(BF16) for s, h in zip(sp, hi)]
        wm = [jnp.dot(jnp.concatenate([h, l_], axis=1), lm, preferred_element_type=F32) for h, l_ in zip(hi, lo)]
        yield
        r_run, acc = list(r_run), list(acc)
        for i, (bi, p) in enumerate(chains):
            a = jnp.exp(z[i] - sp[i] - wm[i][:, :tq] - r_run[p])
            if blocks[bi][1]:
                a = jnp.where(causal2, a, 0.0)
            a = a.astype(BF16)
            v2 = split_rows(vs[bi][:, p * pw:(p + 1) * pw])
            acc[p] = acc[p] + jnp.dot(jnp.concatenate([a[:tq], a[tq:]], axis=1), v2, preferred_element_type=F32)
            r_run[p] = r_run[p] + wm[i][:, tq:]
        state["r_run"], state["acc"] = r_run, acc

    eager_blocks = [(qi, True, None)] + [(jnp.maximum(qi - back, 0), False, qi >= back)
                                         for back in range(1, SB_EAGER_BLOCKS + 1)]
    eager = visit_steps(eager_blocks, [jnp.zeros((2 * tq, tq), F32)] * n_pairs,
                        [jnp.zeros((tq, pw), F32)] * n_pairs)

    def visit(blocks, r_run, acc):
        for _ in visit_steps(blocks, r_run, acc):
            pass
        return state["r_run"], state["acc"]

    def store(acc):
        o_ref[0] = jnp.concatenate(acc, axis=1).astype(o_ref.dtype)

    return eager, state, visit, store


def _sb_finish(units, qi):
    def more(c):
        j, rrs, _ = c
        nearest = jnp.min(functools.reduce(jnp.minimum, [r[:, :1] for rr in rrs for r in rr]))
        return jnp.logical_and(j >= 0, nearest < SB_UNDERFLOW)

    def far(c):
        j, rrs, acs = c
        new = [visit([(j, False, None)], rr, ac) for (_, _, visit, _), rr, ac in zip(units, rrs, acs)]
        return j - 1, [n[0] for n in new], [n[1] for n in new]

    start = (qi - 1 - SB_EAGER_BLOCKS, [u[1]["r_run"] for u in units], [u[1]["acc"] for u in units])
    _, _, acs = lax.while_loop(more, far, start)
    for (_, _, _, store), acc in zip(units, acs):
        store(acc)


def _later_and_ones(tq):
    s_from = jnp.arange(2 * tq)[:, None] % tq
    s_to = jnp.arange(2 * tq)[None, :]
    return jnp.where(s_to < tq, s_from > s_to, True).astype(BF16)


def _unit_lower_inverse(a_strict, rowi, coli):
    eye = (rowi == coli).astype(F32)
    same = lambda sh: (rowi >> sh) == (coli >> sh)
    size = a_strict[0].shape[0]
    levels = [same(sh) for sh in range(3, size.bit_length() - 1)] + [None]
    m8 = levels[0]
    n1 = [-jnp.where(m8, a, 0.0) for a in a_strict]
    n2 = _each(_mm, n1, n1)
    n4 = _each(_mm, n2, n2)
    t = [eye + n for n in n1]
    t = _each(lambda ti, ni: ti + _mm(ti, ni), t, n2)
    t = _each(lambda ti, ni: ti + _mm(ti, ni), t, n4)
    for inner, outer in zip(levels[:-1], levels[1:]):
        off = jnp.logical_not(inner) if outer is None else jnp.logical_and(outer, jnp.logical_not(inner))
        ta = _each(lambda ti, a: _mm(ti, jnp.where(off, a, 0.0)), t, a_strict)
        t = _each(lambda ti, tai: ti - _mm(tai, ti), t, ta)
    return t


def _mixers_kernel(*refs, has_vres, n_units, n_qblocks):
    refs = list(refs)
    n_rw_in = 15 if has_vres else 13
    rw_in, refs = refs[:n_rw_in], refs[n_rw_in:]
    sb_in, refs = refs[:3 * n_units + 1], refs[3 * n_units + 1:]
    if has_vres:
        (main_ref, lora_ref, vfirst_ref, mu_main_ref, mu_lora_ref, w0_ref, a0_ref, kk_ref, ka_ref, vres0_ref,
         wl_ref, bd_ref, lnw_ref, lnb_ref, rk_ref) = rw_in
        y_ref, vout_ref, refs = refs[0], None, refs[1:]
    else:
        (main_ref, lora_ref, mu_main_ref, mu_lora_ref, w0_ref, a0_ref, kk_ref, ka_ref,
         wl_ref, bd_ref, lnw_ref, lnb_ref, rk_ref) = rw_in
        y_ref, vout_ref, refs = refs[0], refs[1], refs[2:]
    ysb_refs, (s_ref, pm_ref, pl_ref) = refs[:n_units], refs[n_units:]
    c = RW_CHUNK
    n_batch = main_ref.shape[0]
    n_chains = n_batch * N_HEADS

    @pl.when(pl.program_id(0) == 0)
    def _():
        s_ref[...] = jnp.zeros_like(s_ref)
        pm_ref[...] = jnp.zeros_like(pm_ref)
        pl_ref[...] = jnp.zeros_like(pl_ref)

    lm = sb_in[-1][...]
    q_block = pl.program_id(0) % n_qblocks
    units = [_sb_unit(sb_in[3 * j], sb_in[3 * j + 1], sb_in[3 * j + 2], lm, ysb_refs[j], q_block, SB_BLOCK)
             for j in range(n_units)]

    def attention_step():
        for unit in units:
            next(unit[0], None)

    first_row = lax.broadcasted_iota(jnp.int32, (c, 1), 0) == 0

    def shifted(x, prev8, mu):
        prev = jnp.where(first_row, prev8[7:8, :], pltpu.roll(x, shift=1, axis=0))
        return x + (prev - x) * mu

    mains = [main_ref[bi].astype(F32) for bi in range(n_batch)]
    loras = [lora_ref[bi].astype(F32) for bi in range(n_batch)]
    xs = jnp.concatenate([shifted(x, pm_ref[bi], mu_main_ref[...]) for bi, x in enumerate(mains)], axis=0)
    lo = jnp.concatenate([shifted(x, pl_ref[bi], mu_lora_ref[...]) for bi, x in enumerate(loras)], axis=0)
    for bi in range(n_batch):
        pm_ref[bi] = mains[bi][c - 8:c, :]
        pl_ref[bi] = loras[bi][c - 8:c, :]

    r = xs[:, 0:C_MIX]
    kr = xs[:, C_MIX:2 * C_MIX]
    vr = xs[:, 2 * C_MIX:3 * C_MIX]
    lane = lax.broadcasted_iota(jnp.int32, (1, 128), 1)
    g_tail = W_LORA + A_LORA + G_LORA - 256
    act = jnp.concatenate([jnp.where(lane < W_LORA, jnp.tanh(lo[:, 0:128]), lo[:, 0:128]),
                           _sigmoid(lo[:, 128:256]),
                           jnp.where(lane < g_tail, _sigmoid(lo[:, 256:384]), lo[:, 256:384])], axis=1)
    pre = jnp.dot(act.astype(BF16), wl_ref[...], preferred_element_type=F32)
    w_log = -_softplus(-(w0_ref[...] + pre[:, 0:C_MIX])) - 0.5
    ld = -jnp.exp(w_log)
    a = _sigmoid(a0_ref[...] + pre[:, C_MIX:2 * C_MIX])
    g = pre[:, 2 * C_MIX:3 * C_MIX]
    if has_vres:
        v_first = jnp.concatenate([vfirst_ref[bi] for bi in range(n_batch)], axis=0)
        vr = vr + (v_first - vr) * _sigmoid(vres0_ref[...] + pre[:, 3 * C_MIX:4 * C_MIX])
    else:
        for bi in range(n_batch):
            vout_ref[bi] = vr[bi * c:(bi + 1) * c]
    bd = bd_ref[...]
    kk = kr * kk_ref[...]
    kk = kk * lax.rsqrt(jnp.maximum(_split_dot(kk * kk, bd), 1e-24))
    k = kr * (1.0 + (a - 1.0) * ka_ref[...])
    bvec = kk * a

    rowi = lax.broadcasted_iota(jnp.int32, (c, c), 0)
    coli = lax.broadcasted_iota(jnp.int32, (c, c), 1)
    lower_incl = coli <= rowi
    lower_strict = coli < rowi
    tri = lower_incl.astype(BF16)
    rows = lambda x, bi: x[bi * c:(bi + 1) * c]
    kt_all, bt_all, kn_all, rt_all, bh_all, kh_all, g_end = [], [], [], [], [], [], []
    for bi in range(n_batch):
        ld_b = rows(ld, bi)
        cum = _prefix_dot(tri, ld_b)
        cum_end = cum[c - 1:c, :]
        e_neg = jnp.exp(-cum)
        to_end = jnp.exp(cum_end - cum)
        kt_all.append(rows(kk, bi) * jnp.exp(cum - ld_b))
        bt_all.append(rows(bvec, bi) * e_neg)
        kn_all.append(rows(k, bi) * e_neg)
        rt_all.append(rows(r, bi) * jnp.exp(cum))
        bh_all.append(rows(bvec, bi) * to_end)
        kh_all.append(rows(k, bi) * to_end)
        g_end.append(jnp.exp(cum_end))
    v_all = [rows(vr, bi) for bi in range(n_batch)]

    heads = lambda xb: [x[:, h * HEAD_DIM:(h + 1) * HEAD_DIM] for x in xb for h in range(N_HEADS)]
    kt, bt, kn, rt, bh, kh, v, g_h = (heads(x) for x in (kt_all, bt_all, kn_all, rt_all, bh_all, kh_all, v_all,
                                                        g_end))
    hd = HEAD_DIM
    kt_rt = _each(lambda x, y: jnp.concatenate([x, y], axis=0), kt, rt)
    on_b = _each(_mm_nt, kt_rt, bt)
    on_k = _each(_mm_nt, kt_rt, kn)
    a_b = [jnp.where(lower_strict, x[:c], 0.0) for x in on_b]
    a_rb = [jnp.where(lower_incl, x[c:], 0.0) for x in on_b]
    a_k = [jnp.where(lower_strict, x[:c], 0.0) for x in on_k]
    a_rk = [jnp.where(lower_incl, x[c:], 0.0) for x in on_k]
    av = _each(lambda x, y, vv: _mm(jnp.concatenate([x, y], axis=0), vv), a_k, a_rk, v)
    akv = [x[:c] for x in av]
    arkv = [x[c:] for x in av]
    vkh = _each(_mm_tn, v, kh)
    attention_step()
    t_inv = _unit_lower_inverse(a_b, rowi, coli)
    attention_step()
    ku = _each(lambda t_, x, y: _mm(t_, jnp.concatenate([x, y], axis=1)), t_inv, kt, akv)
    ry = _each(_mm, a_rb, ku)
    r_hat = _each(lambda x, z: x - z[:, :hd], rt, ry)
    y_bar = _each(lambda x, z: x - z[:, hd:], arkv, ry)
    eh = _each(_mm_tn, ku, bh)
    attention_step()
    e_bar = [x[:hd] for x in eh]
    h_add = _each(lambda x, z: x - z[hd:], vkh, eh)
    s = [s_ref[h] for h in range(n_chains)]
    ys = _each(lambda rh, sh, yb: _mm_nt(rh, sh) + yb, r_hat, s, y_bar)
    s_new = _each(lambda sh, gh, eb, ha: sh * gh - _mm(sh, eb) + ha, s, g_h, e_bar, h_add)
    for h in range(n_chains):
        s_ref[h] = s_new[h]

    y = jnp.concatenate([jnp.concatenate(ys[bi * N_HEADS:(bi + 1) * N_HEADS], axis=-1) for bi in range(n_batch)],
                        axis=0)
    inv_n = 1.0 / HEAD_DIM
    mean = _split_dot(y, bd) * inv_n
    d = y - mean
    var = _split_dot(d * d, bd) * inv_n
    yn = d * lax.rsqrt(var + GN_EPS) * lnw_ref[...] + lnb_ref[...]
    bonus = _split_dot(r * k * rk_ref[...], bd) * vr
    out = (yn + bonus) * g
    for bi in range(n_batch):
        y_ref[bi] = out[bi * c:(bi + 1) * c].astype(y_ref.dtype)

    _sb_finish(units, q_block)


def _mixers(qkv3, rest3, v_first, mu_main, mu_lora, w0, a0, k_k, k_a, vres0, w_lora, bd, ln_w, ln_b, r_k):
    b, l, _ = rest3.shape
    has_vres = v_first is not None
    c = RW_CHUNK
    tq = SB_BLOCK
    n_q = l // tq
    n_units = b // 2
    assert b % 2 == 0 and l // c == 2 * n_q, "one query block of two batch elements per unit and chunk pair"
    tok = pl.BlockSpec((b, c, C_MIX), lambda ci: (0, ci, 0))
    row = lambda width: pl.BlockSpec((1, width), lambda ci: (0, 0))
    in_specs = [pl.BlockSpec((b, c, 3 * C_MIX), lambda ci: (0, ci, COL_RW // (3 * C_MIX))),
                pl.BlockSpec((b, c, LORA_W), lambda ci: (0, ci, COL_LORA // LORA_W))]
    args = [rest3, rest3]
    if has_vres:
        in_specs.append(tok)
        args.append(v_first)
    in_specs += [row(3 * C_MIX), row(LORA_W)] + [row(C_MIX)] * 4
    args += [mu_main, mu_lora, w0, a0, k_k, k_a]
    if has_vres:
        in_specs.append(row(C_MIX))
        args.append(vres0)
    in_specs += [pl.BlockSpec((LORA_W, 4 * C_MIX), lambda ci: (0, 0)), pl.BlockSpec((C_MIX, C_MIX), lambda ci: (0, 0)),
                 row(C_MIX), row(C_MIX), row(C_MIX)]
    args += [w_lora, bd, ln_w, ln_b, r_k]
    for j in range(n_units):
        which = lambda ci, j=j: 2 * j + ci // n_q
        in_specs += [pl.BlockSpec((1, tq, C_MIX), lambda ci, w=which: (w(ci), ci % n_q, 0)),
                     pl.BlockSpec((1, l, C_MIX), lambda ci, w=which: (w(ci), 0, 1), pipeline_mode=pl.Buffered(1)),
                     pl.BlockSpec((1, l, C_MIX), lambda ci, w=which: (w(ci), 0, 2), pipeline_mode=pl.Buffered(1))]
        args += [qkv3, qkv3, qkv3]
    in_specs.append(pl.BlockSpec((2 * tq, 2 * tq), lambda ci: (0, 0)))
    args.append(_later_and_ones(tq))
    out_shape = [jax.ShapeDtypeStruct((b, l, C_MIX), BF16)]
    out_specs = [tok]
    if not has_vres:
        out_shape.append(jax.ShapeDtypeStruct((b, l, C_MIX), F32))
        out_specs.append(tok)
    out_shape += [jax.ShapeDtypeStruct((2, l, C_MIX), BF16)] * n_units
    out_specs += [pl.BlockSpec((1, tq, C_MIX), lambda ci: (ci // n_q, ci % n_q, 0))] * n_units
    res = pl.pallas_call(
        functools.partial(_mixers_kernel, has_vres=has_vres, n_units=n_units, n_qblocks=n_q),
        grid=(l // c,),
        in_specs=in_specs,
        out_specs=out_specs,
        out_shape=out_shape,
        scratch_shapes=[pltpu.VMEM((b * N_HEADS, HEAD_DIM, HEAD_DIM), F32),
                        pltpu.VMEM((b, 8, 3 * C_MIX), F32), pltpu.VMEM((b, 8, LORA_W), F32)],
        compiler_params=_params("arbitrary"),
        name="mixers",
    )(*args)
    y_sb_units = [t.reshape(2 * l, C_MIX) for t in res[-n_units:]]
    return (y_sb_units, res[0], None) if has_vres else (y_sb_units, res[0], res[1])


def _merge_ffn_kernel(*refs, n_units, blocks_per_unit):
    h_ref, gsb_ref, grw_ref = refs[:3]
    ysb_refs = refs[3:3 + n_units]
    yrw_ref, wsb_ref, wrw_ref, wout_ref, g_ref, wg_ref, wu_ref, wo_ref, o_ref = refs[3 + n_units:]
    unit = pl.program_id(0) // blocks_per_unit
    y_sb = ysb_refs[0][...]
    for j in range(1, n_units):
        y_sb = jnp.where(unit == j, ysb_refs[j][...], y_sb)
    o_sb = jnp.dot(y_sb, wsb_ref[...], preferred_element_type=F32)
    o_rw = jnp.dot(yrw_ref[...], wrw_ref[...], preferred_element_type=F32)
    merged = _sigmoid(gsb_ref[...].astype(F32)) * o_sb + _sigmoid(grw_ref[...].astype(F32)) * o_rw
    x = h_ref[...] + jnp.dot(merged.astype(BF16), wout_ref[...], preferred_element_type=F32)
    hn = _rms_norm(x, g_ref[...]).astype(BF16)
    gate = jnp.dot(hn, wg_ref[...], preferred_element_type=F32)
    up = jnp.dot(hn, wu_ref[...], preferred_element_type=F32)
    act = gate * _sigmoid(gate) * up
    o_ref[...] = x + jnp.dot(act.astype(BF16), wo_ref[...], preferred_element_type=F32)


def _merge_ffn(h2, rest2, y_sb_units, y_rw, w_sb, w_rw, w_out, gain, w_ffn_in, w_ffn_out, layer):
    m = h2.shape[0]
    n_units = len(y_sb_units)
    tm = _pick(m // n_units, (384, 256, 128))
    per_unit = m // n_units // tm
    rows = lambda width: pl.BlockSpec((tm, width), lambda i: (i, 0))
    full = lambda shape, col=0: pl.BlockSpec((None,) + shape, lambda i: (layer, 0, col),
                                             pipeline_mode=pl.Buffered(1))
    unit_rows = [pl.BlockSpec((tm, C_MIX), lambda i, j=j: (jnp.clip(i - j * per_unit, 0, per_unit - 1), 0))
                 for j in range(n_units)]
    return pl.pallas_call(
        functools.partial(_merge_ffn_kernel, n_units=n_units, blocks_per_unit=per_unit),
        grid=(m // tm,),
        in_specs=[rows(D_MODEL),
                  pl.BlockSpec((tm, D_MODEL), lambda i: (i, COL_GATES // D_MODEL)),
                  pl.BlockSpec((tm, D_MODEL), lambda i: (i, COL_GATES // D_MODEL + 1)),
                  *unit_rows, rows(C_MIX),
                  full((C_MIX, D_MODEL)), full((C_MIX, D_MODEL)), full((D_MODEL, D_MODEL)),
                  pl.BlockSpec((1, D_MODEL), lambda i: (0, 0)),
                  full((D_MODEL, FFN_HIDDEN)), full((D_MODEL, FFN_HIDDEN), 1), full((FFN_HIDDEN, D_MODEL))],
        out_specs=rows(D_MODEL),
        out_shape=jax.ShapeDtypeStruct((m, D_MODEL), F32),
        compiler_params=_params("arbitrary"),
        name="merge_ffn",
    )(h2, rest2, rest2, *y_sb_units, y_rw, w_sb, w_rw, w_out, gain.reshape(1, D_MODEL),
      w_ffn_in, w_ffn_in, w_ffn_out)


def _final_norm_kernel(h_ref, g_ref, o_ref):
    o_ref[...] = _rms_norm(h_ref[...], g_ref[...])


def _final_norm(h3, gain, s):
    b = h3.shape[0]
    tm = _pick(s, (512, 256, 128, 16))
    return pl.pallas_call(
        _final_norm_kernel,
        grid=(b, s // tm),
        in_specs=[pl.BlockSpec((pl.Element(1), pl.Element(tm), pl.Element(D_MODEL)),
                               lambda bi, i: (bi, pl.multiple_of(i * tm + N_META, N_META), 0)),
                  pl.BlockSpec((1, 1, D_MODEL), lambda bi, i: (0, 0, 0))],
        out_specs=pl.BlockSpec((1, tm, D_MODEL), lambda bi, i: (bi, i, 0)),
        out_shape=jax.ShapeDtypeStruct((b, s, D_MODEL), F32),
        compiler_params=_params("arbitrary", "arbitrary"),
        name="final_norm",
    )(h3, gain.reshape(1, 1, D_MODEL))


def _head_block_diag():
    idx = jnp.arange(C_MIX) // HEAD_DIM
    return (idx[:, None] == idx[None, :]).astype(BF16)


def _lora_weight(w_up, a_up, g_up, vres_up):
    w = jnp.zeros((LORA_W, 4 * C_MIX), F32)
    o = 0
    for seg, (mat, width) in enumerate(((w_up, W_LORA), (a_up, A_LORA), (g_up, G_LORA), (vres_up, V_LORA))):
        if mat is not None:
            w = w.at[o:o + width, seg * C_MIX:(seg + 1) * C_MIX].set(mat)
        o += width
    return w.astype(BF16)


def kernel(x, meta_tokens, norm_mix, norm_ffn, norm_final, w_in, mu_rw, w0, w_up, a0, a_up, g_up, k_k, k_a, r_k, ln_x_w, ln_x_b, vres_down, vres_mu, vres_up, vres0, w_sb_out, w_rw_out, w_out, w_ffn_in, w_ffn_out):
    b, s, d = x.shape
    depth = w_in.shape[0]
    l_real = N_META + s
    l_pad = -(-l_real // SB_BLOCK) * SB_BLOCK
    meta = jnp.broadcast_to(meta_tokens.astype(x.dtype)[None], (b, N_META, d))
    h = jnp.concatenate([meta, x, jnp.zeros((b, l_pad - l_real, d), x.dtype)], axis=1)
    h2 = h.reshape(b * l_pad, d)
    bd = _head_block_diag()
    n_in = 3 * C_MIX + RW_COLS
    row = lambda vec: vec.reshape(1, -1)

    vdown_all = jnp.concatenate([jnp.zeros((1, d, V_LORA), F32), vres_down], axis=0)
    w_cat_all = _proj_weights(w_in, vdown_all)
    w_sb_all, w_rw_all, w_out_all = (w.astype(BF16) for w in (w_sb_out, w_rw_out, w_out))
    w_ffn_in_all, w_ffn_out_all = w_ffn_in.astype(BF16), w_ffn_out.astype(BF16)

    v_first = None
    for layer in range(depth):
        vmu = vres_mu[layer - 1] if layer > 0 else jnp.zeros((V_LORA,), F32)
        mu_main = row(mu_rw[layer, :3 * C_MIX])
        mu_lora = row(jnp.concatenate([mu_rw[layer, 3 * C_MIX:], vmu, jnp.zeros((LORA_W - LORA_USED,), F32)]))
        w_lora = _lora_weight(w_up[layer], a_up[layer], g_up[layer], vres_up[layer - 1] if layer > 0 else None)

        qkv2, rest2 = _norm_proj(h2, norm_mix[layer], w_cat_all, layer)
        y_sb, y_rw, v_out = _mixers(
            qkv2.reshape(b, l_pad, N_QKV), rest2.reshape(b, l_pad, N_REST), v_first, mu_main, mu_lora,
            row(w0[layer]), row(a0[layer]),
            row(k_k[layer]), row(k_a[layer]), row(vres0[layer - 1]) if layer > 0 else None, w_lora, bd,
            row(ln_x_w[layer]), row(ln_x_b[layer]), row(r_k[layer].reshape(-1)))
        if layer == 0:
            v_first = v_out
        flat = lambda t: t.reshape(b * l_pad, C_MIX)
        h2 = _merge_ffn(h2, rest2, y_sb, flat(y_rw), w_sb_all, w_rw_all, w_out_all,
                        norm_ffn[layer], w_ffn_in_all, w_ffn_out_all, layer)

    return _final_norm(h2.reshape(b, l_pad, d), norm_final, s)
```

```python
import functools

import jax
import jax.numpy as jnp
from jax import lax
from jax.experimental import pallas as pl
from jax.experimental.pallas import tpu as pltpu

D_MODEL = 1024
HEAD_DIM = 64
N_HEADS = 8
C_MIX = N_HEADS * HEAD_DIM
N_META = 16
SB_BLOCK = 128
W_LORA, A_LORA, V_LORA, G_LORA = 64, 64, 32, 160
RW_COLS = 3 * C_MIX + W_LORA + A_LORA + G_LORA
FFN_HIDDEN = 2816
RMS_EPS = 1e-6
GN_EPS = 64e-5
RW_CHUNK = 64
SB_UNDERFLOW = 104.0
SB_EAGER_BLOCKS = 2

N_QKV = 3 * C_MIX
COL_RW = 0
COL_LORA = 1536
COL_GATES = 2048
N_REST = 4096
LORA_USED = W_LORA + A_LORA + G_LORA + V_LORA
LORA_W = 384
assert W_LORA + A_LORA == 128 and 256 <= W_LORA + A_LORA + G_LORA and LORA_USED <= LORA_W

V7X_VMEM_LIMIT = 56 * 1024 * 1024

F32 = jnp.float32
BF16 = jnp.bfloat16


def _pick(n, cands):
    for c in cands:
        if n % c == 0:
            return c
    raise ValueError(f"no tile for {n} in {cands}")


def _params(*sem):
    return pltpu.CompilerParams(dimension_semantics=sem, vmem_limit_bytes=V7X_VMEM_LIMIT)


def _mm(a, b):
    return jnp.dot(a.astype(BF16), b.astype(BF16), preferred_element_type=F32)


def _mm_nt(a, b):
    return lax.dot_general(a.astype(BF16), b.astype(BF16), (((1,), (1,)), ((), ())),
                           preferred_element_type=F32)


def _mm_tn(a, b):
    return lax.dot_general(a.astype(BF16), b.astype(BF16), (((0,), (0,)), ((), ())),
                           preferred_element_type=F32)


def _each(fn, *lists):
    return [fn(*xs) for xs in zip(*lists)]


def _split_dot(x, w_bf16):
    hi = x.astype(BF16)
    lo = (x - hi.astype(F32)).astype(BF16)
    return (jnp.dot(hi, w_bf16, preferred_element_type=F32)
            + jnp.dot(lo, w_bf16, preferred_element_type=F32))


def _prefix_dot(tri_bf16, x):
    hi = x.astype(BF16)
    rem = x - hi.astype(F32)
    mid = rem.astype(BF16)
    lo = (rem - mid.astype(F32)).astype(BF16)
    return (jnp.dot(tri_bf16, hi, preferred_element_type=F32) + jnp.dot(tri_bf16, mid, preferred_element_type=F32)
            + jnp.dot(tri_bf16, lo, preferred_element_type=F32))


def _softplus(u):
    return jnp.maximum(u, 0.0) + jnp.log(1.0 + jnp.exp(-jnp.abs(u)))


def _sigmoid(u):
    return 1.0 / (1.0 + jnp.exp(-u))


def _rms_norm(x, gain):
    ms = jnp.mean(x * x, axis=-1, keepdims=True)
    return x * lax.rsqrt(ms + RMS_EPS) * gain


def _sb_unit(q, k_ref, v_ref, lm, o_ref, qi, tq):
    n_pairs = N_HEADS // 2
    pw = 2 * HEAD_DIM
    lane = lax.broadcasted_iota(jnp.int32, (1, pw), 1)
    head_a = lane < HEAD_DIM
    zero_bf = jnp.zeros((), BF16)
    split_rows = lambda x: jnp.concatenate([jnp.where(head_a, x, zero_bf), jnp.where(head_a, zero_bf, x)], axis=0)
    q = q * jnp.asarray(HEAD_DIM ** -0.5, BF16)
    q2 = [split_rows(q[:, p * pw:(p + 1) * pw]) for p in range(n_pairs)]
    row = lax.broadcasted_iota(jnp.int32, (2 * tq, tq), 0)
    col = lax.broadcasted_iota(jnp.int32, (2 * tq, tq), 1)
    causal2 = col < jnp.where(row >= tq, row - tq, row)
    state = {}

    def visit_steps(blocks, r_run, acc):
        chains = [(bi, p) for bi in range(len(blocks)) for p in range(n_pairs)]
        starts = [pl.multiple_of(blk[0] * tq, tq) for blk in blocks]
        ks = [k_ref[0, pl.ds(st, tq), :] for st in starts]
        vs = [v_ref[0, pl.ds(st, tq), :] for st in starts]
        vs = [v if blk[2] is None else jnp.where(blk[2], v, zero_bf) for v, blk in zip(vs, blocks)]
        z = [lax.dot_general(q2[p], ks[bi][:, p * pw:(p + 1) * pw], (((1,), (1,)), ((), ())),
                             preferred_element_type=F32) for bi, p in chains]
        yield
        sp = [_softplus(zi) for zi in z]
        sp = [jnp.where(causal2, s, 0.0) if blocks[bi][1] else s for s, (bi, p) in zip(sp, chains)]
        hi = [s.astype(BF16) for s in sp]
        lo = [(s - h.astype(F32)).astype(BF16) for s, h in zip(sp, hi)]
        wm = [jnp.dot(jnp.concatenate([h, l_], axis=1), lm, preferred_element_type=F32) for h, l_ in zip(hi, lo)]
        yield
        r_run, acc = list(r_run), list(acc)
        for i, (bi, p) in enumerate(chains):
            a = jnp.exp(z[i] - sp[i] - wm[i][:, :tq] - r_run[p])
            if blocks[bi][1]:
                a = jnp.where(causal2, a, 0.0)
            a = a.astype(BF16)
            v2 = split_rows(vs[bi][:, p * pw:(p + 1) * pw])
            acc[p] = acc[p] + jnp.dot(jnp.concatenate([a[:tq], a[tq:]], axis=1), v2, preferred_element_type=F32)
            r_run[p] = r_run[p] + wm[i][:, tq:]
        state["r_run"], state["acc"] = r_run, acc

    eager_blocks = [(qi, True, None)] + [(jnp.maximum(qi - back, 0), False, qi >= back)
                                         for back in range(1, SB_EAGER_BLOCKS + 1)]
    eager = visit_steps(eager_blocks, [jnp.zeros((2 * tq, tq), F32)] * n_pairs,
                        [jnp.zeros((tq, pw), F32)] * n_pairs)

    def visit(blocks, r_run, acc):
        for _ in visit_steps(blocks, r_run, acc):
            pass
        return state["r_run"], state["acc"]

    def store(acc):
        o_ref[...] = jnp.concatenate(acc, axis=1).astype(o_ref.dtype)

    return eager, state, visit, store


def _sb_finish(units, qis):
    def more(c):
        back, rrs, _ = c
        wants = [jnp.logical_and(qi - back >= 0,
                                 jnp.min(functools.reduce(jnp.minimum, [r[:, :1] for r in rr])) < SB_UNDERFLOW)
                 for qi, rr in zip(qis, rrs)]
        return functools.reduce(jnp.logical_or, wants)

    def far(c):
        back, rrs, acs = c
        new = [visit([(jnp.maximum(qi - back, 0), False, qi - back >= 0)], rr, ac)
               for (_, _, visit, _), qi, rr, ac in zip(units, qis, rrs, acs)]
        return back + 1, [n[0] for n in new], [n[1] for n in new]

    start = (SB_EAGER_BLOCKS + 1, [u[1]["r_run"] for u in units], [u[1]["acc"] for u in units])
    _, _, acs = lax.while_loop(more, far, start)
    for (_, _, _, store), acc in zip(units, acs):
        store(acc)


def _later_and_ones(tq):
    s_from = jnp.arange(2 * tq)[:, None] % tq
    s_to = jnp.arange(2 * tq)[None, :]
    return jnp.where(s_to < tq, s_from > s_to, True).astype(BF16)


def _proj_attn_kernel(h_ref, g_ref, w_ref, lm_ref, rest_ref, ysb_ref, k_scr, v_scr, *, blocks_per_batch, tq):
    tm = h_ref.shape[0]
    block = pl.program_id(0) % blocks_per_batch
    row0 = pl.multiple_of(block * tm, tq)
    xn = _rms_norm(h_ref[...], g_ref[...]).astype(BF16)
    qkv = jnp.dot(xn, w_ref[:, :N_QKV], preferred_element_type=F32).astype(BF16)
    k_scr[0, pl.ds(row0, tm), :] = qkv[:, C_MIX:2 * C_MIX]
    v_scr[0, pl.ds(row0, tm), :] = qkv[:, 2 * C_MIX:3 * C_MIX]
    lm = lm_ref[...]
    n_units = tm // tq
    qis = [block * n_units + u for u in range(n_units)]
    units = [_sb_unit(qkv[u * tq:(u + 1) * tq, 0:C_MIX], k_scr, v_scr, lm, ysb_ref.at[u], qis[u], tq)
             for u in range(n_units)]
    n_rest = rest_ref.shape[1]
    n_chunks = 4
    cw = n_rest // n_chunks
    for ch in range(n_chunks):
        if ch < 3:
            for unit in units:
                next(unit[0], None)
        cols = slice(N_QKV + ch * cw, N_QKV + (ch + 1) * cw)
        rest_ref[:, ch * cw:(ch + 1) * cw] = jnp.dot(xn, w_ref[:, cols], preferred_element_type=F32).astype(BF16)
    _sb_finish(units, qis)


def _proj_attn(h2, gain, w_cat_all, layer, l):
    m = h2.shape[0]
    n = w_cat_all.shape[2]
    tq = SB_BLOCK
    tm = _pick(l, (3 * tq, 2 * tq, tq))
    rest, y_sb = pl.pallas_call(
        functools.partial(_proj_attn_kernel, blocks_per_batch=l // tm, tq=tq),
        grid=(m // tm,),
        in_specs=[pl.BlockSpec((tm, D_MODEL), lambda i: (i, 0)),
                  pl.BlockSpec((1, D_MODEL), lambda i: (0, 0)),
                  pl.BlockSpec((None, D_MODEL, n), lambda i: (layer, 0, 0), pipeline_mode=pl.Buffered(1)),
                  pl.BlockSpec((2 * tq, 2 * tq), lambda i: (0, 0))],
        out_specs=[pl.BlockSpec((tm, n - N_QKV), lambda i: (i, 0)),
                   pl.BlockSpec((tm // tq, tq, C_MIX), lambda i: (i, 0, 0))],
        out_shape=[jax.ShapeDtypeStruct((m, n - N_QKV), BF16),
                   jax.ShapeDtypeStruct((m // tq, tq, C_MIX), BF16)],
        scratch_shapes=[pltpu.VMEM((1, l, C_MIX), BF16), pltpu.VMEM((1, l, C_MIX), BF16)],
        compiler_params=_params("arbitrary"),
        name="proj_attn",
    )(h2, gain.reshape(1, D_MODEL), w_cat_all, _later_and_ones(tq))
    return rest, y_sb.reshape(m, C_MIX)


def _unit_lower_inverse(a_strict, rowi, coli):
    eye = (rowi == coli).astype(F32)
    same = lambda sh: (rowi >> sh) == (coli >> sh)
    size = a_strict[0].shape[0]
    levels = [same(sh) for sh in range(3, size.bit_length() - 1)] + [None]
    m8 = levels[0]
    n1 = [-jnp.where(m8, a, 0.0) for a in a_strict]
    n2 = _each(_mm, n1, n1)
    n4 = _each(_mm, n2, n2)
    t = [eye + n for n in n1]
    t = _each(lambda ti, ni: ti + _mm(ti, ni), t, n2)
    t = _each(lambda ti, ni: ti + _mm(ti, ni), t, n4)
    for inner, outer in zip(levels[:-1], levels[1:]):
        off = jnp.logical_not(inner) if outer is None else jnp.logical_and(outer, jnp.logical_not(inner))
        ta = _each(lambda ti, a: _mm(ti, jnp.where(off, a, 0.0)), t, a_strict)
        t = _each(lambda ti, tai: ti - _mm(tai, ti), t, ta)
    return t


def _rw_mix_kernel(*refs, has_vres):
    if has_vres:
        (main_ref, lora_ref, vfirst_ref, mu_main_ref, mu_lora_ref, w0_ref, a0_ref, kk_ref, ka_ref, vres0_ref,
         wl_ref, bd_ref, lnw_ref, lnb_ref, rk_ref, y_ref, s_ref, pm_ref, pl_ref) = refs
        vout_ref = None
    else:
        (main_ref, lora_ref, mu_main_ref, mu_lora_ref, w0_ref, a0_ref, kk_ref, ka_ref,
         wl_ref, bd_ref, lnw_ref, lnb_ref, rk_ref, y_ref, vout_ref, s_ref, pm_ref, pl_ref) = refs
    c = RW_CHUNK
    n_batch = main_ref.shape[0]
    n_chains = n_batch * N_HEADS

    @pl.when(pl.program_id(0) == 0)
    def _():
        s_ref[...] = jnp.zeros_like(s_ref)
        pm_ref[...] = jnp.zeros_like(pm_ref)
        pl_ref[...] = jnp.zeros_like(pl_ref)

    first_row = lax.broadcasted_iota(jnp.int32, (c, 1), 0) == 0

    def shifted(x, prev8, mu):
        prev = jnp.where(first_row, prev8[7:8, :], pltpu.roll(x, shift=1, axis=0))
        return x + (prev - x) * mu

    mains = [main_ref[bi].astype(F32) for bi in range(n_batch)]
    loras = [lora_ref[bi].astype(F32) for bi in range(n_batch)]
    xs = jnp.concatenate([shifted(x, pm_ref[bi], mu_main_ref[...]) for bi, x in enumerate(mains)], axis=0)
    lo = jnp.concatenate([shifted(x, pl_ref[bi], mu_lora_ref[...]) for bi, x in enumerate(loras)], axis=0)
    for bi in range(n_batch):
        pm_ref[bi] = mains[bi][c - 8:c, :]
        pl_ref[bi] = loras[bi][c - 8:c, :]

    r = xs[:, 0:C_MIX]
    kr = xs[:, C_MIX:2 * C_MIX]
    vr = xs[:, 2 * C_MIX:3 * C_MIX]
    lane = lax.broadcasted_iota(jnp.int32, (1, 128), 1)
    g_tail = W_LORA + A_LORA + G_LORA - 256
    act = jnp.concatenate([jnp.where(lane < W_LORA, jnp.tanh(lo[:, 0:128]), lo[:, 0:128]),
                           _sigmoid(lo[:, 128:256]),
                           jnp.where(lane < g_tail, _sigmoid(lo[:, 256:384]), lo[:, 256:384])], axis=1)
    pre = jnp.dot(act.astype(BF16), wl_ref[...], preferred_element_type=F32)
    w_log = -_softplus(-(w0_ref[...] + pre[:, 0:C_MIX])) - 0.5
    ld = -jnp.exp(w_log)
    a = _sigmoid(a0_ref[...] + pre[:, C_MIX:2 * C_MIX])
    g = pre[:, 2 * C_MIX:3 * C_MIX]
    if has_vres:
        v_first = jnp.concatenate([vfirst_ref[bi] for bi in range(n_batch)], axis=0)
        vr = vr + (v_first - vr) * _sigmoid(vres0_ref[...] + pre[:, 3 * C_MIX:4 * C_MIX])
    else:
        for bi in range(n_batch):
            vout_ref[bi] = vr[bi * c:(bi + 1) * c]
    bd = bd_ref[...]
    kk = kr * kk_ref[...]
    kk = kk * lax.rsqrt(jnp.maximum(_split_dot(kk * kk, bd), 1e-24))
    k = kr * (1.0 + (a - 1.0) * ka_ref[...])
    bvec = kk * a

    rowi = lax.broadcasted_iota(jnp.int32, (c, c), 0)
    coli = lax.broadcasted_iota(jnp.int32, (c, c), 1)
    lower_incl = coli <= rowi
    lower_strict = coli < rowi
    tri = lower_incl.astype(BF16)
    rows = lambda x, bi: x[bi * c:(bi + 1) * c]
    kt_all, bt_all, kn_all, rt_all, bh_all, kh_all, g_end = [], [], [], [], [], [], []
    for bi in range(n_batch):
        ld_b = rows(ld, bi)
        cum = _prefix_dot(tri, ld_b)
        cum_end = cum[c - 1:c, :]
        e_neg = jnp.exp(-cum)
        to_end = jnp.exp(cum_end - cum)
        kt_all.append(rows(kk, bi) * jnp.exp(cum - ld_b))
        bt_all.append(rows(bvec, bi) * e_neg)
        kn_all.append(rows(k, bi) * e_neg)
        rt_all.append(rows(r, bi) * jnp.exp(cum))
        bh_all.append(rows(bvec, bi) * to_end)
        kh_all.append(rows(k, bi) * to_end)
        g_end.append(jnp.exp(cum_end))
    v_all = [rows(vr, bi) for bi in range(n_batch)]

    heads = lambda xb: [x[:, h * HEAD_DIM:(h + 1) * HEAD_DIM] for x in xb for h in range(N_HEADS)]
    kt, bt, kn, rt, bh, kh, v, g_h = (heads(x) for x in (kt_all, bt_all, kn_all, rt_all, bh_all, kh_all, v_all,
                                                        g_end))
    hd = HEAD_DIM
    kt_rt = _each(lambda x, y: jnp.concatenate([x, y], axis=0), kt, rt)
    on_b = _each(_mm_nt, kt_rt, bt)
    on_k = _each(_mm_nt, kt_rt, kn)
    a_b = [jnp.where(lower_strict, x[:c], 0.0) for x in on_b]
    a_rb = [jnp.where(lower_incl, x[c:], 0.0) for x in on_b]
    a_k = [jnp.where(lower_strict, x[:c], 0.0) for x in on_k]
    a_rk = [jnp.where(lower_incl, x[c:], 0.0) for x in on_k]
    av = _each(lambda x, y, vv: _mm(jnp.concatenate([x, y], axis=0), vv), a_k, a_rk, v)
    akv = [x[:c] for x in av]
    arkv = [x[c:] for x in av]
    vkh = _each(_mm_tn, v, kh)
    t_inv = _unit_lower_inverse(a_b, rowi, coli)
    ku =_each(lambda t_, x, y: _mm(t_, jnp.concatenate([x, y], axis=1)), t_inv, kt, akv)
    ry = _each(_mm, a_rb, ku)
    r_hat = _each(lambda x, z: x - z[:, :hd], rt, ry)
    y_bar = _each(lambda x, z: x - z[:, hd:], arkv, ry)
    eh = _each(_mm_tn, ku, bh)
    e_bar = [x[:hd] for x in eh]
    h_add = _each(lambda x, z: x - z[hd:], vkh, eh)
    s = [s_ref[h] for h in range(n_chains)]
    ys = _each(lambda rh, sh, yb: _mm_nt(rh, sh) + yb, r_hat, s, y_bar)
    s_new = _each(lambda sh, gh, eb, ha: sh * gh - _mm(sh, eb) + ha, s, g_h, e_bar, h_add)
    for h in range(n_chains):
        s_ref[h] = s_new[h]

    y = jnp.concatenate([jnp.concatenate(ys[bi * N_HEADS:(bi + 1) * N_HEADS], axis=-1) for bi in range(n_batch)],
                        axis=0)
    inv_n = 1.0 / HEAD_DIM
    mean = _split_dot(y, bd) * inv_n
    d = y - mean
    var = _split_dot(d * d, bd) * inv_n
    yn = d * lax.rsqrt(var + GN_EPS) * lnw_ref[...] + lnb_ref[...]
    bonus = _split_dot(r * k * rk_ref[...], bd) * vr
    out = (yn + bonus) * g
    for bi in range(n_batch):
        y_ref[bi] = out[bi * c:(bi + 1) * c].astype(y_ref.dtype)


def _rw_mix(rest3, v_first, mu_main, mu_lora, w0, a0, k_k, k_a, vres0, w_lora, bd, ln_w, ln_b, r_k):
    b, l, _ = rest3.shape
    has_vres = v_first is not None
    c = RW_CHUNK
    tok =pl.BlockSpec((b, c, C_MIX), lambda ci: (0, ci, 0))
    row = lambda width: pl.BlockSpec((1, width), lambda ci: (0, 0))
    in_specs = [pl.BlockSpec((b, c, 3 * C_MIX), lambda ci: (0, ci, COL_RW // (3 * C_MIX))),
                pl.BlockSpec((b, c, LORA_W), lambda ci: (0, ci, COL_LORA // LORA_W))]
    args = [rest3, rest3]
    if has_vres:
        in_specs.append(tok)
        args.append(v_first)
    in_specs += [row(3 * C_MIX), row(LORA_W)] + [row(C_MIX)] * 4
    args += [mu_main, mu_lora, w0, a0, k_k, k_a]
    if has_vres:
        in_specs.append(row(C_MIX))
        args.append(vres0)
    in_specs += [pl.BlockSpec((LORA_W, 4 * C_MIX), lambda ci: (0, 0)), pl.BlockSpec((C_MIX, C_MIX), lambda ci: (0, 0)),
                 row(C_MIX), row(C_MIX), row(C_MIX)]
    args += [w_lora, bd, ln_w, ln_b, r_k]
    out_shape = [jax.ShapeDtypeStruct((b, l, C_MIX), BF16)]
    out_specs = [tok]
    if not has_vres:
        out_shape.append(jax.ShapeDtypeStruct((b, l, C_MIX), F32))
        out_specs.append(tok)
    res = pl.pallas_call(
        functools.partial(_rw_mix_kernel, has_vres=has_vres),
        grid=(l // c,),
        in_specs=in_specs,
        out_specs=out_specs,
        out_shape=out_shape,
        scratch_shapes=[pltpu.VMEM((b * N_HEADS, HEAD_DIM, HEAD_DIM), F32),
                        pltpu.VMEM((b, 8, 3 * C_MIX), F32), pltpu.VMEM((b, 8, LORA_W), F32)],
        compiler_params=_params("arbitrary"),
        name="rw_mix",
    )(*args)
    return (res[0], None) if has_vres else (res[0], res[1])


def _merge_ffn_kernel(*refs, n_units, blocks_per_unit):
    h_ref, gsb_ref, grw_ref = refs[:3]
    ysb_refs = refs[3:3 + n_units]
    yrw_ref, wsb_ref, wrw_ref, wout_ref, g_ref, wg_ref, wu_ref, wo_ref, o_ref = refs[3 + n_units:]
    unit = pl.program_id(0) // blocks_per_unit
    y_sb = ysb_refs[0][...]
    for j in range(1, n_units):
        y_sb = jnp.where(unit == j, ysb_refs[j][...], y_sb)
    o_sb = jnp.dot(y_sb, wsb_ref[...], preferred_element_type=F32)
    o_rw = jnp.dot(yrw_ref[...], wrw_ref[...], preferred_element_type=F32)
    merged = _sigmoid(gsb_ref[...].astype(F32)) * o_sb + _sigmoid(grw_ref[...].astype(F32)) * o_rw
    x = h_ref[...] + jnp.dot(merged.astype(BF16), wout_ref[...], preferred_element_type=F32)
    hn = _rms_norm(x, g_ref[...]).astype(BF16)
    gate = jnp.dot(hn, wg_ref[...], preferred_element_type=F32)
    up = jnp.dot(hn, wu_ref[...], preferred_element_type=F32)
    act = gate * _sigmoid(gate) * up
    o_ref[...] = x + jnp.dot(act.astype(BF16), wo_ref[...], preferred_element_type=F32)


def _merge_ffn(h2, rest2, y_sb_units, y_rw, w_sb, w_rw, w_out, gain, w_ffn_in, w_ffn_out, layer):
    m = h2.shape[0]
    n_units = len(y_sb_units)
    tm = _pick(m // n_units, (384, 256, 128))
    per_unit = m // n_units // tm
    rows = lambda width: pl.BlockSpec((tm, width), lambda i: (i, 0))
    full = lambda shape, col=0: pl.BlockSpec((None,) + shape, lambda i: (layer, 0, col),
                                             pipeline_mode=pl.Buffered(1))
    unit_rows = [pl.BlockSpec((tm, C_MIX), lambda i, j=j: (jnp.clip(i - j * per_unit, 0, per_unit - 1), 0))
                 for j in range(n_units)]
    return pl.pallas_call(
        functools.partial(_merge_ffn_kernel, n_units=n_units, blocks_per_unit=per_unit),
        grid=(m // tm,),
        in_specs=[rows(D_MODEL),
                  pl.BlockSpec((tm, D_MODEL), lambda i: (i, COL_GATES // D_MODEL)),
                  pl.BlockSpec((tm, D_MODEL), lambda i: (i, COL_GATES // D_MODEL + 1)),
                  *unit_rows, rows(C_MIX),
                  full((C_MIX, D_MODEL)), full((C_MIX, D_MODEL)), full((D_MODEL, D_MODEL)),
                  pl.BlockSpec((1, D_MODEL), lambda i: (0, 0)),
                  full((D_MODEL, FFN_HIDDEN)), full((D_MODEL, FFN_HIDDEN), 1), full((FFN_HIDDEN, D_MODEL))],
        out_specs=rows(D_MODEL),
        out_shape=jax.ShapeDtypeStruct((m, D_MODEL), F32),
        compiler_params=_params("arbitrary"),
        name="merge_ffn",
    )(h2, rest2, rest2, *y_sb_units, y_rw, w_sb, w_rw, w_out, gain.reshape(1, D_MODEL),
      w_ffn_in, w_ffn_in, w_ffn_out)


def _final_norm_kernel(h_ref, g_ref, o_ref):
    o_ref[...] = _rms_norm(h_ref[...], g_ref[...])


def _final_norm(h3, gain, s):
    b = h3.shape[0]
    tm = _pick(s, (512, 256, 128, 16))
    return pl.pallas_call(
        _final_norm_kernel,
        grid=(b, s // tm),
        in_specs=[pl.BlockSpec((pl.Element(1), pl.Element(tm), pl.Element(D_MODEL)),
                               lambda bi, i: (bi, pl.multiple_of(i * tm + N_META, N_META), 0)),
                  pl.BlockSpec((1, 1, D_MODEL), lambda bi, i: (0, 0, 0))],
        out_specs=pl.BlockSpec((1, tm, D_MODEL), lambda bi, i: (bi, i, 0)),
        out_shape=jax.ShapeDtypeStruct((b, s, D_MODEL), F32),
        compiler_params=_params("arbitrary", "arbitrary"),
        name="final_norm",
    )(h3, gain.reshape(1, 1, D_MODEL))


def _head_block_diag():
    idx = jnp.arange(C_MIX) // HEAD_DIM
    return (idx[:, None] == idx[None, :]).astype(BF16)


def _lora_weight(w_up, a_up, g_up, vres_up):
    w = jnp.zeros((LORA_W, 4 * C_MIX), F32)
    o = 0
    for seg, (mat, width) in enumerate(((w_up, W_LORA), (a_up, A_LORA), (g_up, G_LORA), (vres_up, V_LORA))):
        if mat is not None:
            w = w.at[o:o + width, seg * C_MIX:(seg + 1) * C_MIX].set(mat)
        o += width
    return w.astype(BF16)


def kernel(x, meta_tokens, norm_mix, norm_ffn, norm_final, w_in, mu_rw, w0, w_up, a0, a_up, g_up, k_k, k_a, r_k, ln_x_w, ln_x_b, vres_down, vres_mu, vres_up, vres0, w_sb_out, w_rw_out, w_out, w_ffn_in, w_ffn_out):
    b, s, d = x.shape
    depth = w_in.shape[0]
    l_real = N_META + s
    l_pad = -(-l_real // SB_BLOCK) * SB_BLOCK
    meta = jnp.broadcast_to(meta_tokens.astype(x.dtype)[None], (b, N_META, d))
    h = jnp.concatenate([meta, x, jnp.zeros((b, l_pad - l_real, d), x.dtype)], axis=1)
    h2 = h.reshape(b * l_pad, d)
    bd = _head_block_diag()
    n_in = 3 * C_MIX + RW_COLS
    row = lambda vec: vec.reshape(1, -1)

    vdown_all = jnp.concatenate([jnp.zeros((1, d, V_LORA), F32), vres_down], axis=0)
    w_cat_all = jnp.concatenate([w_in[:, :, :n_in], vdown_all, jnp.zeros((depth, d, C_MIX - LORA_USED), F32),
                                 w_in[:, :, n_in:]], axis=2).astype(BF16)
    w_sb_all, w_rw_all, w_out_all = (w.astype(BF16) for w in (w_sb_out, w_rw_out, w_out))
    w_ffn_in_all, w_ffn_out_all = w_ffn_in.astype(BF16), w_ffn_out.astype(BF16)

    v_first = None
    for layer in range(depth):
        vmu = vres_mu[layer - 1] if layer > 0 else jnp.zeros((V_LORA,), F32)
        mu_main = row(mu_rw[layer, :3 * C_MIX])
        mu_lora = row(jnp.concatenate([mu_rw[layer, 3 * C_MIX:], vmu, jnp.zeros((LORA_W - LORA_USED,), F32)]))
        w_lora = _lora_weight(w_up[layer], a_up[layer], g_up[layer], vres_up[layer - 1] if layer > 0 else None)

        rest2, y_sb = _proj_attn(h2, norm_mix[layer], w_cat_all, layer, l_pad)
        y_rw, v_out = _rw_mix(
            rest2.reshape(b, l_pad, N_REST), v_first, mu_main, mu_lora,
            row(w0[layer]), row(a0[layer]),
            row(k_k[layer]), row(k_a[layer]), row(vres0[layer - 1]) if layer > 0 else None, w_lora, bd,
            row(ln_x_w[layer]), row(ln_x_b[layer]), row(r_k[layer].reshape(-1)))
        if layer == 0:
            v_first = v_out
        flat = lambda t: t.reshape(b * l_pad, C_MIX)
        h2 = _merge_ffn(h2, rest2, [y_sb], flat(y_rw), w_sb_all, w_rw_all, w_out_all,
                        norm_ffn[layer], w_ffn_in_all, w_ffn_out_all, layer)

    return _final_norm(h2.reshape(b, l_pad, d), norm_final, s)
```

```python
import functools

import jax
import jax.numpy as jnp
from jax import lax
from jax.experimental import pallas as pl
from jax.experimental.pallas import tpu as pltpu

D_MODEL = 1024
HEAD_DIM = 64
N_HEADS = 8
C_MIX = N_HEADS * HEAD_DIM
N_META = 16
SB_BLOCK = 128
W_LORA, A_LORA, V_LORA, G_LORA = 64, 64, 32, 160
RW_COLS = 3 * C_MIX + W_LORA + A_LORA + G_LORA
FFN_HIDDEN = 2816
RMS_EPS = 1e-6
GN_EPS = 64e-5
RW_CHUNK = 64
SB_UNDERFLOW = 104.0
SB_EAGER_BLOCKS = 2

N_QKV = 3 * C_MIX
COL_RW = 0
COL_LORA = 1536
COL_GATES = 2048
N_REST = 4096
LORA_USED = W_LORA + A_LORA + G_LORA + V_LORA
LORA_W = 384
assert W_LORA + A_LORA == 128 and 256 <= W_LORA + A_LORA + G_LORA and LORA_USED <= LORA_W

V7X_VMEM_LIMIT = 56 * 1024 * 1024

F32 = jnp.float32
BF16 = jnp.bfloat16


def _pick(n, cands):
    for c in cands:
        if n % c == 0:
            return c
    raise ValueError(f"no tile for {n} in {cands}")


def _params(*sem):
    return pltpu.CompilerParams(dimension_semantics=sem, vmem_limit_bytes=V7X_VMEM_LIMIT)


def _mm(a, b):
    return jnp.dot(a.astype(BF16), b.astype(BF16), preferred_element_type=F32)


def _mm_nt(a, b):
    return lax.dot_general(a.astype(BF16), b.astype(BF16), (((1,), (1,)), ((), ())),
                           preferred_element_type=F32)


def _mm_tn(a, b):
    return lax.dot_general(a.astype(BF16), b.astype(BF16), (((0,), (0,)), ((), ())),
                           preferred_element_type=F32)


def _each(fn, *lists):
    return [fn(*xs) for xs in zip(*lists)]


def _split_dot(x, w_bf16):
    hi = x.astype(BF16)
    lo = (x - hi.astype(F32)).astype(BF16)
    return (jnp.dot(hi, w_bf16, preferred_element_type=F32)
            + jnp.dot(lo, w_bf16, preferred_element_type=F32))


def _prefix_dot(tri_bf16, x):
    hi = x.astype(BF16)
    rem = x - hi.astype(F32)
    mid = rem.astype(BF16)
    lo = (rem - mid.astype(F32)).astype(BF16)
    return (jnp.dot(tri_bf16, hi, preferred_element_type=F32) + jnp.dot(tri_bf16, mid, preferred_element_type=F32)
            + jnp.dot(tri_bf16, lo, preferred_element_type=F32))


def _softplus(u):
    return jnp.maximum(u, 0.0) + jnp.log(1.0 + jnp.exp(-jnp.abs(u)))


def _sigmoid(u):
    return 1.0 / (1.0 + jnp.exp(-u))


def _rms_norm(x, gain):
    ms = jnp.mean(x * x, axis=-1, keepdims=True)
    return x * lax.rsqrt(ms + RMS_EPS) * gain


def _sb_unit(q, k_ref, v_ref, lm, o_ref, qi, tq):
    n_pairs = N_HEADS // 2
    pw = 2 * HEAD_DIM
    lane = lax.broadcasted_iota(jnp.int32, (1, pw), 1)
    head_a = lane < HEAD_DIM
    zero_bf = jnp.zeros((), BF16)
    split_rows = lambda x: jnp.concatenate([jnp.where(head_a, x, zero_bf), jnp.where(head_a, zero_bf, x)], axis=0)
    q = q * jnp.asarray(HEAD_DIM ** -0.5, BF16)
    q2 = [split_rows(q[:, p * pw:(p + 1) * pw]) for p in range(n_pairs)]
    row = lax.broadcasted_iota(jnp.int32, (2 * tq, tq), 0)
    col = lax.broadcasted_iota(jnp.int32, (2 * tq, tq), 1)
    causal2 = col < jnp.where(row >= tq, row - tq, row)
    state = {}

    def visit_steps(blocks, r_run, acc):
        chains = [(bi, p) for bi in range(len(blocks)) for p in range(n_pairs)]
        starts = [pl.multiple_of(blk[0] * tq, tq) for blk in blocks]
        ks = [k_ref[0, pl.ds(st, tq), :] for st in starts]
        vs = [v_ref[0, pl.ds(st, tq), :] for st in starts]
        vs = [v if blk[2] is None else jnp.where(blk[2], v, zero_bf) for v, blk in zip(vs, blocks)]
        z = [lax.dot_general(q2[p], ks[bi][:, p * pw:(p + 1) * pw], (((1,), (1,)), ((), ())),
                             preferred_element_type=F32) for bi, p in chains]
        yield
        sp = [_softplus(zi) for zi in z]
        sp = [jnp.where(causal2, s, 0.0) if blocks[bi][1] else s for s, (bi, p) in zip(sp, chains)]
        hi = [s.astype(BF16) for s in sp]
        lo = [(s - h.astype(F32)).astype(BF16) for s, h in zip(sp, hi)]
        wm = [jnp.dot(jnp.concatenate([h, l_], axis=1), lm, preferred_element_type=F32) for h, l_ in zip(hi, lo)]
        yield
        r_run, acc = list(r_run), list(acc)
        for i, (bi, p) in enumerate(chains):
            a = jnp.exp(z[i] - sp[i] - wm[i][:, :tq] - r_run[p])
            if blocks[bi][1]:
                a = jnp.where(causal2, a, 0.0)
            a = a.astype(BF16)
            v2 = split_rows(vs[bi][:, p * pw:(p + 1) * pw])
            acc[p] = acc[p] + jnp.dot(jnp.concatenate([a[:tq], a[tq:]], axis=1), v2, preferred_element_type=F32)
            r_run[p] = r_run[p] + wm[i][:, tq:]
        state["r_run"], state["acc"] = r_run, acc

    eager_blocks = [(qi, True, None)] + [(jnp.maximum(qi - back, 0), False, qi >= back)
                                         for back in range(1, SB_EAGER_BLOCKS + 1)]
    eager = visit_steps(eager_blocks, [jnp.zeros((2 * tq, tq), F32)] * n_pairs,
                        [jnp.zeros((tq, pw), F32)] * n_pairs)

    def visit(blocks, r_run, acc):
        for _ in visit_steps(blocks, r_run, acc):
            pass
        return state["r_run"], state["acc"]

    def store(acc):
        o_ref[...] = jnp.concatenate(acc, axis=1).astype(o_ref.dtype)

    return eager, state, visit, store


def _sb_finish(units, qis):
    def more(c):
        back, rrs, _ = c
        wants = [jnp.logical_and(qi - back >= 0,
                                 jnp.min(functools.reduce(jnp.minimum, [r[:, :1] for r in rr])) < SB_UNDERFLOW)
                 for qi, rr in zip(qis, rrs)]
        return functools.reduce(jnp.logical_or, wants)

    def far(c):
        back, rrs, acs = c
        new = [visit([(jnp.maximum(qi - back, 0), False, qi - back >= 0)], rr, ac)
               for (_, _, visit, _), qi, rr, ac in zip(units, qis, rrs, acs)]
        return back + 1, [n[0] for n in new], [n[1] for n in new]

    start = (SB_EAGER_BLOCKS + 1, [u[1]["r_run"] for u in units], [u[1]["acc"] for u in units])
    _, _, acs = lax.while_loop(more, far, start)
    for (_, _, _, store), acc in zip(units, acs):
        store(acc)


def _later_and_ones(tq):
    s_from = jnp.arange(2 * tq)[:, None] % tq
    s_to = jnp.arange(2 * tq)[None, :]
    return jnp.where(s_to < tq, s_from > s_to, True).astype(BF16)


def _proj_attn_kernel(h_ref, g_ref, w_ref, lm_ref, rest_ref, ysb_ref, k_scr, v_scr, *, blocks_per_batch, tq):
    tm = h_ref.shape[0]
    block = pl.program_id(0) % blocks_per_batch
    row0 = pl.multiple_of(block * tm, tq)
    xn = _rms_norm(h_ref[...], g_ref[...]).astype(BF16)
    qkv = jnp.dot(xn, w_ref[:, :N_QKV], preferred_element_type=F32).astype(BF16)
    k_scr[0, pl.ds(row0, tm), :] = qkv[:, C_MIX:2 * C_MIX]
    v_scr[0, pl.ds(row0, tm), :] = qkv[:, 2 * C_MIX:3 * C_MIX]
    lm = lm_ref[...]
    n_units = tm // tq
    qis = [block * n_units + u for u in range(n_units)]
    units = [_sb_unit(qkv[u * tq:(u + 1) * tq, 0:C_MIX], k_scr, v_scr, lm, ysb_ref.at[u], qis[u], tq)
             for u in range(n_units)]
    n_rest = rest_ref.shape[1]
    n_chunks = 4
    cw = n_rest // n_chunks
    for ch in range(n_chunks):
        if ch != 1:
            for unit in units:
                next(unit[0], None)
        cols = slice(N_QKV + ch * cw, N_QKV + (ch + 1) * cw)
        rest_ref[:, ch * cw:(ch + 1) * cw] = jnp.dot(xn, w_ref[:, cols], preferred_element_type=F32).astype(BF16)
    _sb_finish(units, qis)


def _proj_attn(h2, gain, w_cat_all, layer, l):
    m = h2.shape[0]
    n = w_cat_all.shape[2]
    tq = SB_BLOCK
    tm = _pick(l, (3 * tq, 2 * tq, tq))
    rest, y_sb = pl.pallas_call(
        functools.partial(_proj_attn_kernel, blocks_per_batch=l // tm, tq=tq),
        grid=(m // tm,),
        in_specs=[pl.BlockSpec((tm, D_MODEL), lambda i: (i, 0)),
                  pl.BlockSpec((1, D_MODEL), lambda i: (0, 0)),
                  pl.BlockSpec((None, D_MODEL, n), lambda i: (layer, 0, 0), pipeline_mode=pl.Buffered(1)),
                  pl.BlockSpec((2 * tq, 2 * tq), lambda i: (0, 0))],
        out_specs=[pl.BlockSpec((tm, n - N_QKV), lambda i: (i, 0)),
                   pl.BlockSpec((tm // tq, tq, C_MIX), lambda i: (i, 0, 0))],
        out_shape=[jax.ShapeDtypeStruct((m, n - N_QKV), BF16),
                   jax.ShapeDtypeStruct((m // tq, tq, C_MIX), BF16)],
        scratch_shapes=[pltpu.VMEM((1, l, C_MIX), BF16), pltpu.VMEM((1, l, C_MIX), BF16)],
        compiler_params=_params("arbitrary"),
        name="proj_attn",
    )(h2, gain.reshape(1, D_MODEL), w_cat_all, _later_and_ones(tq))
    return rest, y_sb.reshape(m, C_MIX)


def _unit_lower_inverse(a_strict, rowi, coli):
    eye = (rowi == coli).astype(F32)
    same = lambda sh: (rowi >> sh) == (coli >> sh)
    size = a_strict[0].shape[0]
    levels = [same(sh) for sh in range(3, size.bit_length() - 1)] + [None]
    m8 = levels[0]
    n1 = [-jnp.where(m8, a, 0.0) for a in a_strict]
    n2 = _each(_mm, n1, n1)
    n4 = _each(_mm, n2, n2)
    t = [eye + n for n in n1]
    t = _each(lambda ti, ni: ti + _mm(ti, ni), t, n2)
    t = _each(lambda ti, ni: ti + _mm(ti, ni), t, n4)
    for inner, outer in zip(levels[:-1], levels[1:]):
        off = jnp.logical_not(inner) if outer is None else jnp.logical_and(outer, jnp.logical_not(inner))
        ta = _each(lambda ti, a: _mm(ti, jnp.where(off, a, 0.0)), t, a_strict)
        t = _each(lambda ti, tai: ti - _mm(tai, ti), t, ta)
    return t


def _rw_mix_kernel(*refs, has_vres):
    if has_vres:
        (main_ref, lora_ref, vfirst_ref, mu_main_ref, mu_lora_ref, w0_ref, a0_ref, kk_ref, ka_ref, vres0_ref,
         wl_ref, bd_ref, lnw_ref, lnb_ref, rk_ref, y_ref, s_ref, pm_ref, pl_ref) = refs
        vout_ref = None
    else:
        (main_ref, lora_ref, mu_main_ref, mu_lora_ref, w0_ref, a0_ref, kk_ref, ka_ref,
         wl_ref, bd_ref, lnw_ref, lnb_ref, rk_ref, y_ref, vout_ref, s_ref, pm_ref, pl_ref) = refs
    c = RW_CHUNK
    n_batch = main_ref.shape[0]
    n_chains = n_batch * N_HEADS

    @pl.when(pl.program_id(0) == 0)
    def _():
        s_ref[...] = jnp.zeros_like(s_ref)
        pm_ref[...] = jnp.zeros_like(pm_ref)
        pl_ref[...] = jnp.zeros_like(pl_ref)

    first_row = lax.broadcasted_iota(jnp.int32, (c, 1), 0) == 0

    def shifted(x, prev8, mu):
        prev = jnp.where(first_row, prev8[7:8, :], pltpu.roll(x, shift=1, axis=0))
        return x + (prev - x) * mu

    mains = [main_ref[bi].astype(F32) for bi in range(n_batch)]
    loras = [lora_ref[bi].astype(F32) for bi in range(n_batch)]
    xs = jnp.concatenate([shifted(x, pm_ref[bi], mu_main_ref[...]) for bi, x in enumerate(mains)], axis=0)
    lo = jnp.concatenate([shifted(x, pl_ref[bi], mu_lora_ref[...]) for bi, x in enumerate(loras)], axis=0)
    for bi in range(n_batch):
        pm_ref[bi] = mains[bi][c - 8:c, :]
        pl_ref[bi] = loras[bi][c - 8:c, :]

    r = xs[:, 0:C_MIX]
    kr = xs[:, C_MIX:2 * C_MIX]
    vr = xs[:, 2 * C_MIX:3 * C_MIX]
    lane = lax.broadcasted_iota(jnp.int32, (1, 128), 1)
    g_tail = W_LORA + A_LORA + G_LORA - 256
    act = jnp.concatenate([jnp.where(lane < W_LORA, jnp.tanh(lo[:, 0:128]), lo[:, 0:128]),
                           _sigmoid(lo[:, 128:256]),
                           jnp.where(lane < g_tail, _sigmoid(lo[:, 256:384]), lo[:, 256:384])], axis=1)
    pre = jnp.dot(act.astype(BF16), wl_ref[...], preferred_element_type=F32)
    w_log = -_softplus(-(w0_ref[...] + pre[:, 0:C_MIX])) - 0.5
    ld = -jnp.exp(w_log)
    a = _sigmoid(a0_ref[...] + pre[:, C_MIX:2 * C_MIX])
    g = pre[:, 2 * C_MIX:3 * C_MIX]
    if has_vres:
        v_first = jnp.concatenate([vfirst_ref[bi] for bi in range(n_batch)], axis=0)
        vr = vr + (v_first - vr) * _sigmoid(vres0_ref[...] + pre[:, 3 * C_MIX:4 * C_MIX])
    else:
        for bi in range(n_batch):
            vout_ref[bi] = vr[bi * c:(bi + 1) * c]
    bd = bd_ref[...]
    kk = kr * kk_ref[...]
    kk = kk * lax.rsqrt(jnp.maximum(_split_dot(kk * kk, bd), 1e-24))
    k = kr * (1.0 + (a - 1.0) * ka_ref[...])
    bvec = kk * a

    rowi = lax.broadcasted_iota(jnp.int32, (c, c), 0)
    coli = lax.broadcasted_iota(jnp.int32, (c, c), 1)
    lower_incl = coli <= rowi
    lower_strict = coli < rowi
    tri = lower_incl.astype(BF16)
    rows = lambda x, bi: x[bi * c:(bi + 1) * c]
    kt_all, bt_all, kn_all, rt_all, bh_all, kh_all, g_end = [], [], [], [], [], [], []
    for bi in range(n_batch):
        ld_b = rows(ld, bi)
        cum = _prefix_dot(tri, ld_b)
        cum_end = cum[c - 1:c, :]
        e_neg = jnp.exp(-cum)
        to_end = jnp.exp(cum_end - cum)
        kt_all.append(rows(kk, bi) * jnp.exp(cum - ld_b))
        bt_all.append(rows(bvec, bi) * e_neg)
        kn_all.append(rows(k, bi) * e_neg)
        rt_all.append(rows(r, bi) * jnp.exp(cum))
        bh_all.append(rows(bvec, bi) * to_end)
        kh_all.append(rows(k, bi) * to_end)
        g_end.append(jnp.exp(cum_end))
    v_all = [rows(vr, bi) for bi in range(n_batch)]

    heads = lambda xb: [x[:, h * HEAD_DIM:(h + 1) * HEAD_DIM] for x in xb for h in range(N_HEADS)]
    kt, bt, kn, rt, bh, kh, v, g_h = (heads(x) for x in (kt_all, bt_all, kn_all, rt_all, bh_all, kh_all, v_all,
                                                        g_end))
    hd = HEAD_DIM
    kt_rt = _each(lambda x, y: jnp.concatenate([x, y], axis=0), kt, rt)
    on_b = _each(_mm_nt, kt_rt, bt)
    on_k = _each(_mm_nt, kt_rt, kn)
    a_b = [jnp.where(lower_strict, x[:c], 0.0) for x in on_b]
    a_rb = [jnp.where(lower_incl, x[c:], 0.0) for x in on_b]
    a_k = [jnp.where(lower_strict, x[:c], 0.0) for x in on_k]
    a_rk = [jnp.where(lower_incl, x[c:], 0.0) for x in on_k]
    av = _each(lambda x, y, vv: _mm(jnp.concatenate([x, y], axis=0), vv), a_k, a_rk, v)
    akv = [x[:c] for x in av]
    arkv = [x[c:] for x in av]
    vkh = _each(_mm_tn, v, kh)
    t_inv = _unit_lower_inverse(a_b, rowi, coli)
    ku =_each(lambda t_, x, y: _mm(t_, jnp.concatenate([x, y], axis=1)), t_inv, kt, akv)
    ry = _each(_mm, a_rb, ku)
    r_hat = _each(lambda x, z: x - z[:, :hd], rt, ry)
    y_bar = _each(lambda x, z: x - z[:, hd:], arkv, ry)
    eh = _each(_mm_tn, ku, bh)
    e_bar = [x[:hd] for x in eh]
    h_add = _each(lambda x, z: x - z[hd:], vkh, eh)
    s = [s_ref[h] for h in range(n_chains)]
    ys = _each(lambda rh, sh, yb: _mm_nt(rh, sh) + yb, r_hat, s, y_bar)
    s_new = _each(lambda sh, gh, eb, ha: sh * gh - _mm(sh, eb) + ha, s, g_h, e_bar, h_add)
    for h in range(n_chains):
        s_ref[h] = s_new[h]

    y = jnp.concatenate([jnp.concatenate(ys[bi * N_HEADS:(bi + 1) * N_HEADS], axis=-1) for bi in range(n_batch)],
                        axis=0)
    inv_n = 1.0 / HEAD_DIM
    mean = _split_dot(y, bd) * inv_n
    d = y - mean
    var = _split_dot(d * d, bd) * inv_n
    yn = d * lax.rsqrt(var + GN_EPS) * lnw_ref[...] + lnb_ref[...]
    bonus = _split_dot(r * k * rk_ref[...], bd) * vr
    out = (yn + bonus) * g
    for bi in range(n_batch):
        y_ref[bi] = out[bi * c:(bi + 1) * c].astype(y_ref.dtype)


def _rw_mix(rest3, v_first, mu_main, mu_lora, w0, a0, k_k, k_a, vres0, w_lora, bd, ln_w, ln_b, r_k):
    b, l, _ = rest3.shape
    has_vres = v_first is not None
    c = RW_CHUNK
    tok =pl.BlockSpec((b, c, C_MIX), lambda ci: (0, ci, 0))
    row = lambda width: pl.BlockSpec((1, width), lambda ci: (0, 0))
    in_specs = [pl.BlockSpec((b, c, 3 * C_MIX), lambda ci: (0, ci, COL_RW // (3 * C_MIX))),
                pl.BlockSpec((b, c, LORA_W), lambda ci: (0, ci, COL_LORA // LORA_W))]
    args = [rest3, rest3]
    if has_vres:
        in_specs.append(tok)
        args.append(v_first)
    in_specs += [row(3 * C_MIX), row(LORA_W)] + [row(C_MIX)] * 4
    args += [mu_main, mu_lora, w0, a0, k_k, k_a]
    if has_vres:
        in_specs.append(row(C_MIX))
        args.append(vres0)
    in_specs += [pl.BlockSpec((LORA_W, 4 * C_MIX), lambda ci: (0, 0)), pl.BlockSpec((C_MIX, C_MIX), lambda ci: (0, 0)),
                 row(C_MIX), row(C_MIX), row(C_MIX)]
    args += [w_lora, bd, ln_w, ln_b, r_k]
    out_shape = [jax.ShapeDtypeStruct((b, l, C_MIX), BF16)]
    out_specs = [tok]
    if not has_vres:
        out_shape.append(jax.ShapeDtypeStruct((b, l, C_MIX), F32))
        out_specs.append(tok)
    res = pl.pallas_call(
        functools.partial(_rw_mix_kernel, has_vres=has_vres),
        grid=(l // c,),
        in_specs=in_specs,
        out_specs=out_specs,
        out_shape=out_shape,
        scratch_shapes=[pltpu.VMEM((b * N_HEADS, HEAD_DIM, HEAD_DIM), F32),
                        pltpu.VMEM((b, 8, 3 * C_MIX), F32), pltpu.VMEM((b, 8, LORA_W), F32)],
        compiler_params=_params("arbitrary"),
        name="rw_mix",
    )(*args)
    return (res[0], None) if has_vres else (res[0], res[1])


def _merge_ffn_kernel(*refs, n_units, blocks_per_unit):
    h_ref, gsb_ref, grw_ref = refs[:3]
    ysb_refs = refs[3:3 + n_units]
    yrw_ref, wsb_ref, wrw_ref, wout_ref, g_ref, wg_ref, wu_ref, wo_ref, o_ref = refs[3 + n_units:]
    unit = pl.program_id(0) // blocks_per_unit
    y_sb = ysb_refs[0][...]
    for j in range(1, n_units):
        y_sb = jnp.where(unit == j, ysb_refs[j][...], y_sb)
    o_sb = jnp.dot(y_sb, wsb_ref[...], preferred_element_type=F32)
    o_rw = jnp.dot(yrw_ref[...], wrw_ref[...], preferred_element_type=F32)
    merged = _sigmoid(gsb_ref[...].astype(F32)) * o_sb + _sigmoid(grw_ref[...].astype(F32)) * o_rw
    x = h_ref[...] + jnp.dot(merged.astype(BF16), wout_ref[...], preferred_element_type=F32)
    hn = _rms_norm(x, g_ref[...]).astype(BF16)
    gate = jnp.dot(hn, wg_ref[...], preferred_element_type=F32)
    up = jnp.dot(hn, wu_ref[...], preferred_element_type=F32)
    act = gate * _sigmoid(gate) * up
    o_ref[...] = x + jnp.dot(act.astype(BF16), wo_ref[...], preferred_element_type=F32)


def _merge_ffn(h2, rest2, y_sb_units, y_rw, w_sb, w_rw, w_out, gain, w_ffn_in, w_ffn_out, layer):
    m = h2.shape[0]
    n_units = len(y_sb_units)
    tm = _pick(m // n_units, (384, 256, 128))
    per_unit = m // n_units // tm
    rows = lambda width: pl.BlockSpec((tm, width), lambda i: (i, 0))
    full = lambda shape, col=0: pl.BlockSpec((None,) + shape, lambda i: (layer, 0, col),
                                             pipeline_mode=pl.Buffered(1))
    unit_rows = [pl.BlockSpec((tm, C_MIX), lambda i, j=j: (jnp.clip(i - j * per_unit, 0, per_unit - 1), 0))
                 for j in range(n_units)]
    return pl.pallas_call(
        functools.partial(_merge_ffn_kernel, n_units=n_units, blocks_per_unit=per_unit),
        grid=(m // tm,),
        in_specs=[rows(D_MODEL),
                  pl.BlockSpec((tm, D_MODEL), lambda i: (i, COL_GATES // D_MODEL)),
                  pl.BlockSpec((tm, D_MODEL), lambda i: (i, COL_GATES // D_MODEL + 1)),
                  *unit_rows, rows(C_MIX),
                  full((C_MIX, D_MODEL)), full((C_MIX, D_MODEL)), full((D_MODEL, D_MODEL)),
                  pl.BlockSpec((1, D_MODEL), lambda i: (0, 0)),
                  full((D_MODEL, FFN_HIDDEN)), full((D_MODEL, FFN_HIDDEN), 1), full((FFN_HIDDEN, D_MODEL))],
        out_specs=rows(D_MODEL),
        out_shape=jax.ShapeDtypeStruct((m, D_MODEL), F32),
        compiler_params=_params("arbitrary"),
        name="merge_ffn",
    )(h2, rest2, rest2, *y_sb_units, y_rw, w_sb, w_rw, w_out, gain.reshape(1, D_MODEL),
      w_ffn_in, w_ffn_in, w_ffn_out)


def _final_norm_kernel(h_ref, g_ref, o_ref):
    o_ref[...] = _rms_norm(h_ref[...], g_ref[...])


def _final_norm(h3, gain, s):
    b = h3.shape[0]
    tm = _pick(s, (512, 256, 128, 16))
    return pl.pallas_call(
        _final_norm_kernel,
        grid=(b, s // tm),
        in_specs=[pl.BlockSpec((pl.Element(1), pl.Element(tm), pl.Element(D_MODEL)),
                               lambda bi, i: (bi, pl.multiple_of(i * tm + N_META, N_META), 0)),
                  pl.BlockSpec((1, 1, D_MODEL), lambda bi, i: (0, 0, 0))],
        out_specs=pl.BlockSpec((1, tm, D_MODEL), lambda bi, i: (bi, i, 0)),
        out_shape=jax.ShapeDtypeStruct((b, s, D_MODEL), F32),
        compiler_params=_params("arbitrary", "arbitrary"),
        name="final_norm",
    )(h3, gain.reshape(1, 1, D_MODEL))


def _head_block_diag():
    idx = jnp.arange(C_MIX) // HEAD_DIM
    return (idx[:, None] == idx[None, :]).astype(BF16)


def _lora_weight(w_up, a_up, g_up, vres_up):
    w = jnp.zeros((LORA_W, 4 * C_MIX), F32)
    o = 0
    for seg, (mat, width) in enumerate(((w_up, W_LORA), (a_up, A_LORA), (g_up, G_LORA), (vres_up, V_LORA))):
        if mat is not None:
            w = w.at[o:o + width, seg * C_MIX:(seg + 1) * C_MIX].set(mat)
        o += width
    return w.astype(BF16)


def kernel(x, meta_tokens, norm_mix, norm_ffn, norm_final, w_in, mu_rw, w0, w_up, a0, a_up, g_up, k_k, k_a, r_k, ln_x_w, ln_x_b, vres_down, vres_mu, vres_up, vres0, w_sb_out, w_rw_out, w_out, w_ffn_in, w_ffn_out):
    b, s, d = x.shape
    depth = w_in.shape[0]
    l_real = N_META + s
    l_pad = -(-l_real // SB_BLOCK) * SB_BLOCK
    meta = jnp.broadcast_to(meta_tokens.astype(x.dtype)[None], (b, N_META, d))
    h = jnp.concatenate([meta, x, jnp.zeros((b, l_pad - l_real, d), x.dtype)], axis=1)
    h2 = h.reshape(b * l_pad, d)
    bd = _head_block_diag()
    n_in = 3 * C_MIX + RW_COLS
    row = lambda vec: vec.reshape(1, -1)

    vdown_all = jnp.concatenate([jnp.zeros((1, d, V_LORA), F32), vres_down], axis=0)
    w_cat_all = jnp.concatenate([w_in[:, :, :n_in], vdown_all, jnp.zeros((depth, d, C_MIX - LORA_USED), F32),
                                 w_in[:, :, n_in:]], axis=2).astype(BF16)
    w_sb_all, w_rw_all, w_out_all = (w.astype(BF16) for w in (w_sb_out, w_rw_out, w_out))
    w_ffn_in_all, w_ffn_out_all = w_ffn_in.astype(BF16), w_ffn_out.astype(BF16)

    v_first = None
    for layer in range(depth):
        vmu = vres_mu[layer - 1] if layer > 0 else jnp.zeros((V_LORA,), F32)
        mu_main = row(mu_rw[layer, :3 * C_MIX])
        mu_lora = row(jnp.concatenate([mu_rw[layer, 3 * C_MIX:], vmu, jnp.zeros((LORA_W - LORA_USED,), F32)]))
        w_lora = _lora_weight(w_up[layer], a_up[layer], g_up[layer], vres_up[layer - 1] if layer > 0 else None)

        rest2, y_sb = _proj_attn(h2, norm_mix[layer], w_cat_all, layer, l_pad)
        y_rw, v_out = _rw_mix(
            rest2.reshape(b, l_pad, N_REST), v_first, mu_main, mu_lora,
            row(w0[layer]), row(a0[layer]),
            row(k_k[layer]), row(k_a[layer]), row(vres0[layer - 1]) if layer > 0 else None, w_lora, bd,
            row(ln_x_w[layer]), row(ln_x_b[layer]), row(r_k[layer].reshape(-1)))
        if layer == 0:
            v_first = v_out
        flat = lambda t: t.reshape(b * l_pad, C_MIX)
        h2 = _merge_ffn(h2, rest2, [y_sb], flat(y_rw), w_sb_all, w_rw_all, w_out_all,
                        norm_ffn[layer], w_ffn_in_all, w_ffn_out_all, layer)

    return _final_norm(h2.reshape(b, l_pad, d), norm_final, s)
```

```python
import functools

import jax
import jax.numpy as jnp
from jax import lax
from jax.experimental import pallas as pl
from jax.experimental.pallas import tpu as pltpu

D_MODEL = 1024
HEAD_DIM = 64
N_HEADS = 8
C_MIX = N_HEADS * HEAD_DIM
N_META = 16
SB_BLOCK = 128
W_LORA, A_LORA, V_LORA, G_LORA = 64, 64, 32, 160
RW_COLS = 3 * C_MIX + W_LORA + A_LORA + G_LORA
FFN_HIDDEN = 2816
RMS_EPS = 1e-6
GN_EPS = 64e-5
RW_CHUNK = 64
SB_UNDERFLOW = 104.0
SB_EAGER_BLOCKS = 2

N_QKV = 3 * C_MIX
COL_RW = 0
COL_LORA = 1536
COL_GATES = 2048
N_REST = 4096
LORA_USED = W_LORA + A_LORA + G_LORA + V_LORA
LORA_W = 384
assert W_LORA + A_LORA == 128 and 256 <= W_LORA + A_LORA + G_LORA and LORA_USED <= LORA_W

V7X_VMEM_LIMIT = 56 * 1024 * 1024

F32 = jnp.float32
BF16 = jnp.bfloat16


def _pick(n, cands):
    for c in cands:
        if n % c == 0:
            return c
    raise ValueError(f"no tile for {n} in {cands}")


def _params(*sem):
    return pltpu.CompilerParams(dimension_semantics=sem, vmem_limit_bytes=V7X_VMEM_LIMIT)


def _mm(a, b):
    return jnp.dot(a.astype(BF16), b.astype(BF16), preferred_element_type=F32)


def _mm_nt(a, b):
    return lax.dot_general(a.astype(BF16), b.astype(BF16), (((1,), (1,)), ((), ())),
                           preferred_element_type=F32)


def _mm_tn(a, b):
    return lax.dot_general(a.astype(BF16), b.astype(BF16), (((0,), (0,)), ((), ())),
                           preferred_element_type=F32)


def _each(fn, *lists):
    return [fn(*xs) for xs in zip(*lists)]


def _split_dot(x, w_bf16):
    hi = x.astype(BF16)
    lo = (x - hi.astype(F32)).astype(BF16)
    return (jnp.dot(hi, w_bf16, preferred_element_type=F32)
            + jnp.dot(lo, w_bf16, preferred_element_type=F32))


def _prefix_dot(tri_bf16, x):
    hi = x.astype(BF16)
    rem = x - hi.astype(F32)
    mid = rem.astype(BF16)
    lo = (rem - mid.astype(F32)).astype(BF16)
    return (jnp.dot(tri_bf16, hi, preferred_element_type=F32) + jnp.dot(tri_bf16, mid, preferred_element_type=F32)
            + jnp.dot(tri_bf16, lo, preferred_element_type=F32))


def _softplus(u):
    return jnp.maximum(u, 0.0) + jnp.log(1.0 + jnp.exp(-jnp.abs(u)))


def _sigmoid(u):
    return 1.0 / (1.0 + jnp.exp(-u))


def _rms_norm(x, gain):
    ms = jnp.mean(x * x, axis=-1, keepdims=True)
    return x * lax.rsqrt(ms + RMS_EPS) * gain


def _sb_unit(q, k_ref, v_ref, lm, o_ref, qi, tq):
    n_pairs = N_HEADS // 2
    pw = 2 * HEAD_DIM
    lane = lax.broadcasted_iota(jnp.int32, (1, pw), 1)
    head_a = lane < HEAD_DIM
    zero_bf = jnp.zeros((), BF16)
    split_rows = lambda x: jnp.concatenate([jnp.where(head_a, x, zero_bf), jnp.where(head_a, zero_bf, x)], axis=0)
    q = q * jnp.asarray(HEAD_DIM ** -0.5, BF16)
    q2 = [split_rows(q[:, p * pw:(p + 1) * pw]) for p in range(n_pairs)]
    row = lax.broadcasted_iota(jnp.int32, (2 * tq, tq), 0)
    col = lax.broadcasted_iota(jnp.int32, (2 * tq, tq), 1)
    causal2 = col < jnp.where(row >= tq, row - tq, row)
    state = {}

    def visit_steps(blocks, r_run, acc):
        chains = [(bi, p) for bi in range(len(blocks)) for p in range(n_pairs)]
        starts = [pl.multiple_of(blk[0] * tq, tq) for blk in blocks]
        ks = [k_ref[0, pl.ds(st, tq), :] for st in starts]
        vs = [v_ref[0, pl.ds(st, tq), :] for st in starts]
        vs = [v if blk[2] is None else jnp.where(blk[2], v, zero_bf) for v, blk in zip(vs, blocks)]
        z = [lax.dot_general(q2[p], ks[bi][:, p * pw:(p + 1) * pw], (((1,), (1,)), ((), ())),
                             preferred_element_type=F32) for bi, p in chains]
        yield
        sp = [_softplus(zi) for zi in z]
        sp = [jnp.where(causal2, s, 0.0) if blocks[bi][1] else s for s, (bi, p) in zip(sp, chains)]
        hi = [s.astype(BF16) for s in sp]
        lo = [(s - h.astype(F32)).astype(BF16) for s, h in zip(sp, hi)]
        wm = [jnp.dot(jnp.concatenate([h, l_], axis=1), lm, preferred_element_type=F32) for h, l_ in zip(hi, lo)]
        yield
        r_run, acc = list(r_run), list(acc)
        for i, (bi, p) in enumerate(chains):
            a = jnp.exp(z[i] - sp[i] - wm[i][:, :tq] - r_run[p])
            if blocks[bi][1]:
                a = jnp.where(causal2, a, 0.0)
            a = a.astype(BF16)
            v2 = split_rows(vs[bi][:, p * pw:(p + 1) * pw])
            acc[p] = acc[p] + jnp.dot(jnp.concatenate([a[:tq], a[tq:]], axis=1), v2, preferred_element_type=F32)
            r_run[p] = r_run[p] + wm[i][:, tq:]
        state["r_run"], state["acc"] = r_run, acc

    eager_blocks = [(qi, True, None)] + [(jnp.maximum(qi - back, 0), False, qi >= back)
                                         for back in range(1, SB_EAGER_BLOCKS + 1)]
    eager = visit_steps(eager_blocks, [jnp.zeros((2 * tq, tq), F32)] * n_pairs,
                        [jnp.zeros((tq, pw), F32)] * n_pairs)

    def visit(blocks, r_run, acc):
        for _ in visit_steps(blocks, r_run, acc):
            pass
        return state["r_run"], state["acc"]

    def store(acc):
        o_ref[...] = jnp.concatenate(acc, axis=1).astype(o_ref.dtype)

    return eager, state, visit, store


def _sb_finish(units, qis):
    def more(c):
        back, rrs, _ = c
        wants = [jnp.logical_and(qi - back >= 0,
                                 jnp.min(functools.reduce(jnp.minimum, [r[:, :1] for r in rr])) < SB_UNDERFLOW)
                 for qi, rr in zip(qis, rrs)]
        return functools.reduce(jnp.logical_or, wants)

    def far(c):
        back, rrs, acs = c
        new = [visit([(jnp.maximum(qi - back, 0), False, qi - back >= 0)], rr, ac)
               for (_, _, visit, _), qi, rr, ac in zip(units, qis, rrs, acs)]
        return back + 1, [n[0] for n in new], [n[1] for n in new]

    start = (SB_EAGER_BLOCKS + 1, [u[1]["r_run"] for u in units], [u[1]["acc"] for u in units])
    _, _, acs = lax.while_loop(more, far, start)
    for (_, _, _, store), acc in zip(units, acs):
        store(acc)


def _later_and_ones(tq):
    s_from = jnp.arange(2 * tq)[:, None] % tq
    s_to = jnp.arange(2 * tq)[None, :]
    return jnp.where(s_to < tq, s_from > s_to, True).astype(BF16)


def _proj_attn_kernel(h_ref, g_ref, w_ref, lm_ref, rest_ref, ysb_ref, k_scr, v_scr, *, blocks_per_batch, tq):
    tm = h_ref.shape[0]
    block = pl.program_id(0) % blocks_per_batch
    row0 = pl.multiple_of(block * tm, tq)
    xn = _rms_norm(h_ref[...], g_ref[...]).astype(BF16)
    qkv = jnp.dot(xn, w_ref[:, :N_QKV], preferred_element_type=F32).astype(BF16)
    k_scr[0, pl.ds(row0, tm), :] = qkv[:, C_MIX:2 * C_MIX]
    v_scr[0, pl.ds(row0, tm), :] = qkv[:, 2 * C_MIX:3 * C_MIX]
    lm = lm_ref[...]
    n_units = tm // tq
    qis = [block * n_units + u for u in range(n_units)]
    units = [_sb_unit(qkv[u * tq:(u + 1) * tq, 0:C_MIX], k_scr, v_scr, lm, ysb_ref.at[u], qis[u], tq)
             for u in range(n_units)]
    n_rest = rest_ref.shape[1]
    n_chunks = 4
    cw = n_rest // n_chunks
    for ch in range(n_chunks):
        if ch < 3:
            for unit in units:
                next(unit[0], None)
        cols = slice(N_QKV + ch * cw, N_QKV + (ch + 1) * cw)
        rest_ref[:, ch * cw:(ch + 1) * cw] = jnp.dot(xn, w_ref[:, cols], preferred_element_type=F32).astype(BF16)
    _sb_finish(units, qis)


def _proj_attn(h2, gain, w_cat_all, layer, l):
    m = h2.shape[0]
    n = w_cat_all.shape[2]
    tq = SB_BLOCK
    tm = _pick(l, (3 * tq, 2 * tq, tq))
    rest, y_sb = pl.pallas_call(
        functools.partial(_proj_attn_kernel, blocks_per_batch=l // tm, tq=tq),
        grid=(m // tm,),
        in_specs=[pl.BlockSpec((tm, D_MODEL), lambda i: (i, 0)),
                  pl.BlockSpec((1, D_MODEL), lambda i: (0, 0)),
                  pl.BlockSpec((None, D_MODEL, n), lambda i: (layer, 0, 0), pipeline_mode=pl.Buffered(1)),
                  pl.BlockSpec((2 * tq, 2 * tq), lambda i: (0, 0))],
        out_specs=[pl.BlockSpec((tm, n - N_QKV), lambda i: (i, 0)),
                   pl.BlockSpec((tm // tq, tq, C_MIX), lambda i: (i, 0, 0))],
        out_shape=[jax.ShapeDtypeStruct((m, n - N_QKV), BF16),
                   jax.ShapeDtypeStruct((m // tq, tq, C_MIX), BF16)],
        scratch_shapes=[pltpu.VMEM((1, l, C_MIX), BF16), pltpu.VMEM((1, l, C_MIX), BF16)],
        compiler_params=_params("arbitrary"),
        name="proj_attn",
    )(h2, gain.reshape(1, D_MODEL), w_cat_all, _later_and_ones(tq))
    return rest, y_sb.reshape(m, C_MIX)


def _unit_lower_inverse(a_strict, rowi, coli):
    eye = (rowi == coli).astype(F32)
    same = lambda sh: (rowi >> sh) == (coli >> sh)
    size = a_strict[0].shape[0]
    levels = [same(sh) for sh in range(3, size.bit_length() - 1)] + [None]
    m8 = levels[0]
    n1 = [-jnp.where(m8, a, 0.0) for a in a_strict]
    n2 = _each(_mm, n1, n1)
    n4 = _each(_mm, n2, n2)
    t = [eye + n for n in n1]
    t = _each(lambda ti, ni: ti + _mm(ti, ni), t, n2)
    t = _each(lambda ti, ni: ti + _mm(ti, ni), t, n4)
    for inner, outer in zip(levels[:-1], levels[1:]):
        off = jnp.logical_not(inner) if outer is None else jnp.logical_and(outer, jnp.logical_not(inner))
        ta = _each(lambda ti, a: _mm(ti, jnp.where(off, a, 0.0)), t, a_strict)
        t = _each(lambda ti, tai: ti - _mm(tai, ti), t, ta)
    return t


def _rw_mix_kernel(*refs, has_vres):
    if has_vres:
        (main_ref, lora_ref, vfirst_ref, mu_main_ref, mu_lora_ref, w0_ref, a0_ref, kk_ref, ka_ref, vres0_ref,
         wl_ref, bd_ref, lnw_ref, lnb_ref, rk_ref, y_ref, s_ref, pm_ref, pl_ref) = refs
        vout_ref = None
    else:
        (main_ref, lora_ref, mu_main_ref, mu_lora_ref, w0_ref, a0_ref, kk_ref, ka_ref,
         wl_ref, bd_ref, lnw_ref, lnb_ref, rk_ref, y_ref, vout_ref, s_ref, pm_ref, pl_ref) = refs
    c = RW_CHUNK
    n_batch = main_ref.shape[0]
    n_chains = n_batch * N_HEADS

    @pl.when(pl.program_id(0) == 0)
    def _():
        s_ref[...] = jnp.zeros_like(s_ref)
        pm_ref[...] = jnp.zeros_like(pm_ref)
        pl_ref[...] = jnp.zeros_like(pl_ref)

    first_row = lax.broadcasted_iota(jnp.int32, (c, 1), 0) == 0

    def shifted(x, prev8, mu):
        prev = jnp.where(first_row, prev8[7:8, :], pltpu.roll(x, shift=1, axis=0))
        return x + (prev - x) * mu

    mains = [main_ref[bi].astype(F32) for bi in range(n_batch)]
    loras = [lora_ref[bi].astype(F32) for bi in range(n_batch)]
    xs = jnp.concatenate([shifted(x, pm_ref[bi], mu_main_ref[...]) for bi, x in enumerate(mains)], axis=0)
    lo = jnp.concatenate([shifted(x, pl_ref[bi], mu_lora_ref[...]) for bi, x in enumerate(loras)], axis=0)
    for bi in range(n_batch):
        pm_ref[bi] = mains[bi][c - 8:c, :]
        pl_ref[bi] = loras[bi][c - 8:c, :]

    r = xs[:, 0:C_MIX]
    kr = xs[:, C_MIX:2 * C_MIX]
    vr = xs[:, 2 * C_MIX:3 * C_MIX]
    lane = lax.broadcasted_iota(jnp.int32, (1, 128), 1)
    g_tail = W_LORA + A_LORA + G_LORA - 256
    act = jnp.concatenate([jnp.where(lane < W_LORA, jnp.tanh(lo[:, 0:128]), lo[:, 0:128]),
                           _sigmoid(lo[:, 128:256]),
                           jnp.where(lane < g_tail, _sigmoid(lo[:, 256:384]), lo[:, 256:384])], axis=1)
    pre = jnp.dot(act.astype(BF16), wl_ref[...], preferred_element_type=F32)
    w_log = -_softplus(-(w0_ref[...] + pre[:, 0:C_MIX])) - 0.5
    ld = -jnp.exp(w_log)
    a = _sigmoid(a0_ref[...] + pre[:, C_MIX:2 * C_MIX])
    g = pre[:, 2 * C_MIX:3 * C_MIX]
    if has_vres:
        v_first = jnp.concatenate([vfirst_ref[bi] for bi in range(n_batch)], axis=0)
        vr = vr + (v_first - vr) * _sigmoid(vres0_ref[...] + pre[:, 3 * C_MIX:4 * C_MIX])
    else:
        for bi in range(n_batch):
            vout_ref[bi] = vr[bi * c:(bi + 1) * c]
    bd = bd_ref[...]
    kk = kr * kk_ref[...]
    kk = kk * lax.rsqrt(jnp.maximum(_split_dot(kk * kk, bd), 1e-24))
    k = kr * (1.0 + (a - 1.0) * ka_ref[...])
    bvec = kk * a

    rowi = lax.broadcasted_iota(jnp.int32, (c, c), 0)
    coli = lax.broadcasted_iota(jnp.int32, (c, c), 1)
    lower_incl = coli <= rowi
    lower_strict = coli < rowi
    tri = lower_incl.astype(BF16)
    rows = lambda x, bi: x[bi * c:(bi + 1) * c]
    kt_all, bt_all, kn_all, rt_all, bh_all, kh_all, g_end = [], [], [], [], [], [], []
    for bi in range(n_batch):
        ld_b = rows(ld, bi)
        cum = _prefix_dot(tri, ld_b)
        cum_end = cum[c - 1:c, :]
        e_neg = jnp.exp(-cum)
        to_end = jnp.exp(cum_end - cum)
        kt_all.append(rows(kk, bi) * jnp.exp(cum - ld_b))
        bt_all.append(rows(bvec, bi) * e_neg)
        kn_all.append(rows(k, bi) * e_neg)
        rt_all.append(rows(r, bi) * jnp.exp(cum))
        bh_all.append(rows(bvec, bi) * to_end)
        kh_all.append(rows(k, bi) * to_end)
        g_end.append(jnp.exp(cum_end))
    v_all = [rows(vr, bi) for bi in range(n_batch)]

    heads = lambda xb: [x[:, h * HEAD_DIM:(h + 1) * HEAD_DIM] for x in xb for h in range(N_HEADS)]
    kt, bt, kn, rt, bh, kh, v, g_h = (heads(x) for x in (kt_all, bt_all, kn_all, rt_all, bh_all, kh_all, v_all,
                                                        g_end))
    hd = HEAD_DIM
    kt_rt = _each(lambda x, y: jnp.concatenate([x, y], axis=0), kt, rt)
    on_b = _each(_mm_nt, kt_rt, bt)
    on_k = _each(_mm_nt, kt_rt, kn)
    a_b = [jnp.where(lower_strict, x[:c], 0.0) for x in on_b]
    a_rb = [jnp.where(lower_incl, x[c:], 0.0) for x in on_b]
    a_k = [jnp.where(lower_strict, x[:c], 0.0) for x in on_k]
    a_rk = [jnp.where(lower_incl, x[c:], 0.0) for x in on_k]
    av = _each(lambda x, y, vv: _mm(jnp.concatenate([x, y], axis=0), vv), a_k, a_rk, v)
    akv = [x[:c] for x in av]
    arkv = [x[c:] for x in av]
    vkh = _each(_mm_tn, v, kh)
    t_inv = _unit_lower_inverse(a_b, rowi, coli)
    ku =_each(lambda t_, x, y: _mm(t_, jnp.concatenate([x, y], axis=1)), t_inv, kt, akv)
    ry = _each(_mm, a_rb, ku)
    r_hat = _each(lambda x, z: x - z[:, :hd], rt, ry)
    y_bar = _each(lambda x, z: x - z[:, hd:], arkv, ry)
    eh = _each(_mm_tn, ku, bh)
    e_bar = [x[:hd] for x in eh]
    h_add = _each(lambda x, z: x - z[hd:], vkh, eh)
    s = [s_ref[h] for h in range(n_chains)]
    ys = _each(lambda rh, sh, yb: _mm_nt(rh, sh) + yb, r_hat, s, y_bar)
    s_new = _each(lambda sh, gh, eb, ha: sh * gh - _mm(sh, eb) + ha, s, g_h, e_bar, h_add)
    for h in range(n_chains):
        s_ref[h] = s_new[h]

    y = jnp.concatenate([jnp.concatenate(ys[bi * N_HEADS:(bi + 1) * N_HEADS], axis=-1) for bi in range(n_batch)],
                        axis=0)
    inv_n = 1.0 / HEAD_DIM
    mean = _split_dot(y, bd) * inv_n
    d = y - mean
    var = _split_dot(d * d, bd) * inv_n
    yn = d * lax.rsqrt(var + GN_EPS) * lnw_ref[...] + lnb_ref[...]
    bonus = _split_dot(r * k * rk_ref[...], bd) * vr
    out = (yn + bonus) * g
    for bi in range(n_batch):
        y_ref[bi] = out[bi * c:(bi + 1) * c].astype(y_ref.dtype)


def _rw_mix(rest3, v_first, mu_main, mu_lora, w0, a0, k_k, k_a, vres0, w_lora, bd, ln_w, ln_b, r_k):
    b, l, _ = rest3.shape
    has_vres = v_first is not None
    c = RW_CHUNK
    tok =pl.BlockSpec((b, c, C_MIX), lambda ci: (0, ci, 0))
    row = lambda width: pl.BlockSpec((1, width), lambda ci: (0, 0))
    in_specs = [pl.BlockSpec((b, c, 3 * C_MIX), lambda ci: (0, ci, COL_RW // (3 * C_MIX))),
                pl.BlockSpec((b, c, LORA_W), lambda ci: (0, ci, COL_LORA // LORA_W))]
    args = [rest3, rest3]
    if has_vres:
        in_specs.append(tok)
        args.append(v_first)
    in_specs += [row(3 * C_MIX), row(LORA_W)] + [row(C_MIX)] * 4
    args += [mu_main, mu_lora, w0, a0, k_k, k_a]
    if has_vres:
        in_specs.append(row(C_MIX))
        args.append(vres0)
    in_specs += [pl.BlockSpec((LORA_W, 4 * C_MIX), lambda ci: (0, 0)), pl.BlockSpec((C_MIX, C_MIX), lambda ci: (0, 0)),
                 row(C_MIX), row(C_MIX), row(C_MIX)]
    args += [w_lora, bd, ln_w, ln_b, r_k]
    out_shape = [jax.ShapeDtypeStruct((b, l, C_MIX), BF16)]
    out_specs = [tok]
    if not has_vres:
        out_shape.append(jax.ShapeDtypeStruct((b, l, C_MIX), F32))
        out_specs.append(tok)
    res = pl.pallas_call(
        functools.partial(_rw_mix_kernel, has_vres=has_vres),
        grid=(l // c,),
        in_specs=in_specs,
        out_specs=out_specs,
        out_shape=out_shape,
        scratch_shapes=[pltpu.VMEM((b * N_HEADS, HEAD_DIM, HEAD_DIM), F32),
                        pltpu.VMEM((b, 8, 3 * C_MIX), F32), pltpu.VMEM((b, 8, LORA_W), F32)],
        compiler_params=_params("arbitrary"),
        name="rw_mix",
    )(*args)
    return (res[0], None) if has_vres else (res[0], res[1])


def _merge_ffn_kernel(h_ref, gsb_ref, grw_ref, ysb_ref, yrw_ref, wsb_ref, wrw_ref, wout_ref,
                      g_ref, wg_ref, wu_ref, wo_ref, o_ref):
    o_sb = jnp.dot(ysb_ref[...], wsb_ref[...], preferred_element_type=F32)
    o_rw = jnp.dot(yrw_ref[...], wrw_ref[...], preferred_element_type=F32)
    merged = _sigmoid(gsb_ref[...].astype(F32)) * o_sb + _sigmoid(grw_ref[...].astype(F32)) * o_rw
    x = h_ref[...] + jnp.dot(merged.astype(BF16), wout_ref[...], preferred_element_type=F32)
    hn = _rms_norm(x, g_ref[...]).astype(BF16)
    gate = jnp.dot(hn, wg_ref[...], preferred_element_type=F32)
    up = jnp.dot(hn, wu_ref[...], preferred_element_type=F32)
    act = gate * _sigmoid(gate) * up
    o_ref[...] = x + jnp.dot(act.astype(BF16), wo_ref[...], preferred_element_type=F32)


def _merge_ffn(h2, rest2, y_sb, y_rw, w_sb, w_rw, w_out, gain, w_ffn_in, w_ffn_out, layer):
    m = h2.shape[0]
    tm = _pick(m, (384, 256, 128))
    rows = lambda width: pl.BlockSpec((tm, width), lambda i: (i, 0))
    full = lambda shape, col=0: pl.BlockSpec((None,) + shape, lambda i: (layer, 0, col),
                                             pipeline_mode=pl.Buffered(1))
    return pl.pallas_call(
        _merge_ffn_kernel,
        grid=(m // tm,),
        in_specs=[rows(D_MODEL),
                  pl.BlockSpec((tm, D_MODEL), lambda i: (i, COL_GATES // D_MODEL)),
                  pl.BlockSpec((tm, D_MODEL), lambda i: (i, COL_GATES // D_MODEL + 1)),
                  rows(C_MIX), rows(C_MIX),
                  full((C_MIX, D_MODEL)), full((C_MIX, D_MODEL)), full((D_MODEL, D_MODEL)),
                  pl.BlockSpec((1, D_MODEL), lambda i: (0, 0)),
                  full((D_MODEL, FFN_HIDDEN)), full((D_MODEL, FFN_HIDDEN), 1), full((FFN_HIDDEN, D_MODEL))],
        out_specs=rows(D_MODEL),
        out_shape=jax.ShapeDtypeStruct((m, D_MODEL), F32),
        compiler_params=_params("arbitrary"),
        name="merge_ffn",
    )(h2, rest2, rest2, y_sb, y_rw, w_sb, w_rw, w_out, gain.reshape(1, D_MODEL), w_ffn_in, w_ffn_in, w_ffn_out)


def _final_norm_kernel(h_ref, g_ref, o_ref):
    o_ref[...] = _rms_norm(h_ref[...], g_ref[...])


def _final_norm(h3, gain, s):
    b = h3.shape[0]
    tm = _pick(s, (512, 256, 128, 16))
    return pl.pallas_call(
        _final_norm_kernel,
        grid=(b, s // tm),
        in_specs=[pl.BlockSpec((pl.Element(1), pl.Element(tm), pl.Element(D_MODEL)),
                               lambda bi, i: (bi, pl.multiple_of(i * tm + N_META, N_META), 0)),
                  pl.BlockSpec((1, 1, D_MODEL), lambda bi, i: (0, 0, 0))],
        out_specs=pl.BlockSpec((1, tm, D_MODEL), lambda bi, i: (bi, i, 0)),
        out_shape=jax.ShapeDtypeStruct((b, s, D_MODEL), F32),
        compiler_params=_params("arbitrary", "arbitrary"),
        name="final_norm",
    )(h3, gain.reshape(1, 1, D_MODEL))


def _head_block_diag():
    idx = jnp.arange(C_MIX) // HEAD_DIM
    return (idx[:, None] == idx[None, :]).astype(BF16)


def _lora_weight(w_up, a_up, g_up, vres_up):
    w = jnp.zeros((LORA_W, 4 * C_MIX), F32)
    o = 0
    for seg, (mat, width) in enumerate(((w_up, W_LORA), (a_up, A_LORA), (g_up, G_LORA), (vres_up, V_LORA))):
        if mat is not None:
            w = w.at[o:o + width, seg * C_MIX:(seg + 1) * C_MIX].set(mat)
        o += width
    return w.astype(BF16)


def kernel(x, meta_tokens, norm_mix, norm_ffn, norm_final, w_in, mu_rw, w0, w_up, a0, a_up, g_up, k_k, k_a, r_k, ln_x_w, ln_x_b, vres_down, vres_mu, vres_up, vres0, w_sb_out, w_rw_out, w_out, w_ffn_in, w_ffn_out):
    b, s, d = x.shape
    depth = w_in.shape[0]
    l_real = N_META + s
    l_pad = -(-l_real // SB_BLOCK) * SB_BLOCK
    meta = jnp.broadcast_to(meta_tokens.astype(x.dtype)[None], (b, N_META, d))
    h = jnp.concatenate([meta, x, jnp.zeros((b, l_pad - l_real, d), x.dtype)], axis=1)
    h2 = h.reshape(b * l_pad, d)
    bd = _head_block_diag()
    n_in = 3 * C_MIX + RW_COLS
    row = lambda vec: vec.reshape(1, -1)

    vdown_all = jnp.concatenate([jnp.zeros((1, d, V_LORA), F32), vres_down], axis=0)
    w_cat_all = jnp.concatenate([w_in[:, :, :n_in], vdown_all, jnp.zeros((depth, d, C_MIX - LORA_USED), F32),
                                 w_in[:, :, n_in:]], axis=2).astype(BF16)
    w_sb_all, w_rw_all, w_out_all = (w.astype(BF16) for w in (w_sb_out, w_rw_out, w_out))
    w_ffn_in_all, w_ffn_out_all = w_ffn_in.astype(BF16), w_ffn_out.astype(BF16)

    v_first = None
    for layer in range(depth):
        vmu = vres_mu[layer - 1] if layer > 0 else jnp.zeros((V_LORA,), F32)
        mu_main = row(mu_rw[layer, :3 * C_MIX])
        mu_lora = row(jnp.concatenate([mu_rw[layer, 3 * C_MIX:], vmu, jnp.zeros((LORA_W - LORA_USED,), F32)]))
        w_lora = _lora_weight(w_up[layer], a_up[layer], g_up[layer], vres_up[layer - 1] if layer > 0 else None)

        rest2, y_sb = _proj_attn(h2, norm_mix[layer], w_cat_all, layer, l_pad)
        y_rw, v_out = _rw_mix(
            rest2.reshape(b, l_pad, N_REST), v_first, mu_main, mu_lora,
            row(w0[layer]), row(a0[layer]),
            row(k_k[layer]), row(k_a[layer]), row(vres0[layer - 1]) if layer > 0 else None, w_lora, bd,
            row(ln_x_w[layer]), row(ln_x_b[layer]), row(r_k[layer].reshape(-1)))
        if layer == 0:
            v_first = v_out
        flat = lambda t: t.reshape(b * l_pad, C_MIX)
        h2 = _merge_ffn(h2, rest2, y_sb, flat(y_rw), w_sb_all, w_rw_all, w_out_all,
                        norm_ffn[layer], w_ffn_in_all, w_ffn_out_all, layer)

    return _final_norm(h2.reshape(b, l_pad, d), norm_final, s)
```

```python
import functools

import jax
import jax.numpy as jnp
from jax import lax
from jax.experimental import pallas as pl
from jax.experimental.pallas import tpu as pltpu

D_MODEL = 1024
HEAD_DIM = 64
N_HEADS = 8
C_MIX = N_HEADS * HEAD_DIM
N_META = 16
SB_BLOCK = 128
W_LORA, A_LORA, V_LORA, G_LORA = 64, 64, 32, 160
RW_COLS = 3 * C_MIX + W_LORA + A_LORA + G_LORA
FFN_HIDDEN = 2816
RMS_EPS = 1e-6
GN_EPS = 64e-5
RW_CHUNK = 128
SB_UNDERFLOW = 104.0
SB_EAGER_BLOCKS = 2

N_QKV = 3 * C_MIX
COL_RW = 0
COL_LORA = 1536
COL_GATES = 2048
N_REST = 4096
LORA_USED = W_LORA + A_LORA + G_LORA + V_LORA
LORA_W = 384
assert W_LORA + A_LORA == 128 and 256 <= W_LORA + A_LORA + G_LORA and LORA_USED <= LORA_W

V7X_VMEM_LIMIT = 56 * 1024 * 1024

F32 = jnp.float32
BF16 = jnp.bfloat16


def _pick(n, cands):
    for c in cands:
        if n % c == 0:
            return c
    raise ValueError(f"no tile for {n} in {cands}")


def _params(*sem):
    return pltpu.CompilerParams(dimension_semantics=sem, vmem_limit_bytes=V7X_VMEM_LIMIT)


def _mm(a, b):
    return jnp.dot(a.astype(BF16), b.astype(BF16), preferred_element_type=F32)


def _mm_nt(a, b):
    return lax.dot_general(a.astype(BF16), b.astype(BF16), (((1,), (1,)), ((), ())),
                           preferred_element_type=F32)


def _mm_tn(a, b):
    return lax.dot_general(a.astype(BF16), b.astype(BF16), (((0,), (0,)), ((), ())),
                           preferred_element_type=F32)


def _each(fn, *lists):
    return [fn(*xs) for xs in zip(*lists)]


def _split_dot(x, w_bf16):
    hi = x.astype(BF16)
    lo = (x - hi.astype(F32)).astype(BF16)
    return (jnp.dot(hi, w_bf16, preferred_element_type=F32)
            + jnp.dot(lo, w_bf16, preferred_element_type=F32))


def _prefix_dot(tri_bf16, x):
    hi = x.astype(BF16)
    rem = x - hi.astype(F32)
    mid = rem.astype(BF16)
    lo = (rem - mid.astype(F32)).astype(BF16)
    return (jnp.dot(tri_bf16, hi, preferred_element_type=F32) + jnp.dot(tri_bf16, mid, preferred_element_type=F32)
            + jnp.dot(tri_bf16, lo, preferred_element_type=F32))


def _softplus(u):
    return jnp.maximum(u, 0.0) + jnp.log(1.0 + jnp.exp(-jnp.abs(u)))


def _sigmoid(u):
    return 1.0 / (1.0 + jnp.exp(-u))


def _rms_norm(x, gain):
    ms = jnp.mean(x * x, axis=-1, keepdims=True)
    return x * lax.rsqrt(ms + RMS_EPS) * gain


def _sb_unit(q, k_ref, v_ref, lm, o_ref, qi, tq):
    n_pairs = N_HEADS // 2
    pw = 2 * HEAD_DIM
    lane = lax.broadcasted_iota(jnp.int32, (1, pw), 1)
    head_a = lane < HEAD_DIM
    zero_bf = jnp.zeros((), BF16)
    split_rows = lambda x: jnp.concatenate([jnp.where(head_a, x, zero_bf), jnp.where(head_a, zero_bf, x)], axis=0)
    q = q * jnp.asarray(HEAD_DIM ** -0.5, BF16)
    q2 = [split_rows(q[:, p * pw:(p + 1) * pw]) for p in range(n_pairs)]
    row = lax.broadcasted_iota(jnp.int32, (2 * tq, tq), 0)
    col = lax.broadcasted_iota(jnp.int32, (2 * tq, tq), 1)
    causal2 = col < jnp.where(row >= tq, row - tq, row)
    state = {}

    def visit_steps(blocks, r_run, acc):
        chains = [(bi, p) for bi in range(len(blocks)) for p in range(n_pairs)]
        starts = [pl.multiple_of(blk[0] * tq, tq) for blk in blocks]
        ks = [k_ref[0, pl.ds(st, tq), :] for st in starts]
        vs = [v_ref[0, pl.ds(st, tq), :] for st in starts]
        vs = [v if blk[2] is None else jnp.where(blk[2], v, zero_bf) for v, blk in zip(vs, blocks)]
        z = [lax.dot_general(q2[p], ks[bi][:, p * pw:(p + 1) * pw], (((1,), (1,)), ((), ())),
                             preferred_element_type=F32) for bi, p in chains]
        yield
        sp = [_softplus(zi) for zi in z]
        sp = [jnp.where(causal2, s, 0.0) if blocks[bi][1] else s for s, (bi, p) in zip(sp, chains)]
        hi = [s.astype(BF16) for s in sp]
        lo = [(s - h.astype(F32)).astype(BF16) for s, h in zip(sp, hi)]
        wm = [jnp.dot(jnp.concatenate([h, l_], axis=1), lm, preferred_element_type=F32) for h, l_ in zip(hi, lo)]
        yield
        r_run, acc = list(r_run), list(acc)
        for i, (bi, p) in enumerate(chains):
            a = jnp.exp(z[i] - sp[i] - wm[i][:, :tq] - r_run[p])
            if blocks[bi][1]:
                a = jnp.where(causal2, a, 0.0)
            a = a.astype(BF16)
            v2 = split_rows(vs[bi][:, p * pw:(p + 1) * pw])
            acc[p] = acc[p] + jnp.dot(jnp.concatenate([a[:tq], a[tq:]], axis=1), v2, preferred_element_type=F32)
            r_run[p] = r_run[p] + wm[i][:, tq:]
        state["r_run"], state["acc"] = r_run, acc

    eager_blocks = [(qi, True, None)] + [(jnp.maximum(qi - back, 0), False, qi >= back)
                                         for back in range(1, SB_EAGER_BLOCKS + 1)]
    eager = visit_steps(eager_blocks, [jnp.zeros((2 * tq, tq), F32)] * n_pairs,
                        [jnp.zeros((tq, pw), F32)] * n_pairs)

    def visit(blocks, r_run, acc):
        for _ in visit_steps(blocks, r_run, acc):
            pass
        return state["r_run"], state["acc"]

    def store(acc):
        o_ref[...] = jnp.concatenate(acc, axis=1).astype(o_ref.dtype)

    return eager, state, visit, store


def _sb_finish(units, qis):
    def more(c):
        back, rrs, _ = c
        wants = [jnp.logical_and(qi - back >= 0,
                                 jnp.min(functools.reduce(jnp.minimum, [r[:, :1] for r in rr])) < SB_UNDERFLOW)
                 for qi, rr in zip(qis, rrs)]
        return functools.reduce(jnp.logical_or, wants)

    def far(c):
        back, rrs, acs = c
        new = [visit([(jnp.maximum(qi - back, 0), False, qi - back >= 0)], rr, ac)
               for (_, _, visit, _), qi, rr, ac in zip(units, qis, rrs, acs)]
        return back + 1, [n[0] for n in new], [n[1] for n in new]

    start = (SB_EAGER_BLOCKS + 1, [u[1]["r_run"] for u in units], [u[1]["acc"] for u in units])
    _, _, acs = lax.while_loop(more, far, start)
    for (_, _, _, store), acc in zip(units, acs):
        store(acc)


def _later_and_ones(tq):
    s_from = jnp.arange(2 * tq)[:, None] % tq
    s_to = jnp.arange(2 * tq)[None, :]
    return jnp.where(s_to < tq, s_from > s_to, True).astype(BF16)


def _proj_attn_kernel(h_ref, g_ref, w_ref, lm_ref, rest_ref, ysb_ref, k_scr, v_scr, *, blocks_per_batch, tq):
    tm = h_ref.shape[0]
    block = pl.program_id(0) % blocks_per_batch
    row0 = pl.multiple_of(block * tm, tq)
    xn = _rms_norm(h_ref[...], g_ref[...]).astype(BF16)
    qkv = jnp.dot(xn, w_ref[:, :N_QKV], preferred_element_type=F32).astype(BF16)
    k_scr[0, pl.ds(row0, tm), :] = qkv[:, C_MIX:2 * C_MIX]
    v_scr[0, pl.ds(row0, tm), :] = qkv[:, 2 * C_MIX:3 * C_MIX]
    lm = lm_ref[...]
    n_units = tm // tq
    qis = [block * n_units + u for u in range(n_units)]
    units = [_sb_unit(qkv[u * tq:(u + 1) * tq, 0:C_MIX], k_scr, v_scr, lm, ysb_ref.at[u], qis[u], tq)
             for u in range(n_units)]
    n_rest = rest_ref.shape[1]
    n_chunks = 4
    cw = n_rest // n_chunks
    for ch in range(n_chunks):
        if ch < 3:
            for unit in units:
                next(unit[0], None)
        cols = slice(N_QKV + ch * cw, N_QKV + (ch + 1) * cw)
        rest_ref[:, ch * cw:(ch + 1) * cw] = jnp.dot(xn, w_ref[:, cols], preferred_element_type=F32).astype(BF16)
    _sb_finish(units, qis)


def _proj_attn(h2, gain, w_cat_all, layer, l):
    m = h2.shape[0]
    n = w_cat_all.shape[2]
    tq = SB_BLOCK
    tm = _pick(l, (3 * tq, 2 * tq, tq))
    rest, y_sb = pl.pallas_call(
        functools.partial(_proj_attn_kernel, blocks_per_batch=l // tm, tq=tq),
        grid=(m // tm,),
        in_specs=[pl.BlockSpec((tm, D_MODEL), lambda i: (i, 0)),
                  pl.BlockSpec((1, D_MODEL), lambda i: (0, 0)),
                  pl.BlockSpec((None, D_MODEL, n), lambda i: (layer, 0, 0), pipeline_mode=pl.Buffered(1)),
                  pl.BlockSpec((2 * tq, 2 * tq), lambda i: (0, 0))],
        out_specs=[pl.BlockSpec((tm, n - N_QKV), lambda i: (i, 0)),
                   pl.BlockSpec((tm // tq, tq, C_MIX), lambda i: (i, 0, 0))],
        out_shape=[jax.ShapeDtypeStruct((m, n - N_QKV), BF16),
                   jax.ShapeDtypeStruct((m // tq, tq, C_MIX), BF16)],
        scratch_shapes=[pltpu.VMEM((1, l, C_MIX), BF16), pltpu.VMEM((1, l, C_MIX), BF16)],
        compiler_params=_params("arbitrary"),
        name="proj_attn",
    )(h2, gain.reshape(1, D_MODEL), w_cat_all, _later_and_ones(tq))
    return rest, y_sb.reshape(m, C_MIX)


def _unit_lower_inverse(a_strict, rowi, coli):
    eye = (rowi == coli).astype(F32)
    same = lambda sh: (rowi >> sh) == (coli >> sh)
    size = a_strict[0].shape[0]
    levels = [same(sh) for sh in range(3, size.bit_length() - 1)] + [None]
    m8 = levels[0]
    n1 = [-jnp.where(m8, a, 0.0) for a in a_strict]
    n2 = _each(_mm, n1, n1)
    n4 = _each(_mm, n2, n2)
    t = [eye + n for n in n1]
    t = _each(lambda ti, ni: ti + _mm(ti, ni), t, n2)
    t = _each(lambda ti, ni: ti + _mm(ti, ni), t, n4)
    for inner, outer in zip(levels[:-1], levels[1:]):
        off = jnp.logical_not(inner) if outer is None else jnp.logical_and(outer, jnp.logical_not(inner))
        ta = _each(lambda ti, a: _mm(ti, jnp.where(off, a, 0.0)), t, a_strict)
        t = _each(lambda ti, tai: ti - _mm(tai, ti), t, ta)
    return t


def _rw_mix_kernel(*refs, has_vres):
    if has_vres:
        (main_ref, lora_ref, vfirst_ref, mu_main_ref, mu_lora_ref, w0_ref, a0_ref, kk_ref, ka_ref, vres0_ref,
         wl_ref, bd_ref, lnw_ref, lnb_ref, rk_ref, y_ref, s_ref, pm_ref, pl_ref) = refs
        vout_ref = None
    else:
        (main_ref, lora_ref, mu_main_ref, mu_lora_ref, w0_ref, a0_ref, kk_ref, ka_ref,
         wl_ref, bd_ref, lnw_ref, lnb_ref, rk_ref, y_ref, vout_ref, s_ref, pm_ref, pl_ref) = refs
    c = RW_CHUNK
    n_batch = main_ref.shape[0]
    n_chains = n_batch * N_HEADS

    @pl.when(pl.program_id(0) == 0)
    def _():
        s_ref[...] = jnp.zeros_like(s_ref)
        pm_ref[...] = jnp.zeros_like(pm_ref)
        pl_ref[...] = jnp.zeros_like(pl_ref)

    first_row = lax.broadcasted_iota(jnp.int32, (c, 1), 0) == 0

    def shifted(x, prev8, mu):
        prev = jnp.where(first_row, prev8[7:8, :], pltpu.roll(x, shift=1, axis=0))
        return x + (prev - x) * mu

    mains = [main_ref[bi].astype(F32) for bi in range(n_batch)]
    loras = [lora_ref[bi].astype(F32) for bi in range(n_batch)]
    xs = jnp.concatenate([shifted(x, pm_ref[bi], mu_main_ref[...]) for bi, x in enumerate(mains)], axis=0)
    lo = jnp.concatenate([shifted(x, pl_ref[bi], mu_lora_ref[...]) for bi, x in enumerate(loras)], axis=0)
    for bi in range(n_batch):
        pm_ref[bi] = mains[bi][c - 8:c, :]
        pl_ref[bi] = loras[bi][c - 8:c, :]

    r = xs[:, 0:C_MIX]
    kr = xs[:, C_MIX:2 * C_MIX]
    vr = xs[:, 2 * C_MIX:3 * C_MIX]
    lane = lax.broadcasted_iota(jnp.int32, (1, 128), 1)
    g_tail = W_LORA + A_LORA + G_LORA - 256
    act = jnp.concatenate([jnp.where(lane < W_LORA, jnp.tanh(lo[:, 0:128]), lo[:, 0:128]),
                           _sigmoid(lo[:, 128:256]),
                           jnp.where(lane < g_tail, _sigmoid(lo[:, 256:384]), lo[:, 256:384])], axis=1)
    pre = jnp.dot(act.astype(BF16), wl_ref[...], preferred_element_type=F32)
    w_log = -_softplus(-(w0_ref[...] + pre[:, 0:C_MIX])) - 0.5
    ld = -jnp.exp(w_log)
    a = _sigmoid(a0_ref[...] + pre[:, C_MIX:2 * C_MIX])
    g = pre[:, 2 * C_MIX:3 * C_MIX]
    if has_vres:
        v_first = jnp.concatenate([vfirst_ref[bi] for bi in range(n_batch)], axis=0)
        vr = vr + (v_first - vr) * _sigmoid(vres0_ref[...] + pre[:, 3 * C_MIX:4 * C_MIX])
    else:
        for bi in range(n_batch):
            vout_ref[bi] = vr[bi * c:(bi + 1) * c]
    bd = bd_ref[...]
    kk = kr * kk_ref[...]
    kk = kk * lax.rsqrt(jnp.maximum(_split_dot(kk * kk, bd), 1e-24))
    k = kr * (1.0 + (a - 1.0) * ka_ref[...])
    bvec = kk * a

    rowi = lax.broadcasted_iota(jnp.int32, (c, c), 0)
    coli = lax.broadcasted_iota(jnp.int32, (c, c), 1)
    lower_incl = coli <= rowi
    lower_strict = coli < rowi
    tri = lower_incl.astype(BF16)
    rows = lambda x, bi: x[bi * c:(bi + 1) * c]
    kt_all, bt_all, kn_all, rt_all, bh_all, kh_all, g_end = [], [], [], [], [], [], []
    for bi in range(n_batch):
        ld_b = rows(ld, bi)
        cum = _prefix_dot(tri, ld_b)
        cum_end = cum[c - 1:c, :]
        e_neg = jnp.exp(-cum)
        to_end = jnp.exp(cum_end - cum)
        kt_all.append(rows(kk, bi) * jnp.exp(cum - ld_b))
        bt_all.append(rows(bvec, bi) * e_neg)
        kn_all.append(rows(k, bi) * e_neg)
        rt_all.append(rows(r, bi) * jnp.exp(cum))
        bh_all.append(rows(bvec, bi) * to_end)
        kh_all.append(rows(k, bi) * to_end)
        g_end.append(jnp.exp(cum_end))
    v_all = [rows(vr, bi) for bi in range(n_batch)]

    heads = lambda xb: [x[:, h * HEAD_DIM:(h + 1) * HEAD_DIM] for x in xb for h in range(N_HEADS)]
    kt, bt, kn, rt, bh, kh, v, g_h = (heads(x) for x in (kt_all, bt_all, kn_all, rt_all, bh_all, kh_all, v_all,
                                                        g_end))
    hd = HEAD_DIM
    kt_rt = _each(lambda x, y: jnp.concatenate([x, y], axis=0), kt, rt)
    on_b = _each(_mm_nt, kt_rt, bt)
    on_k = _each(_mm_nt, kt_rt, kn)
    a_b = [jnp.where(lower_strict, x[:c], 0.0) for x in on_b]
    a_rb = [jnp.where(lower_incl, x[c:], 0.0) for x in on_b]
    a_k = [jnp.where(lower_strict, x[:c], 0.0) for x in on_k]
    a_rk = [jnp.where(lower_incl, x[c:], 0.0) for x in on_k]
    av = _each(lambda x, y, vv: _mm(jnp.concatenate([x, y], axis=0), vv), a_k, a_rk, v)
    akv = [x[:c] for x in av]
    arkv = [x[c:] for x in av]
    vkh = _each(_mm_tn, v, kh)
    t_inv = _unit_lower_inverse(a_b, rowi, coli)
    ku =_each(lambda t_, x, y: _mm(t_, jnp.concatenate([x, y], axis=1)), t_inv, kt, akv)
    ry = _each(_mm, a_rb, ku)
    r_hat = _each(lambda x, z: x - z[:, :hd], rt, ry)
    y_bar = _each(lambda x, z: x - z[:, hd:], arkv, ry)
    eh = _each(_mm_tn, ku, bh)
    e_bar = [x[:hd] for x in eh]
    h_add = _each(lambda x, z: x - z[hd:], vkh, eh)
    s = [s_ref[h] for h in range(n_chains)]
    ys = _each(lambda rh, sh, yb: _mm_nt(rh, sh) + yb, r_hat, s, y_bar)
    s_new = _each(lambda sh, gh, eb, ha: sh * gh - _mm(sh, eb) + ha, s, g_h, e_bar, h_add)
    for h in range(n_chains):
        s_ref[h] = s_new[h]

    y = jnp.concatenate([jnp.concatenate(ys[bi * N_HEADS:(bi + 1) * N_HEADS], axis=-1) for bi in range(n_batch)],
                        axis=0)
    inv_n = 1.0 / HEAD_DIM
    mean = _split_dot(y, bd) * inv_n
    d = y - mean
    var = _split_dot(d * d, bd) * inv_n
    yn = d * lax.rsqrt(var + GN_EPS) * lnw_ref[...] + lnb_ref[...]
    bonus = _split_dot(r * k * rk_ref[...], bd) * vr
    out = (yn + bonus) * g
    for bi in range(n_batch):
        y_ref[bi] = out[bi * c:(bi + 1) * c].astype(y_ref.dtype)


def _rw_mix(rest3, v_first, mu_main, mu_lora, w0, a0, k_k, k_a, vres0, w_lora, bd, ln_w, ln_b, r_k):
    b, l, _ = rest3.shape
    has_vres = v_first is not None
    c = RW_CHUNK
    tok =pl.BlockSpec((b, c, C_MIX), lambda ci: (0, ci, 0))
    row = lambda width: pl.BlockSpec((1, width), lambda ci: (0, 0))
    in_specs = [pl.BlockSpec((b, c, 3 * C_MIX), lambda ci: (0, ci, COL_RW // (3 * C_MIX))),
                pl.BlockSpec((b, c, LORA_W), lambda ci: (0, ci, COL_LORA // LORA_W))]
    args = [rest3, rest3]
    if has_vres:
        in_specs.append(tok)
        args.append(v_first)
    in_specs += [row(3 * C_MIX), row(LORA_W)] + [row(C_MIX)] * 4
    args += [mu_main, mu_lora, w0, a0, k_k, k_a]
    if has_vres:
        in_specs.append(row(C_MIX))
        args.append(vres0)
    in_specs += [pl.BlockSpec((LORA_W, 4 * C_MIX), lambda ci: (0, 0)), pl.BlockSpec((C_MIX, C_MIX), lambda ci: (0, 0)),
                 row(C_MIX), row(C_MIX), row(C_MIX)]
    args += [w_lora, bd, ln_w, ln_b, r_k]
    out_shape = [jax.ShapeDtypeStruct((b, l, C_MIX), BF16)]
    out_specs = [tok]
    if not has_vres:
        out_shape.append(jax.ShapeDtypeStruct((b, l, C_MIX), F32))
        out_specs.append(tok)
    res = pl.pallas_call(
        functools.partial(_rw_mix_kernel, has_vres=has_vres),
        grid=(l // c,),
        in_specs=in_specs,
        out_specs=out_specs,
        out_shape=out_shape,
        scratch_shapes=[pltpu.VMEM((b * N_HEADS, HEAD_DIM, HEAD_DIM), F32),
                        pltpu.VMEM((b, 8, 3 * C_MIX), F32), pltpu.VMEM((b, 8, LORA_W), F32)],
        compiler_params=_params("arbitrary"),
        name="rw_mix",
    )(*args)
    return (res[0], None) if has_vres else (res[0], res[1])


def _merge_ffn_kernel(h_ref, gsb_ref, grw_ref, ysb_ref, yrw_ref, wsb_ref, wrw_ref, wout_ref,
                      g_ref, wg_ref, wu_ref, wo_ref, o_ref):
    o_sb = jnp.dot(ysb_ref[...], wsb_ref[...], preferred_element_type=F32)
    o_rw = jnp.dot(yrw_ref[...], wrw_ref[...], preferred_element_type=F32)
    merged = _sigmoid(gsb_ref[...].astype(F32)) * o_sb + _sigmoid(grw_ref[...].astype(F32)) * o_rw
    x = h_ref[...] + jnp.dot(merged.astype(BF16), wout_ref[...], preferred_element_type=F32)
    hn = _rms_norm(x, g_ref[...]).astype(BF16)
    gate = jnp.dot(hn, wg_ref[...], preferred_element_type=F32)
    up = jnp.dot(hn, wu_ref[...], preferred_element_type=F32)
    act = gate * _sigmoid(gate) * up
    o_ref[...] = x + jnp.dot(act.astype(BF16), wo_ref[...], preferred_element_type=F32)


def _merge_ffn(h2, rest2, y_sb, y_rw, w_sb, w_rw, w_out, gain, w_ffn_in, w_ffn_out, layer):
    m = h2.shape[0]
    tm = _pick(m, (384, 256, 128))
    rows = lambda width: pl.BlockSpec((tm, width), lambda i: (i, 0))
    full = lambda shape, col=0: pl.BlockSpec((None,) + shape, lambda i: (layer, 0, col),
                                             pipeline_mode=pl.Buffered(1))
    return pl.pallas_call(
        _merge_ffn_kernel,
        grid=(m // tm,),
        in_specs=[rows(D_MODEL),
                  pl.BlockSpec((tm, D_MODEL), lambda i: (i, COL_GATES // D_MODEL)),
                  pl.BlockSpec((tm, D_MODEL), lambda i: (i, COL_GATES // D_MODEL + 1)),
                  rows(C_MIX), rows(C_MIX),
                  full((C_MIX, D_MODEL)), full((C_MIX, D_MODEL)), full((D_MODEL, D_MODEL)),
                  pl.BlockSpec((1, D_MODEL), lambda i: (0, 0)),
                  full((D_MODEL, FFN_HIDDEN)), full((D_MODEL, FFN_HIDDEN), 1), full((FFN_HIDDEN, D_MODEL))],
        out_specs=rows(D_MODEL),
        out_shape=jax.ShapeDtypeStruct((m, D_MODEL), F32),
        compiler_params=_params("arbitrary"),
        name="merge_ffn",
    )(h2, rest2, rest2, y_sb, y_rw, w_sb, w_rw, w_out, gain.reshape(1, D_MODEL), w_ffn_in, w_ffn_in, w_ffn_out)


def _final_norm_kernel(h_ref, g_ref, o_ref):
    o_ref[...] = _rms_norm(h_ref[...], g_ref[...])


def _final_norm(h3, gain, s):
    b = h3.shape[0]
    tm = _pick(s, (512, 256, 128, 16))
    return pl.pallas_call(
        _final_norm_kernel,
        grid=(b, s // tm),
        in_specs=[pl.BlockSpec((pl.Element(1), pl.Element(tm), pl.Element(D_MODEL)),
                               lambda bi, i: (bi, pl.multiple_of(i * tm + N_META, N_META), 0)),
                  pl.BlockSpec((1, 1, D_MODEL), lambda bi, i: (0, 0, 0))],
        out_specs=pl.BlockSpec((1, tm, D_MODEL), lambda bi, i: (bi, i, 0)),
        out_shape=jax.ShapeDtypeStruct((b, s, D_MODEL), F32),
        compiler_params=_params("arbitrary", "arbitrary"),
        name="final_norm",
    )(h3, gain.reshape(1, 1, D_MODEL))


def _head_block_diag():
    idx = jnp.arange(C_MIX) // HEAD_DIM
    return (idx[:, None] == idx[None, :]).astype(BF16)


def _lora_weight(w_up, a_up, g_up, vres_up):
    w = jnp.zeros((LORA_W, 4 * C_MIX), F32)
    o = 0
    for seg, (mat, width) in enumerate(((w_up, W_LORA), (a_up, A_LORA), (g_up, G_LORA), (vres_up, V_LORA))):
        if mat is not None:
            w = w.at[o:o + width, seg * C_MIX:(seg + 1) * C_MIX].set(mat)
        o += width
    return w.astype(BF16)


def kernel(x, meta_tokens, norm_mix, norm_ffn, norm_final, w_in, mu_rw, w0, w_up, a0, a_up, g_up, k_k, k_a, r_k, ln_x_w, ln_x_b, vres_down, vres_mu, vres_up, vres0, w_sb_out, w_rw_out, w_out, w_ffn_in, w_ffn_out):
    b, s, d = x.shape
    depth = w_in.shape[0]
    l_real = N_META + s
    l_pad = -(-l_real // SB_BLOCK) * SB_BLOCK
    meta = jnp.broadcast_to(meta_tokens.astype(x.dtype)[None], (b, N_META, d))
    h = jnp.concatenate([meta, x, jnp.zeros((b, l_pad - l_real, d), x.dtype)], axis=1)
    h2 = h.reshape(b * l_pad, d)
    bd = _head_block_diag()
    n_in = 3 * C_MIX + RW_COLS
    row = lambda vec: vec.reshape(1, -1)

    vdown_all = jnp.concatenate([jnp.zeros((1, d, V_LORA), F32), vres_down], axis=0)
    w_cat_all = jnp.concatenate([w_in[:, :, :n_in], vdown_all, jnp.zeros((depth, d, C_MIX - LORA_USED), F32),
                                 w_in[:, :, n_in:]], axis=2).astype(BF16)
    w_sb_all, w_rw_all, w_out_all = (w.astype(BF16) for w in (w_sb_out, w_rw_out, w_out))
    w_ffn_in_all, w_ffn_out_all = w_ffn_in.astype(BF16), w_ffn_out.astype(BF16)

    v_first = None
    for layer in range(depth):
        vmu = vres_mu[layer - 1] if layer > 0 else jnp.zeros((V_LORA,), F32)
        mu_main = row(mu_rw[layer, :3 * C_MIX])
        mu_lora = row(jnp.concatenate([mu_rw[layer, 3 * C_MIX:], vmu, jnp.zeros((LORA_W - LORA_USED,), F32)]))
        w_lora = _lora_weight(w_up[layer], a_up[layer], g_up[layer], vres_up[layer - 1] if layer > 0 else None)

        rest2, y_sb = _proj_attn(h2, norm_mix[layer], w_cat_all, layer, l_pad)
        y_rw, v_out = _rw_mix(
            rest2.reshape(b, l_pad, N_REST), v_first, mu_main, mu_lora,
            row(w0[layer]), row(a0[layer]),
            row(k_k[layer]), row(k_a[layer]), row(vres0[layer - 1]) if layer > 0 else None, w_lora, bd,
            row(ln_x_w[layer]), row(ln_x_b[layer]), row(r_k[layer].reshape(-1)))
        if layer == 0:
            v_first = v_out
        flat = lambda t: t.reshape(b * l_pad, C_MIX)
        h2 = _merge_ffn(h2, rest2, y_sb, flat(y_rw), w_sb_all, w_rw_all, w_out_all,
                        norm_ffn[layer], w_ffn_in_all, w_ffn_out_all, layer)

    return _final_norm(h2.reshape(b, l_pad, d), norm_final, s)
```
